```python
import math
import jax, jax.numpy as jnp
from jax import lax
import numpy as np

D_MODEL = 1024
BATCH = 2
SEQ = 16384
DEPTH = 1

N_META = 16
GRID_W = 64
HEAD_DIM = 128
N_Q_HEADS = 4
N_KV_HEADS = 2
Q_GROUP = N_Q_HEADS // N_KV_HEADS
ATTN_W = N_Q_HEADS * HEAD_DIM
KV_W = N_KV_HEADS * HEAD_DIM
N_FOURIER_GROUPS = 4
FOURIER_GROUP_W = 128
FOURIER_W = N_FOURIER_GROUPS * FOURIER_GROUP_W
MIX_W = ATTN_W + FOURIER_W
IN_PROJ_W = ATTN_W + 2 * KV_W + FOURIER_W
D_FF = 4 * D_MODEL
Q_BLOCK = 128
ROPE_THETA = 10000.0
ROPE_AXIS_DIM = HEAD_DIM // 2
RMS_EPS = 1e-6

kernel_name = 'hymba_axial_gqa_fnet_encoder_block'


def rms_norm(x, g):
    xf = x.astype(jnp.float32)
    y = xf * lax.rsqrt(jnp.mean(xf * xf, axis=-1, keepdims=True) + RMS_EPS)
    return (y * g.astype(jnp.float32)).astype(x.dtype)


def grid_positions(n_tok):
    rows_count = n_tok // GRID_W
    real_row = jnp.repeat(jnp.arange(rows_count, dtype=jnp.float32), GRID_W)
    real_col = jnp.tile(jnp.arange(GRID_W, dtype=jnp.float32), rows_count)
    meta_row = jnp.full((N_META,), -1.0, dtype=jnp.float32)
    meta_col = jnp.arange(N_META, dtype=jnp.float32)
    return jnp.concatenate([meta_row, real_row]), jnp.concatenate([meta_col, real_col])


def rope_angles(pos):
    inv_freq = ROPE_THETA ** (-jnp.arange(0, ROPE_AXIS_DIM, 2, dtype=jnp.float32) / ROPE_AXIS_DIM)
    ang = pos[:, None] * inv_freq[None, :]
    return jnp.cos(ang), jnp.sin(ang)


def _rotate(x, cos, sin):
    c = cos[None, :, None, :]
    s = sin[None, :, None, :]
    x1, x2 = jnp.split(x, 2, axis=-1)
    return jnp.concatenate([x1 * c - x2 * s, x2 * c + x1 * s], axis=-1)


def axial_rope(x, cos_r, sin_r, cos_c, sin_c):
    xf = x.astype(jnp.float32)
    xr = _rotate(xf[..., :ROPE_AXIS_DIM], cos_r, sin_r)
    xc = _rotate(xf[..., ROPE_AXIS_DIM:], cos_c, sin_c)
    return jnp.concatenate([xr, xc], axis=-1).astype(x.dtype)


def block_attention(q, k, v):
    b, l, _, d = q.shape
    scale = 1.0 / math.sqrt(d)
    qt = q.reshape(b, l, N_KV_HEADS, Q_GROUP, d).transpose(0, 2, 3, 1, 4)
    kt = k.transpose(0, 2, 1, 3)
    vt = v.transpose(0, 2, 1, 3)

    def attend(qb):
        s = jnp.einsum('bkgqd,bksd->bkgqs', qb, kt).astype(jnp.float32) * scale
        p = jax.nn.softmax(s, axis=-1).astype(vt.dtype)
        return jnp.einsum('bkgqs,bksd->bkgqd', p, vt)

    out_meta = attend(qt[:, :, :, :N_META])
    n_real = l - N_META
    nb = n_real // Q_BLOCK
    qr = qt[:, :, :, N_META:].reshape(b, N_KV_HEADS, Q_GROUP, nb, Q_BLOCK, d)
    qr = jnp.moveaxis(qr, 3, 0)
    out_real = lax.map(attend, qr)
    out_real = jnp.moveaxis(out_real, 0, 3).reshape(b, N_KV_HEADS, Q_GROUP, n_real, d)
    out = jnp.concatenate([out_meta, out_real], axis=3)
    return out.transpose(0, 3, 1, 2, 4).reshape(b, l, ATTN_W)


def fourier_mix(u, w_f):
    b, l, _ = u.shape
    ug = u.reshape(b, l, N_FOURIER_GROUPS, FOURIER_GROUP_W).astype(jnp.float32)
    f = jnp.fft.fft2(ug, axes=(1, 3), norm='ortho').real.astype(u.dtype)
    y = jnp.einsum('blgc,gcd->blgd', f, w_f)
    return y.reshape(b, l, FOURIER_W)


def setup_inputs(seed: int = 0) -> dict:
    key = jax.random.key(seed)
    ks = jax.random.split(key, 16)
    f32 = jnp.float32

    def gain(k, shape):
        return 1.0 + 0.02 * jax.random.normal(k, shape, f32)

    x = jax.random.normal(ks[0], (BATCH, SEQ, D_MODEL), f32)
    meta_tokens = jax.random.normal(ks[1], (N_META, D_MODEL), f32)
    g_mix = gain(ks[2], (DEPTH, D_MODEL))
    w_in = jax.random.normal(ks[3], (DEPTH, D_MODEL, IN_PROJ_W), f32) * D_MODEL ** -0.5
    g_q = gain(ks[4], (DEPTH, HEAD_DIM))
    g_k = gain(ks[5], (DEPTH, HEAD_DIM))
    w_fourier = jax.random.normal(ks[6], (DEPTH, N_FOURIER_GROUPS, FOURIER_GROUP_W, FOURIER_GROUP_W), f32) * FOURIER_GROUP_W ** -0.5
    g_attn_out = gain(ks[7], (DEPTH, ATTN_W))
    g_fourier_out = gain(ks[8], (DEPTH, FOURIER_W))
    w_out = jax.random.normal(ks[9], (DEPTH, MIX_W, D_MODEL), f32) * MIX_W ** -0.5
    g_mlp = gain(ks[10], (DEPTH, D_MODEL))
    w_up = jax.random.normal(ks[11], (DEPTH, D_MODEL, D_FF), f32) * D_MODEL ** -0.5
    w_down = jax.random.normal(ks[12], (DEPTH, D_FF, D_MODEL), f32) * D_FF ** -0.5
    g_final = gain(ks[13], (D_MODEL,))
    return {'x': x, 'meta_tokens': meta_tokens, 'g_mix': g_mix, 'w_in': w_in, 'g_q': g_q, 'g_k': g_k,
            'w_fourier': w_fourier, 'g_attn_out': g_attn_out, 'g_fourier_out': g_fourier_out, 'w_out': w_out,
            'g_mlp': g_mlp, 'w_up': w_up, 'w_down': w_down, 'g_final': g_final}


def reference(x, meta_tokens, g_mix, w_in, g_q, g_k, w_fourier, g_attn_out, g_fourier_out, w_out,
              g_mlp, w_up, w_down, g_final):
    b, n_tok, d = x.shape
    meta = jnp.broadcast_to(meta_tokens[None].astype(x.dtype), (b, N_META, d))
    h = jnp.concatenate([meta, x], axis=1)
    l = h.shape[1]

    row, col = grid_positions(n_tok)
    cos_r, sin_r = rope_angles(row)
    cos_c, sin_c = rope_angles(col)

    for i in range(DEPTH):
        hn = rms_norm(h, g_mix[i])
        proj = hn @ w_in[i]
        q = proj[..., :ATTN_W].reshape(b, l, N_Q_HEADS, HEAD_DIM)
        k = proj[..., ATTN_W:ATTN_W + KV_W].reshape(b, l, N_KV_HEADS, HEAD_DIM)
        v = proj[..., ATTN_W + KV_W:ATTN_W + 2 * KV_W].reshape(b, l, N_KV_HEADS, HEAD_DIM)
        u = proj[..., ATTN_W + 2 * KV_W:]
        q = axial_rope(rms_norm(q, g_q[i]), cos_r, sin_r, cos_c, sin_c)
        k = axial_rope(rms_norm(k, g_k[i]), cos_r, sin_r, cos_c, sin_c)
        attn = block_attention(q, k, v)
        four = fourier_mix(u, w_fourier[i])
        mixed = jnp.concatenate([rms_norm(attn, g_attn_out[i]), rms_norm(four, g_fourier_out[i])], axis=-1)
        h = h + mixed @ w_out[i]
        m = rms_norm(h, g_mlp[i])
        h = h + jnp.square(jax.nn.relu(m @ w_up[i])) @ w_down[i]

    return rms_norm(h, g_final)[:, N_META:]
```

```python
import functools
import math

import jax
import jax.numpy as jnp
import numpy as np
from jax import lax
from jax.experimental import pallas as pl
from jax.experimental.pallas import tpu as pltpu

N_META = 16
GRID_W = 64
HEAD_DIM = 128
N_Q_HEADS = 4
N_KV_HEADS = 2
Q_GROUP = N_Q_HEADS // N_KV_HEADS
ATTN_W = N_Q_HEADS * HEAD_DIM
KV_W = N_KV_HEADS * HEAD_DIM
N_FOURIER_GROUPS = 4
FOURIER_GROUP_W = 128
FOURIER_W = N_FOURIER_GROUPS * FOURIER_GROUP_W
ROPE_THETA = 10000.0
ROPE_AXIS_DIM = HEAD_DIM // 2
RMS_EPS = 1e-6

V7X_VMEM_BYTES = 64 * 1024 * 1024
V7X_BF16_SUBLANES = 16

TOKEN_TILE = 512
ATTN_Q_TILE = 512
ATTN_KV_CHUNK = 1024
DFT_N2_TILE = 16
DFT_COL_TILE = 2048

F32 = jnp.float32
BF16 = jnp.bfloat16


def _vmem_limit(estimate_bytes):
    return int(min(max(2 * estimate_bytes, 32 * 1024 * 1024), V7X_VMEM_BYTES - 8 * 1024 * 1024))


def _rms(x, g):
    return x * lax.rsqrt(jnp.mean(x * x, axis=-1, keepdims=True) + RMS_EPS) * g


def _dft_factors(l):
    best = None
    for n1 in range(V7X_BF16_SUBLANES, l, V7X_BF16_SUBLANES):
        if l % n1 == 0:
            n2 = l // n1
            if best is None or abs(n1 - n2) < abs(best[0] - best[1]):
                best = (n1, n2)
    assert best is not None
    return best


def _rope_tables(n_tok, n_rows):
    rows_count = n_tok // GRID_W
    real_row = jnp.repeat(jnp.arange(rows_count, dtype=F32), GRID_W)
    real_col = jnp.tile(jnp.arange(GRID_W, dtype=F32), rows_count)
    meta_row = jnp.full((N_META,), -1.0, dtype=F32)
    meta_col = jnp.arange(N_META, dtype=F32)
    pad = jnp.zeros((n_rows - n_tok - N_META,), F32)
    row = jnp.concatenate([real_row, meta_row, pad])
    col = jnp.concatenate([real_col, meta_col, pad])
    inv_freq = ROPE_THETA ** (-jnp.arange(0, ROPE_AXIS_DIM, 2, dtype=F32) / ROPE_AXIS_DIM)
    ang_r = row[:, None] * inv_freq[None, :]
    ang_c = col[:, None] * inv_freq[None, :]
    cos = jnp.concatenate([jnp.cos(ang_r)] * 2 + [jnp.cos(ang_c)] * 2, axis=-1)
    sin = jnp.concatenate([-jnp.sin(ang_r), jnp.sin(ang_r), -jnp.sin(ang_c), jnp.sin(ang_c)], axis=-1)
    return cos, sin


def _channel_dft_tables(l):
    n = np.arange(FOURIER_GROUP_W)
    ang = 2.0 * np.pi * ((n[:, None] * n[None, :]) % FOURIER_GROUP_W) / FOURIER_GROUP_W
    scale = 1.0 / math.sqrt(l * FOURIER_GROUP_W)
    return (np.cos(ang) * scale).astype(np.float32), (np.sin(ang) * scale).astype(np.float32)


def _dft_stage_tables(l, n1, n2, n2_pad):
    s = N_META
    k1 = np.arange(n1)[None, :, None]
    m1 = np.arange(n1)[None, None, :]
    j = np.arange(n2)[:, None, None]
    ph = ((k1 + s) * (n2 * m1 + j + s)) % l
    ang = 2.0 * np.pi * ph / l
    mr, mi = np.cos(ang), np.sin(ang)
    stage1 = np.zeros((n2_pad, 2 * n1, 2 * n1), np.float32)
    stage1[:n2] = np.concatenate(
        [np.concatenate([mr, -mi], axis=2), np.concatenate([mi, mr], axis=2)], axis=1
    )
    k2 = np.arange(n2)[:, None]
    jj = np.arange(n2)[None, :]
    ang2 = 2.0 * np.pi * ((k2 * (jj + s)) % n2) / n2
    stage2 = np.zeros((n2_pad, n2_pad, 2), np.float32)
    stage2[:n2, :n2, 0] = np.cos(ang2)
    stage2[:n2, :n2, 1] = -np.sin(ang2)
    return stage1, stage2.reshape(n2_pad, 2 * n2_pad)


def _fourier_weight_kernel(c_ref, s_ref, w_ref, ab_ref):
    w = w_ref[0]
    a = jnp.dot(c_ref[...], w, preferred_element_type=F32, precision=lax.Precision.HIGHEST)
    b = jnp.dot(s_ref[...], w, preferred_element_type=F32, precision=lax.Precision.HIGHEST)
    ab_ref[0, :, :FOURIER_GROUP_W] = a.astype(BF16)
    ab_ref[0, :, FOURIER_GROUP_W:] = b.astype(BF16)


def _fourier_weights(w_f, l):
    c, s = _channel_dft_tables(l)
    gw = FOURIER_GROUP_W
    return pl.pallas_call(
        _fourier_weight_kernel,
        grid=(N_FOURIER_GROUPS,),
        in_specs=[
            pl.BlockSpec((gw, gw), lambda g: (0, 0)),
            pl.BlockSpec((gw, gw), lambda g: (0, 0)),
            pl.BlockSpec((1, gw, gw), lambda g: (g, 0, 0)),
        ],
        out_specs=pl.BlockSpec((1, gw, 2 * gw), lambda g: (g, 0, 0)),
        out_shape=jax.ShapeDtypeStruct((N_FOURIER_GROUPS, gw, 2 * gw), BF16),
    )(jnp.asarray(c), jnp.asarray(s), w_f)


def _rope(x, cos, sin, lower_half):
    partner = jnp.where(lower_half, pltpu.roll(x, HEAD_DIM - 32, axis=1), pltpu.roll(x, 32, axis=1))
    return x * cos + partner * sin


def _in_proj_kernel(
    x_ref, meta_ref, cos_ref, sin_ref, gmix_ref, win_ref, gq_ref, gk_ref, ab_ref,
    q_ref, k_ref, v_ref, p_ref, qq_ref, *, n_real_tiles, q_scale,
):
    i = pl.program_id(1)
    xt = jnp.where(i == n_real_tiles, meta_ref[...], x_ref[0])
    hn = _rms(xt, gmix_ref[...]).astype(BF16)
    proj = jnp.dot(hn, win_ref[...], preferred_element_type=F32)
    cos = cos_ref[...]
    sin = sin_ref[...]
    lane = lax.broadcasted_iota(jnp.int32, cos.shape, 1)
    lower_half = (lane % ROPE_AXIS_DIM) < (ROPE_AXIS_DIM // 2)
    d = HEAD_DIM
    for h in range(N_Q_HEADS):
        qh = _rope(_rms(proj[:, h * d:(h + 1) * d], gq_ref[...]), cos, sin, lower_half)
        q_ref[0, :, h * d:(h + 1) * d] = (qh * q_scale).astype(BF16)
    for h in range(N_KV_HEADS):
        off = ATTN_W + h * d
        kh = _rope(_rms(proj[:, off:off + d], gk_ref[...]), cos, sin, lower_half)
        k_ref[0, :, h * d:(h + 1) * d] = kh.astype(BF16)
        off = ATTN_W + KV_W + h * d
        v_ref[0, :, 2 * h * d:(2 * h + 1) * d] = proj[:, off:off + d].astype(BF16)
        v_ref[0, :, (2 * h + 1) * d:(2 * h + 2) * d] = jnp.ones((proj.shape[0], d), BF16)
    gw = FOURIER_GROUP_W
    for g in range(N_FOURIER_GROUPS):
        off = ATTN_W + 2 * KV_W + g * gw
        pq = jnp.dot(proj[:, off:off + gw].astype(BF16), ab_ref[g], preferred_element_type=F32)
        p_ref[0, :, g * gw:(g + 1) * gw] = pq[:, :gw].astype(BF16)
        qq_ref[0, :, g * gw:(g + 1) * gw] = pq[:, gw:].astype(BF16)


def _in_proj(x, meta_pad, cos_tab, sin_tab, g_mix, w_in, g_q, g_k, ab, l):
    b, n_tok, d_model = x.shape
    t = TOKEN_TILE
    n_real_tiles = n_tok // t
    n_rows = (n_real_tiles + 1) * t
    in_w = w_in.shape[1]
    q_scale = math.log2(math.e) / math.sqrt(HEAD_DIM)
    const2 = lambda bi, i: (0, 0)
    est = (2 * t * d_model * 4 * 2 + d_model * in_w * 2 * 2 + t * in_w * 4 * 3
           + 2 * t * (2 * ATTN_W + KV_W + 2 * FOURIER_W) * 2)
    return pl.pallas_call(
        functools.partial(_in_proj_kernel, n_real_tiles=n_real_tiles, q_scale=q_scale),
        grid=(b, n_real_tiles + 1),
        in_specs=[
            pl.BlockSpec((1, t, d_model), lambda bi, i: (bi, jnp.minimum(i, n_real_tiles - 1), 0)),
            pl.BlockSpec((t, d_model), const2),
            pl.BlockSpec((t, HEAD_DIM), lambda bi, i: (i, 0)),
            pl.BlockSpec((t, HEAD_DIM), lambda bi, i: (i, 0)),
            pl.BlockSpec((1, d_model), const2),
            pl.BlockSpec((d_model, in_w), const2),
            pl.BlockSpec((1, HEAD_DIM), const2),
            pl.BlockSpec((1, HEAD_DIM), const2),
            pl.BlockSpec((N_FOURIER_GROUPS, FOURIER_GROUP_W, 2 * FOURIER_GROUP_W), lambda bi, i: (0, 0, 0)),
        ],
        out_specs=[
            pl.BlockSpec((1, t, ATTN_W), lambda bi, i: (bi, i, 0)),
            pl.BlockSpec((1, t, KV_W), lambda bi, i: (bi, i, 0)),
            pl.BlockSpec((1, t, 2 * KV_W), lambda bi, i: (bi, i, 0)),
            pl.BlockSpec((1, t, FOURIER_W), lambda bi, i: (bi, i, 0)),
            pl.BlockSpec((1, t, FOURIER_W), lambda bi, i: (bi, i, 0)),
        ],
        out_shape=[
            jax.ShapeDtypeStruct((b, n_rows, ATTN_W), BF16),
            jax.ShapeDtypeStruct((b, n_rows, KV_W), BF16),
            jax.ShapeDtypeStruct((b, n_rows, 2 * KV_W), BF16),
            jax.ShapeDtypeStruct((b, l, FOURIER_W), BF16),
            jax.ShapeDtypeStruct((b, l, FOURIER_W), BF16),
        ],
        compiler_params=pltpu.CompilerParams(
            dimension_semantics=("arbitrary", "arbitrary"), vmem_limit_bytes=_vmem_limit(est)),
    )(x, meta_pad, cos_tab, sin_tab, g_mix, w_in, g_q, g_k, ab)


_NT = (((1,), (1,)), ((), ()))


def _attention_kernel(q_ref, k_ref, v_ref, km_ref, vm_ref, o_ref, m_ref, acc_ref, *, n_chunks, chunk):
    d = HEAD_DIM
    tq = q_ref.shape[1]
    q2 = jnp.concatenate([q_ref[0, :, :d], q_ref[0, :, d:]], axis=0)

    s0 = lax.dot_general(q2, km_ref[0], _NT, preferred_element_type=F32)
    m0 = jnp.max(s0, axis=-1, keepdims=True)
    p0 = jnp.exp2(s0 - m0).astype(BF16)
    m_ref[...] = m0
    acc_ref[...] = jnp.dot(p0, vm_ref[0], preferred_element_type=F32)

    def body(j, carry):
        start = pl.multiple_of(j * chunk, chunk)
        kc = k_ref[0, pl.ds(start, chunk), :]
        vc = v_ref[0, pl.ds(start, chunk), :]
        s = lax.dot_general(q2, kc, _NT, preferred_element_type=F32)
        m_old = m_ref[...]
        m_new = jnp.maximum(m_old, jnp.max(s, axis=-1, keepdims=True))
        p = jnp.exp2(s - m_new).astype(BF16)
        acc_ref[...] = acc_ref[...] * jnp.exp2(m_old - m_new) + jnp.dot(p, vc, preferred_element_type=F32)
        m_ref[...] = m_new
        return carry

    lax.fori_loop(0, n_chunks, body, 0)
    acc = acc_ref[...]
    out = acc[:, :d] / acc[:, d:]
    o_ref[0, :, :d] = out[:tq]
    o_ref[0, :, d:] = out[tq:]


def _attention(q, k, v, n_tok):
    b = q.shape[0]
    tq = min(ATTN_Q_TILE, n_tok)
    chunk = min(ATTN_KV_CHUNK, n_tok)
    d = HEAD_DIM
    gq = Q_GROUP * d
    meta_blk = n_tok // N_META
    est = 2 * (tq * gq * 2 + n_tok * d * 2 + n_tok * 2 * d * 2 + tq * gq * 4) + Q_GROUP * tq * (
        chunk * 4 * 2 + chunk * 2 + 2 * d * 4 * 3)
    return pl.pallas_call(
        functools.partial(_attention_kernel, n_chunks=n_tok // chunk, chunk=chunk),
        grid=(b, N_KV_HEADS, n_tok // tq),
        in_specs=[
            pl.BlockSpec((1, tq, gq), lambda bi, g, i: (bi, i, g)),
            pl.BlockSpec((1, n_tok, d), lambda bi, g, i: (bi, 0, g)),
            pl.BlockSpec((1, n_tok, 2 * d), lambda bi, g, i: (bi, 0, g)),
            pl.BlockSpec((1, N_META, d), lambda bi, g, i: (bi, meta_blk, g)),
            pl.BlockSpec((1, N_META, 2 * d), lambda bi, g, i: (bi, meta_blk, g)),
        ],
        out_specs=pl.BlockSpec((1, tq, gq), lambda bi, g, i: (bi, i, g)),
        out_shape=jax.ShapeDtypeStruct((b, n_tok, ATTN_W), F32),
        scratch_shapes=[
            pltpu.VMEM((Q_GROUP * tq, 1), F32),
            pltpu.VMEM((Q_GROUP * tq, 2 * d), F32),
        ],
        compiler_params=pltpu.CompilerParams(
            dimension_semantics=("arbitrary", "arbitrary", "arbitrary"), vmem_limit_bytes=_vmem_limit(est)),
    )(q, k, v, k, v)


def _dft1_kernel(m_ref, p_ref, q_ref, g_ref, *, n2, cols):
    step = pl.program_id(0)
    n_sub = m_ref.shape[0]
    for j in range(n_sub):
        z = jnp.concatenate(
            [p_ref[0, :, j * cols:(j + 1) * cols], q_ref[0, :, j * cols:(j + 1) * cols]], axis=0)
        res = jnp.dot(m_ref[j].astype(BF16), z, preferred_element_type=F32)
        valid = step * n_sub + j < n2
        g_ref[0, j] = jnp.where(valid, res, 0.0).astype(BF16)


def _dft2_kernel(f_ref, g_ref, y_ref, *, n2):
    res = jnp.dot(f_ref[...].astype(BF16), g_ref[0], preferred_element_type=F32)
    y_ref[0] = res[:n2]


def _sequence_dft(p, q, l):
    b = p.shape[0]
    cols = p.shape[2]
    n1, n2 = _dft_factors(l)
    tn = DFT_N2_TILE
    n2_pad = -(-n2 // tn) * tn
    stage1, stage2 = _dft_stage_tables(l, n1, n2, n2_pad)
    pv = p.reshape(b, n1, n2 * cols)
    qv = q.reshape(b, n1, n2 * cols)
    est1 = 2 * (tn * 4 * n1 * n1 * 4 + 2 * n1 * tn * cols * 2 + tn * 2 * n1 * cols * 2) + 4 * n1 * cols * 4
    g = pl.pallas_call(
        functools.partial(_dft1_kernel, n2=n2, cols=cols),
        grid=(n2_pad // tn, b),
        in_specs=[
            pl.BlockSpec((tn, 2 * n1, 2 * n1), lambda i, bi: (i, 0, 0)),
            pl.BlockSpec((1, n1, tn * cols), lambda i, bi: (bi, 0, i)),
            pl.BlockSpec((1, n1, tn * cols), lambda i, bi: (bi, 0, i)),
        ],
        out_specs=pl.BlockSpec((1, tn, 2 * n1, cols), lambda i, bi: (bi, i, 0, 0)),
        out_shape=jax.ShapeDtypeStruct((b, n2_pad, 2 * n1, cols), BF16),
        compiler_params=pltpu.CompilerParams(
            dimension_semantics=("arbitrary", "arbitrary"), vmem_limit_bytes=_vmem_limit(est1)),
    )(jnp.asarray(stage1), pv, qv)

    gv = g.reshape(b, 2 * n2_pad, n1 * cols)
    tc = DFT_COL_TILE
    assert (n1 * cols) % tc == 0
    est2 = 2 * (2 * n2_pad * n2_pad * 4 + 2 * n2_pad * tc * 2 + n2 * tc * 4) + n2_pad * tc * 4
    y = pl.pallas_call(
        functools.partial(_dft2_kernel, n2=n2),
        grid=(b, n1 * cols // tc),
        in_specs=[
            pl.BlockSpec((n2_pad, 2 * n2_pad), lambda bi, i: (0, 0)),
            pl.BlockSpec((1, 2 * n2_pad, tc), lambda bi, i: (bi, 0, i)),
        ],
        out_specs=pl.BlockSpec((1, n2, tc), lambda bi, i: (bi, 0, i)),
        out_shape=jax.ShapeDtypeStruct((b, n2, n1 * cols), F32),
        compiler_params=pltpu.CompilerParams(
            dimension_semantics=("arbitrary", "arbitrary"), vmem_limit_bytes=_vmem_limit(est2)),
    )(jnp.asarray(stage2), gv)
    return y.reshape(b, l, cols)


def _out_mlp_kernel(
    x_ref, a_ref, f_ref, ga_ref, gf_ref, wout_ref, gmlp_ref, wup_ref, wdown_ref, gfin_ref, o_ref,
):
    an = _rms(a_ref[0], ga_ref[...]).astype(BF16)
    fn = _rms(f_ref[0], gf_ref[...]).astype(BF16)
    h = x_ref[0]
    h = h + jnp.dot(an, wout_ref[:ATTN_W, :], preferred_element_type=F32)
    h = h + jnp.dot(fn, wout_ref[ATTN_W:, :], preferred_element_type=F32)
    m = _rms(h, gmlp_ref[...]).astype(BF16)
    act = jnp.maximum(jnp.dot(m, wup_ref[...], preferred_element_type=F32), 0.0)
    out = h + jnp.dot((act * act).astype(BF16), wdown_ref[...], preferred_element_type=F32)
    o_ref[0] = _rms(out, gfin_ref[...])


def _out_mlp(x, attn, four, g_attn_out, g_fourier_out, w_out, g_mlp, w_up, w_down, g_final):
    b, n_tok, d_model = x.shape
    d_ff = w_up.shape[1]
    t = TOKEN_TILE
    const2 = lambda bi, i: (0, 0)
    tile3 = lambda bi, i: (bi, i, 0)
    est = (2 * (2 * t * d_model * 4 + t * ATTN_W * 4 + t * FOURIER_W * 4)
           + 2 * (d_model * d_model + 2 * d_model * d_ff) * 2 + t * (4 * d_model * 4 + d_ff * 6))
    return pl.pallas_call(
        _out_mlp_kernel,
        grid=(b, n_tok // t),
        in_specs=[
            pl.BlockSpec((1, t, d_model), tile3),
            pl.BlockSpec((1, t, ATTN_W), tile3),
            pl.BlockSpec((1, t, FOURIER_W), tile3),
            pl.BlockSpec((1, ATTN_W), const2),
            pl.BlockSpec((1, FOURIER_W), const2),
            pl.BlockSpec((d_model, d_model), const2),
            pl.BlockSpec((1, d_model), const2),
            pl.BlockSpec((d_model, d_ff), const2),
            pl.BlockSpec((d_ff, d_model), const2),
            pl.BlockSpec((1, d_model), const2),
        ],
        out_specs=pl.BlockSpec((1, t, d_model), tile3),
        out_shape=jax.ShapeDtypeStruct((b, n_tok, d_model), F32),
        compiler_params=pltpu.CompilerParams(
            dimension_semantics=("arbitrary", "arbitrary"), vmem_limit_bytes=_vmem_limit(est)),
    )(x, attn, four, g_attn_out, g_fourier_out, w_out, g_mlp, w_up, w_down, g_final)


def kernel(x, meta_tokens, g_mix, w_in, g_q, g_k, w_fourier, g_attn_out, g_fourier_out, w_out,
           g_mlp, w_up, w_down, g_final):
    assert g_mix.shape[0] == 1, "meta-token rows are only carried for a single layer"
    b, n_tok, d_model = x.shape
    l = n_tok + N_META
    t = TOKEN_TILE
    assert n_tok % t == 0 and n_tok % GRID_W == 0

    cos_tab, sin_tab = _rope_tables(n_tok, n_tok + t)
    meta_pad = jnp.pad(meta_tokens.astype(F32), ((0, t - N_META), (0, 0)))
    ab = _fourier_weights(w_fourier[0], l)
    q, k, v, p, qq = _in_proj(
        x, meta_pad, cos_tab, sin_tab, g_mix, w_in[0].astype(BF16), g_q, g_k, ab, l)
    attn = _attention(q, k, v, n_tok)
    four = _sequence_dft(p, qq, l)
    return _out_mlp(
        x, attn, four, g_attn_out, g_fourier_out, w_out[0].astype(BF16), g_mlp,
        w_up[0].astype(BF16), w_down[0].astype(BF16), g_final[None, :])
```

```python
import functools
import math

import jax
import jax.numpy as jnp
import numpy as np
from jax import lax
from jax.experimental import pallas as pl
from jax.experimental.pallas import tpu as pltpu

N_META = 16
GRID_W = 64
HEAD_DIM = 128
N_Q_HEADS = 4
N_KV_HEADS = 2
Q_GROUP = N_Q_HEADS // N_KV_HEADS
ATTN_W = N_Q_HEADS * HEAD_DIM
KV_W = N_KV_HEADS * HEAD_DIM
N_FOURIER_GROUPS = 4
FOURIER_GROUP_W = 128
FOURIER_W = N_FOURIER_GROUPS * FOURIER_GROUP_W
ROPE_THETA = 10000.0
ROPE_AXIS_DIM = HEAD_DIM // 2
RMS_EPS = 1e-6

V7X_VMEM_BYTES = 64 * 1024 * 1024
V7X_BF16_SUBLANES = 16

TOKEN_TILE = 512
ATTN_Q_TILE = 512
ATTN_KV_CHUNK = 512
DFT_N2_TILE = 16
DFT_COL_TILE = 2048

F32 = jnp.float32
BF16 = jnp.bfloat16


def _vmem_limit(estimate_bytes):
    return int(min(max(2 * estimate_bytes, 32 * 1024 * 1024), V7X_VMEM_BYTES - 8 * 1024 * 1024))


def _rms(x, g):
    return x * lax.rsqrt(jnp.mean(x * x, axis=-1, keepdims=True) + RMS_EPS) * g


def _dft_factors(l):
    best = None
    for n1 in range(V7X_BF16_SUBLANES, l, V7X_BF16_SUBLANES):
        if l % n1 == 0:
            n2 = l // n1
            if best is None or abs(n1 - n2) < abs(best[0] - best[1]):
                best = (n1, n2)
    assert best is not None
    return best


def _rope_tables(n_tok, n_rows):
    rows_count = n_tok // GRID_W
    real_row = jnp.repeat(jnp.arange(rows_count, dtype=F32), GRID_W)
    real_col = jnp.tile(jnp.arange(GRID_W, dtype=F32), rows_count)
    meta_row = jnp.full((N_META,), -1.0, dtype=F32)
    meta_col = jnp.arange(N_META, dtype=F32)
    pad = jnp.zeros((n_rows - n_tok - N_META,), F32)
    row = jnp.concatenate([real_row, meta_row, pad])
    col = jnp.concatenate([real_col, meta_col, pad])
    inv_freq = ROPE_THETA ** (-jnp.arange(0, ROPE_AXIS_DIM, 2, dtype=F32) / ROPE_AXIS_DIM)
    ang_r = row[:, None] * inv_freq[None, :]
    ang_c = col[:, None] * inv_freq[None, :]
    cos = jnp.concatenate([jnp.cos(ang_r)] * 2 + [jnp.cos(ang_c)] * 2, axis=-1)
    sin = jnp.concatenate([-jnp.sin(ang_r), jnp.sin(ang_r), -jnp.sin(ang_c), jnp.sin(ang_c)], axis=-1)
    return cos, sin


def _channel_dft_tables(l):
    n = np.arange(FOURIER_GROUP_W)
    ang = 2.0 * np.pi * ((n[:, None] * n[None, :]) % FOURIER_GROUP_W) / FOURIER_GROUP_W
    scale = 1.0 / math.sqrt(l * FOURIER_GROUP_W)
    return (np.cos(ang) * scale).astype(np.float32), (np.sin(ang) * scale).astype(np.float32)


def _dft_stage_tables(l, n1, n2, n2_pad):
    s = N_META
    k1 = np.arange(n1)[None, :, None]
    m1 = np.arange(n1)[None, None, :]
    j = np.arange(n2)[:, None, None]
    ph = ((k1 + s) * (n2 * m1 + j + s)) % l
    ang = 2.0 * np.pi * ph / l
    mr, mi = np.cos(ang), np.sin(ang)
    stage1 = np.zeros((n2_pad, 2 * n1, 2 * n1), np.float32)
    stage1[:n2] = np.concatenate(
        [np.concatenate([mr, -mi], axis=2), np.concatenate([mi, mr], axis=2)], axis=1
    )
    k2 = np.arange(n2)[:, None]
    jj = np.arange(n2)[None, :]
    ang2 = 2.0 * np.pi * ((k2 * (jj + s)) % n2) / n2
    stage2 = np.zeros((n2_pad, n2_pad, 2), np.float32)
    stage2[:n2, :n2, 0] = np.cos(ang2)
    stage2[:n2, :n2, 1] = -np.sin(ang2)
    return stage1, stage2.reshape(n2_pad, 2 * n2_pad)


def _fourier_weight_kernel(c_ref, s_ref, w_ref, ab_ref):
    w = w_ref[0]
    a = jnp.dot(c_ref[...], w, preferred_element_type=F32, precision=lax.Precision.HIGHEST)
    b = jnp.dot(s_ref[...], w, preferred_element_type=F32, precision=lax.Precision.HIGHEST)
    ab_ref[0, :, :FOURIER_GROUP_W] = a.astype(BF16)
    ab_ref[0, :, FOURIER_GROUP_W:] = b.astype(BF16)


def _fourier_weights(w_f, l):
    c, s = _channel_dft_tables(l)
    gw = FOURIER_GROUP_W
    return pl.pallas_call(
        _fourier_weight_kernel,
        grid=(N_FOURIER_GROUPS,),
        in_specs=[
            pl.BlockSpec((gw, gw), lambda g: (0, 0)),
            pl.BlockSpec((gw, gw), lambda g: (0, 0)),
            pl.BlockSpec((1, gw, gw), lambda g: (g, 0, 0)),
        ],
        out_specs=pl.BlockSpec((1, gw, 2 * gw), lambda g: (g, 0, 0)),
        out_shape=jax.ShapeDtypeStruct((N_FOURIER_GROUPS, gw, 2 * gw), BF16),
    )(jnp.asarray(c), jnp.asarray(s), w_f)


def _rope(x, cos, sin, lower_half):
    partner = jnp.where(lower_half, pltpu.roll(x, HEAD_DIM - 32, axis=1), pltpu.roll(x, 32, axis=1))
    return x * cos + partner * sin


def _in_proj_kernel(
    x_ref, meta_ref, cos_ref, sin_ref, gmix_ref, win_ref, gq_ref, gk_ref, ab_ref,
    q_ref, k_ref, v_ref, p_ref, qq_ref, *, n_real_tiles, q_scale,
):
    i = pl.program_id(1)
    xt = jnp.where(i == n_real_tiles, meta_ref[...], x_ref[0])
    hn = _rms(xt, gmix_ref[...]).astype(BF16)
    proj = jnp.dot(hn, win_ref[...], preferred_element_type=F32)
    cos = cos_ref[...]
    sin = sin_ref[...]
    lane = lax.broadcasted_iota(jnp.int32, cos.shape, 1)
    lower_half = (lane % ROPE_AXIS_DIM) < (ROPE_AXIS_DIM // 2)
    d = HEAD_DIM
    for h in range(N_Q_HEADS):
        qh = _rope(_rms(proj[:, h * d:(h + 1) * d], gq_ref[...]), cos, sin, lower_half)
        q_ref[0, :, h * d:(h + 1) * d] = (qh * q_scale).astype(BF16)
    for h in range(N_KV_HEADS):
        off = ATTN_W + h * d
        kh = _rope(_rms(proj[:, off:off + d], gk_ref[...]), cos, sin, lower_half)
        k_ref[0, :, h * d:(h + 1) * d] = kh.astype(BF16)
        off = ATTN_W + KV_W + h * d
        v_ref[0, :, 2 * h * d:(2 * h + 1) * d] = proj[:, off:off + d].astype(BF16)
        v_ref[0, :, (2 * h + 1) * d:(2 * h + 2) * d] = jnp.ones((proj.shape[0], d), BF16)
    gw = FOURIER_GROUP_W
    for g in range(N_FOURIER_GROUPS):
        off = ATTN_W + 2 * KV_W + g * gw
        pq = jnp.dot(proj[:, off:off + gw].astype(BF16), ab_ref[g], preferred_element_type=F32)
        p_ref[0, :, g * gw:(g + 1) * gw] = pq[:, :gw].astype(BF16)
        qq_ref[0, :, g * gw:(g + 1) * gw] = pq[:, gw:].astype(BF16)


def _in_proj(x, meta_pad, cos_tab, sin_tab, g_mix, w_in, g_q, g_k, ab, l):
    b, n_tok, d_model = x.shape
    t = TOKEN_TILE
    n_real_tiles = n_tok // t
    n_rows = (n_real_tiles + 1) * t
    in_w = w_in.shape[1]
    q_scale = math.log2(math.e) / math.sqrt(HEAD_DIM)
    const2 = lambda bi, i: (0, 0)
    est = (2 * t * d_model * 4 * 2 + d_model * in_w * 2 * 2 + t * in_w * 4 * 3
           + 2 * t * (2 * ATTN_W + KV_W + 2 * FOURIER_W) * 2)
    return pl.pallas_call(
        functools.partial(_in_proj_kernel, n_real_tiles=n_real_tiles, q_scale=q_scale),
        grid=(b, n_real_tiles + 1),
        in_specs=[
            pl.BlockSpec((1, t, d_model), lambda bi, i: (bi, jnp.minimum(i, n_real_tiles - 1), 0)),
            pl.BlockSpec((t, d_model), const2),
            pl.BlockSpec((t, HEAD_DIM), lambda bi, i: (i, 0)),
            pl.BlockSpec((t, HEAD_DIM), lambda bi, i: (i, 0)),
            pl.BlockSpec((1, d_model), const2),
            pl.BlockSpec((d_model, in_w), const2),
            pl.BlockSpec((1, HEAD_DIM), const2),
            pl.BlockSpec((1, HEAD_DIM), const2),
            pl.BlockSpec((N_FOURIER_GROUPS, FOURIER_GROUP_W, 2 * FOURIER_GROUP_W), lambda bi, i: (0, 0, 0)),
        ],
        out_specs=[
            pl.BlockSpec((1, t, ATTN_W), lambda bi, i: (bi, i, 0)),
            pl.BlockSpec((1, t, KV_W), lambda bi, i: (bi, i, 0)),
            pl.BlockSpec((1, t, 2 * KV_W), lambda bi, i: (bi, i, 0)),
            pl.BlockSpec((1, t, FOURIER_W), lambda bi, i: (bi, i, 0)),
            pl.BlockSpec((1, t, FOURIER_W), lambda bi, i: (bi, i, 0)),
        ],
        out_shape=[
            jax.ShapeDtypeStruct((b, n_rows, ATTN_W), BF16),
            jax.ShapeDtypeStruct((b, n_rows, KV_W), BF16),
            jax.ShapeDtypeStruct((b, n_rows, 2 * KV_W), BF16),
            jax.ShapeDtypeStruct((b, l, FOURIER_W), BF16),
            jax.ShapeDtypeStruct((b, l, FOURIER_W), BF16),
        ],
        compiler_params=pltpu.CompilerParams(
            dimension_semantics=("arbitrary", "arbitrary"), vmem_limit_bytes=_vmem_limit(est)),
    )(x, meta_pad, cos_tab, sin_tab, g_mix, w_in, g_q, g_k, ab)


_NT = (((1,), (1,)), ((), ()))


def _attention_kernel(q_ref, k_ref, v_ref, km_ref, vm_ref, o_ref, q2_ref, s0_ref, s1_ref, m_ref, acc_ref,
                      *, n_chunks, chunk):
    d = HEAD_DIM
    tq = q_ref.shape[1]
    q2_ref[:tq] = q_ref[0, :, :d]
    q2_ref[tq:] = q_ref[0, :, d:]

    def scores(c, s_ref):
        start = pl.multiple_of(c * chunk, chunk)
        s_ref[...] = lax.dot_general(
            q2_ref[...], k_ref[0, pl.ds(start, chunk), :], _NT, preferred_element_type=F32)

    def update(c, s_ref):
        start = pl.multiple_of(c * chunk, chunk)
        s = s_ref[...]
        m_old = m_ref[...]
        m_new = jnp.maximum(m_old, jnp.max(s, axis=-1, keepdims=True))
        p = jnp.exp2(s - m_new).astype(BF16)
        pv = jnp.dot(p, v_ref[0, pl.ds(start, chunk), :], preferred_element_type=F32)
        acc_ref[...] = acc_ref[...] * jnp.exp2(m_old - m_new) + pv
        m_ref[...] = m_new

    s_meta = lax.dot_general(q2_ref[...], km_ref[0], _NT, preferred_element_type=F32)
    m_meta = jnp.max(s_meta, axis=-1, keepdims=True)
    m_ref[...] = m_meta
    acc_ref[...] = jnp.dot(jnp.exp2(s_meta - m_meta).astype(BF16), vm_ref[0], preferred_element_type=F32)

    scores(0, s0_ref)

    def body(j, carry):
        c = 2 * j
        scores(c + 1, s1_ref)
        update(c, s0_ref)
        scores(c + 2, s0_ref)
        update(c + 1, s1_ref)
        return carry

    lax.fori_loop(0, n_chunks // 2 - 1, body, 0)
    scores(n_chunks - 1, s1_ref)
    update(n_chunks - 2, s0_ref)
    update(n_chunks - 1, s1_ref)

    acc = acc_ref[...]
    out = acc[:, :d] / acc[:, d:]
    o_ref[0, :, :d] = out[:tq]
    o_ref[0, :, d:] = out[tq:]


def _attention(q, k, v, n_tok):
    b = q.shape[0]
    tq = min(ATTN_Q_TILE, n_tok)
    chunk = min(ATTN_KV_CHUNK, n_tok)
    d = HEAD_DIM
    gq = Q_GROUP * d
    meta_blk = n_tok // N_META
    assert (n_tok // chunk) % 2 == 0
    est = 2 * (tq * gq * 2 + n_tok * d * 2 + n_tok * 2 * d * 2 + tq * gq * 4) + Q_GROUP * tq * (
        chunk * 4 * 3 + chunk * 2 + 2 * d * 4 * 3)
    return pl.pallas_call(
        functools.partial(_attention_kernel, n_chunks=n_tok // chunk, chunk=chunk),
        grid=(b, N_KV_HEADS, n_tok // tq),
        in_specs=[
            pl.BlockSpec((1, tq, gq), lambda bi, g, i: (bi, i, g)),
            pl.BlockSpec((1, n_tok, d), lambda bi, g, i: (bi, 0, g)),
            pl.BlockSpec((1, n_tok, 2 * d), lambda bi, g, i: (bi, 0, g)),
            pl.BlockSpec((1, N_META, d), lambda bi, g, i: (bi, meta_blk, g)),
            pl.BlockSpec((1, N_META, 2 * d), lambda bi, g, i: (bi, meta_blk, g)),
        ],
        out_specs=pl.BlockSpec((1, tq, gq), lambda bi, g, i: (bi, i, g)),
        out_shape=jax.ShapeDtypeStruct((b, n_tok, ATTN_W), F32),
        scratch_shapes=[
            pltpu.VMEM((Q_GROUP * tq, d), BF16),
            pltpu.VMEM((Q_GROUP * tq, chunk), F32),
            pltpu.VMEM((Q_GROUP * tq, chunk), F32),
            pltpu.VMEM((Q_GROUP * tq, 1), F32),
            pltpu.VMEM((Q_GROUP * tq, 2 * d), F32),
        ],
        compiler_params=pltpu.CompilerParams(
            dimension_semantics=("arbitrary", "arbitrary", "arbitrary"), vmem_limit_bytes=_vmem_limit(est)),
    )(q, k, v, k, v)


def _dft1_kernel(m_ref, p_ref, q_ref, g_ref, *, n2, cols):
    step = pl.program_id(0)
    n_sub = m_ref.shape[0]
    for j in range(n_sub):
        z = jnp.concatenate(
            [p_ref[0, :, j * cols:(j + 1) * cols], q_ref[0, :, j * cols:(j + 1) * cols]], axis=0)
        res = jnp.dot(m_ref[j].astype(BF16), z, preferred_element_type=F32)
        valid = step * n_sub + j < n2
        g_ref[0, j] = jnp.where(valid, res, 0.0).astype(BF16)


def _dft2_kernel(f_ref, g_ref, y_ref, *, n2):
    res = jnp.dot(f_ref[...].astype(BF16), g_ref[0], preferred_element_type=F32)
    y_ref[0] = res[:n2]


def _sequence_dft(p, q, l):
    b = p.shape[0]
    cols = p.shape[2]
    n1, n2 = _dft_factors(l)
    tn = DFT_N2_TILE
    n2_pad = -(-n2 // tn) * tn
    stage1, stage2 = _dft_stage_tables(l, n1, n2, n2_pad)
    pv = p.reshape(b, n1, n2 * cols)
    qv = q.reshape(b, n1, n2 * cols)
    est1 = 2 * (tn * 4 * n1 * n1 * 4 + 2 * n1 * tn * cols * 2 + tn * 2 * n1 * cols * 2) + 4 * n1 * cols * 4
    g = pl.pallas_call(
        functools.partial(_dft1_kernel, n2=n2, cols=cols),
        grid=(n2_pad // tn, b),
        in_specs=[
            pl.BlockSpec((tn, 2 * n1, 2 * n1), lambda i, bi: (i, 0, 0)),
            pl.BlockSpec((1, n1, tn * cols), lambda i, bi: (bi, 0, i)),
            pl.BlockSpec((1, n1, tn * cols), lambda i, bi: (bi, 0, i)),
        ],
        out_specs=pl.BlockSpec((1, tn, 2 * n1, cols), lambda i, bi: (bi, i, 0, 0)),
        out_shape=jax.ShapeDtypeStruct((b, n2_pad, 2 * n1, cols), BF16),
        compiler_params=pltpu.CompilerParams(
            dimension_semantics=("arbitrary", "arbitrary"), vmem_limit_bytes=_vmem_limit(est1)),
    )(jnp.asarray(stage1), pv, qv)

    gv = g.reshape(b, 2 * n2_pad, n1 * cols)
    tc = DFT_COL_TILE
    assert (n1 * cols) % tc == 0
    est2 = 2 * (2 * n2_pad * n2_pad * 4 + 2 * n2_pad * tc * 2 + n2 * tc * 4) + n2_pad * tc * 4
    y = pl.pallas_call(
        functools.partial(_dft2_kernel, n2=n2),
        grid=(b, n1 * cols // tc),
        in_specs=[
            pl.BlockSpec((n2_pad, 2 * n2_pad), lambda bi, i: (0, 0)),
            pl.BlockSpec((1, 2 * n2_pad, tc), lambda bi, i: (bi, 0, i)),
        ],
        out_specs=pl.BlockSpec((1, n2, tc), lambda bi, i: (bi, 0, i)),
        out_shape=jax.ShapeDtypeStruct((b, n2, n1 * cols), F32),
        compiler_params=pltpu.CompilerParams(
            dimension_semantics=("arbitrary", "arbitrary"), vmem_limit_bytes=_vmem_limit(est2)),
    )(jnp.asarray(stage2), gv)
    return y.reshape(b, l, cols)


def _out_mlp_kernel(
    x_ref, a_ref, f_ref, ga_ref, gf_ref, wout_ref, gmlp_ref, wup_ref, wdown_ref, gfin_ref, o_ref,
):
    an = _rms(a_ref[0], ga_ref[...]).astype(BF16)
    fn = _rms(f_ref[0], gf_ref[...]).astype(BF16)
    h = x_ref[0]
    h = h + jnp.dot(an, wout_ref[:ATTN_W, :], preferred_element_type=F32)
    h = h + jnp.dot(fn, wout_ref[ATTN_W:, :], preferred_element_type=F32)
    m = _rms(h, gmlp_ref[...]).astype(BF16)
    act = jnp.maximum(jnp.dot(m, wup_ref[...], preferred_element_type=F32), 0.0)
    out = h + jnp.dot((act * act).astype(BF16), wdown_ref[...], preferred_element_type=F32)
    o_ref[0] = _rms(out, gfin_ref[...])


def _out_mlp(x, attn, four, g_attn_out, g_fourier_out, w_out, g_mlp, w_up, w_down, g_final):
    b, n_tok, d_model = x.shape
    d_ff = w_up.shape[1]
    t = TOKEN_TILE
    const2 = lambda bi, i: (0, 0)
    tile3 = lambda bi, i: (bi, i, 0)
    est = (2 * (2 * t * d_model * 4 + t * ATTN_W * 4 + t * FOURIER_W * 4)
           + 2 * (d_model * d_model + 2 * d_model * d_ff) * 2 + t * (4 * d_model * 4 + d_ff * 6))
    return pl.pallas_call(
        _out_mlp_kernel,
        grid=(b, n_tok // t),
        in_specs=[
            pl.BlockSpec((1, t, d_model), tile3),
            pl.BlockSpec((1, t, ATTN_W), tile3),
            pl.BlockSpec((1, t, FOURIER_W), tile3),
            pl.BlockSpec((1, ATTN_W), const2),
            pl.BlockSpec((1, FOURIER_W), const2),
            pl.BlockSpec((d_model, d_model), const2),
            pl.BlockSpec((1, d_model), const2),
            pl.BlockSpec((d_model, d_ff), const2),
            pl.BlockSpec((d_ff, d_model), const2),
            pl.BlockSpec((1, d_model), const2),
        ],
        out_specs=pl.BlockSpec((1, t, d_model), tile3),
        out_shape=jax.ShapeDtypeStruct((b, n_tok, d_model), F32),
        compiler_params=pltpu.CompilerParams(
            dimension_semantics=("arbitrary", "arbitrary"), vmem_limit_bytes=_vmem_limit(est)),
    )(x, attn, four, g_attn_out, g_fourier_out, w_out, g_mlp, w_up, w_down, g_final)


def kernel(x, meta_tokens, g_mix, w_in, g_q, g_k, w_fourier, g_attn_out, g_fourier_out, w_out,
           g_mlp, w_up, w_down, g_final):
    assert g_mix.shape[0] == 1, "meta-token rows are only carried for a single layer"
    b, n_tok, d_model = x.shape
    l = n_tok + N_META
    t = TOKEN_TILE
    assert n_tok % t == 0 and n_tok % GRID_W == 0

    cos_tab, sin_tab = _rope_tables(n_tok, n_tok + t)
    meta_pad = jnp.pad(meta_tokens.astype(F32), ((0, t - N_META), (0, 0)))
    ab = _fourier_weights(w_fourier[0], l)
    q, k, v, p, qq = _in_proj(
        x, meta_pad, cos_tab, sin_tab, g_mix, w_in[0].astype(BF16), g_q, g_k, ab, l)
    attn = _attention(q, k, v, n_tok)
    four = _sequence_dft(p, qq, l)
    return _out_mlp(
        x, attn, four, g_attn_out, g_fourier_out, w_out[0].astype(BF16), g_mlp,
        w_up[0].astype(BF16), w_down[0].astype(BF16), g_final[None, :])
```

```python
import functools
import math

import jax
import jax.numpy as jnp
import numpy as np
from jax import lax
from jax.experimental import pallas as pl
from jax.experimental.pallas import tpu as pltpu

N_META = 16
GRID_W = 64
HEAD_DIM = 128
N_Q_HEADS = 4
N_KV_HEADS = 2
Q_GROUP = N_Q_HEADS // N_KV_HEADS
ATTN_W = N_Q_HEADS * HEAD_DIM
KV_W = N_KV_HEADS * HEAD_DIM
N_FOURIER_GROUPS = 4
FOURIER_GROUP_W = 128
FOURIER_W = N_FOURIER_GROUPS * FOURIER_GROUP_W
ROPE_THETA = 10000.0
ROPE_AXIS_DIM = HEAD_DIM // 2
RMS_EPS = 1e-6

V7X_VMEM_BYTES = 64 * 1024 * 1024
V7X_BF16_SUBLANES = 16

TOKEN_TILE = 512
ATTN_Q_TILE = 512
ATTN_KV_CHUNK = 512
DFT_N2_TILE = 16
DFT_COL_TILE = 2048

F32 = jnp.float32
BF16 = jnp.bfloat16


def _vmem_limit(estimate_bytes):
    return int(min(max(2 * estimate_bytes, 32 * 1024 * 1024), V7X_VMEM_BYTES - 8 * 1024 * 1024))


def _rms(x, g):
    return x * lax.rsqrt(jnp.mean(x * x, axis=-1, keepdims=True) + RMS_EPS) * g


def _dft_factors(l):
    best = None
    for n1 in range(V7X_BF16_SUBLANES, l, V7X_BF16_SUBLANES):
        if l % n1 == 0:
            n2 = l // n1
            if best is None or abs(n1 - n2) < abs(best[0] - best[1]):
                best = (n1, n2)
    assert best is not None
    return best


def _rope_tables(n_tok, n_rows):
    rows_count = n_tok // GRID_W
    real_row = jnp.repeat(jnp.arange(rows_count, dtype=F32), GRID_W)
    real_col = jnp.tile(jnp.arange(GRID_W, dtype=F32), rows_count)
    meta_row = jnp.full((N_META,), -1.0, dtype=F32)
    meta_col = jnp.arange(N_META, dtype=F32)
    pad = jnp.zeros((n_rows - n_tok - N_META,), F32)
    row = jnp.concatenate([real_row, meta_row, pad])
    col = jnp.concatenate([real_col, meta_col, pad])
    inv_freq = ROPE_THETA ** (-jnp.arange(0, ROPE_AXIS_DIM, 2, dtype=F32) / ROPE_AXIS_DIM)
    ang_r = row[:, None] * inv_freq[None, :]
    ang_c = col[:, None] * inv_freq[None, :]
    cos = jnp.concatenate([jnp.cos(ang_r)] * 2 + [jnp.cos(ang_c)] * 2, axis=-1)
    sin = jnp.concatenate([-jnp.sin(ang_r), jnp.sin(ang_r), -jnp.sin(ang_c), jnp.sin(ang_c)], axis=-1)
    return cos, sin


def _channel_dft_tables(l):
    n = np.arange(FOURIER_GROUP_W)
    ang = 2.0 * np.pi * ((n[:, None] * n[None, :]) % FOURIER_GROUP_W) / FOURIER_GROUP_W
    scale = 1.0 / math.sqrt(l * FOURIER_GROUP_W)
    return (np.cos(ang) * scale).astype(np.float32), (np.sin(ang) * scale).astype(np.float32)


def _dft_stage_tables(l, n1, n2, n2_pad):
    s = N_META
    k1 = np.arange(n1)[:, None]
    ang1 = 2.0 * np.pi * (((k1 + s) * n2 * np.arange(n1)[None, :]) % l) / l
    fr, fi = np.cos(ang1), np.sin(ang1)
    stage1 = np.concatenate(
        [np.concatenate([fr, -fi], axis=1), np.concatenate([fi, fr], axis=1)], axis=0).astype(np.float32)
    ang_t = 2.0 * np.pi * (((k1 + s) * (np.arange(n2)[None, :] + s)) % l) / l
    twiddle = np.zeros((2, n1, n2_pad), np.float32)
    twiddle[0, :, :n2] = np.cos(ang_t)
    twiddle[1, :, :n2] = np.sin(ang_t)
    k2 = np.arange(n2)[:, None]
    jj = np.arange(n2)[None, :]
    ang2 = 2.0 * np.pi * ((k2 * (jj + s)) % n2) / n2
    stage2 = np.zeros((n2_pad, n2_pad, 2), np.float32)
    stage2[:n2, :n2, 0] = np.cos(ang2)
    stage2[:n2, :n2, 1] = -np.sin(ang2)
    return stage1, twiddle, stage2.reshape(n2_pad, 2 * n2_pad)


def _fourier_weight_kernel(c_ref, s_ref, w_ref, ab_ref):
    w = w_ref[0]
    a = jnp.dot(c_ref[...], w, preferred_element_type=F32, precision=lax.Precision.HIGHEST)
    b = jnp.dot(s_ref[...], w, preferred_element_type=F32, precision=lax.Precision.HIGHEST)
    ab_ref[0, :, :FOURIER_GROUP_W] = a.astype(BF16)
    ab_ref[0, :, FOURIER_GROUP_W:] = b.astype(BF16)


def _fourier_weights(w_f, l):
    c, s = _channel_dft_tables(l)
    gw = FOURIER_GROUP_W
    return pl.pallas_call(
        _fourier_weight_kernel,
        grid=(N_FOURIER_GROUPS,),
        in_specs=[
            pl.BlockSpec((gw, gw), lambda g: (0, 0)),
            pl.BlockSpec((gw, gw), lambda g: (0, 0)),
            pl.BlockSpec((1, gw, gw), lambda g: (g, 0, 0)),
        ],
        out_specs=pl.BlockSpec((1, gw, 2 * gw), lambda g: (g, 0, 0)),
        out_shape=jax.ShapeDtypeStruct((N_FOURIER_GROUPS, gw, 2 * gw), BF16),
    )(jnp.asarray(c), jnp.asarray(s), w_f)


def _rope(x, cos, sin, lower_half):
    partner = jnp.where(lower_half, pltpu.roll(x, HEAD_DIM - 32, axis=1), pltpu.roll(x, 32, axis=1))
    return x * cos + partner * sin


def _in_proj_kernel(
    x_ref, meta_ref, cos_ref, sin_ref, gmix_ref, win_ref, gq_ref, gk_ref, ab_ref,
    q_ref, k_ref, v_ref, p_ref, qq_ref, *, n_real_tiles, q_scale,
):
    i = pl.program_id(1)
    xt = jnp.where(i == n_real_tiles, meta_ref[...], x_ref[0])
    hn = _rms(xt, gmix_ref[...]).astype(BF16)
    proj = jnp.dot(hn, win_ref[...], preferred_element_type=F32)
    cos = cos_ref[...]
    sin = sin_ref[...]
    lane = lax.broadcasted_iota(jnp.int32, cos.shape, 1)
    lower_half = (lane % ROPE_AXIS_DIM) < (ROPE_AXIS_DIM // 2)
    d = HEAD_DIM
    for h in range(N_Q_HEADS):
        qh = _rope(_rms(proj[:, h * d:(h + 1) * d], gq_ref[...]), cos, sin, lower_half)
        q_ref[0, :, h * d:(h + 1) * d] = (qh * q_scale).astype(BF16)
    for h in range(N_KV_HEADS):
        off = ATTN_W + h * d
        kh = _rope(_rms(proj[:, off:off + d], gk_ref[...]), cos, sin, lower_half)
        k_ref[0, :, h * d:(h + 1) * d] = kh.astype(BF16)
        off = ATTN_W + KV_W + h * d
        v_ref[0, :, 2 * h * d:(2 * h + 1) * d] = proj[:, off:off + d].astype(BF16)
        v_ref[0, :, (2 * h + 1) * d:(2 * h + 2) * d] = jnp.ones((proj.shape[0], d), BF16)
    gw = FOURIER_GROUP_W
    for g in range(N_FOURIER_GROUPS):
        off = ATTN_W + 2 * KV_W + g * gw
        pq = jnp.dot(proj[:, off:off + gw].astype(BF16), ab_ref[g], preferred_element_type=F32)
        p_ref[0, :, g * gw:(g + 1) * gw] = pq[:, :gw].astype(BF16)
        qq_ref[0, :, g * gw:(g + 1) * gw] = pq[:, gw:].astype(BF16)


def _in_proj(x, meta_pad, cos_tab, sin_tab, g_mix, w_in, g_q, g_k, ab, l):
    b, n_tok, d_model = x.shape
    t = TOKEN_TILE
    n_real_tiles = n_tok // t
    n_rows = (n_real_tiles + 1) * t
    in_w = w_in.shape[1]
    q_scale = math.log2(math.e) / math.sqrt(HEAD_DIM)
    const2 = lambda bi, i: (0, 0)
    est = (2 * t * d_model * 4 * 2 + d_model * in_w * 2 * 2 + t * in_w * 4 * 3
           + 2 * t * (2 * ATTN_W + KV_W + 2 * FOURIER_W) * 2)
    return pl.pallas_call(
        functools.partial(_in_proj_kernel, n_real_tiles=n_real_tiles, q_scale=q_scale),
        grid=(b, n_real_tiles + 1),
        in_specs=[
            pl.BlockSpec((1, t, d_model), lambda bi, i: (bi, jnp.minimum(i, n_real_tiles - 1), 0)),
            pl.BlockSpec((t, d_model), const2),
            pl.BlockSpec((t, HEAD_DIM), lambda bi, i: (i, 0)),
            pl.BlockSpec((t, HEAD_DIM), lambda bi, i: (i, 0)),
            pl.BlockSpec((1, d_model), const2),
            pl.BlockSpec((d_model, in_w), const2),
            pl.BlockSpec((1, HEAD_DIM), const2),
            pl.BlockSpec((1, HEAD_DIM), const2),
            pl.BlockSpec((N_FOURIER_GROUPS, FOURIER_GROUP_W, 2 * FOURIER_GROUP_W), lambda bi, i: (0, 0, 0)),
        ],
        out_specs=[
            pl.BlockSpec((1, t, ATTN_W), lambda bi, i: (bi, i, 0)),
            pl.BlockSpec((1, t, KV_W), lambda bi, i: (bi, i, 0)),
            pl.BlockSpec((1, t, 2 * KV_W), lambda bi, i: (bi, i, 0)),
            pl.BlockSpec((1, t, FOURIER_W), lambda bi, i: (bi, i, 0)),
            pl.BlockSpec((1, t, FOURIER_W), lambda bi, i: (bi, i, 0)),
        ],
        out_shape=[
            jax.ShapeDtypeStruct((b, n_rows, ATTN_W), BF16),
            jax.ShapeDtypeStruct((b, n_rows, KV_W), BF16),
            jax.ShapeDtypeStruct((b, n_rows, 2 * KV_W), BF16),
            jax.ShapeDtypeStruct((b, l, FOURIER_W), BF16),
            jax.ShapeDtypeStruct((b, l, FOURIER_W), BF16),
        ],
        compiler_params=pltpu.CompilerParams(
            dimension_semantics=("arbitrary", "arbitrary"), vmem_limit_bytes=_vmem_limit(est)),
    )(x, meta_pad, cos_tab, sin_tab, g_mix, w_in, g_q, g_k, ab)


_NT = (((1,), (1,)), ((), ()))


def _attention_kernel(q_ref, k_ref, v_ref, km_ref, vm_ref, o_ref, q2_ref, s0_ref, s1_ref, m_ref, acc_ref,
                      *, n_chunks, chunk):
    d = HEAD_DIM
    tq = q_ref.shape[1]
    q2_ref[:tq] = q_ref[0, :, :d]
    q2_ref[tq:] = q_ref[0, :, d:]

    def scores(c, s_ref):
        start = pl.multiple_of(c * chunk, chunk)
        s_ref[...] = lax.dot_general(
            q2_ref[...], k_ref[0, pl.ds(start, chunk), :], _NT, preferred_element_type=F32)

    def update(c, s_ref):
        start = pl.multiple_of(c * chunk, chunk)
        s = s_ref[...]
        m_old = m_ref[...]
        m_new = jnp.maximum(m_old, jnp.max(s, axis=-1, keepdims=True))
        p = jnp.exp2(s - m_new).astype(BF16)
        pv = jnp.dot(p, v_ref[0, pl.ds(start, chunk), :], preferred_element_type=F32)
        acc_ref[...] = acc_ref[...] * jnp.exp2(m_old - m_new) + pv
        m_ref[...] = m_new

    s_meta = lax.dot_general(q2_ref[...], km_ref[0], _NT, preferred_element_type=F32)
    m_meta = jnp.max(s_meta, axis=-1, keepdims=True)
    m_ref[...] = m_meta
    acc_ref[...] = jnp.dot(jnp.exp2(s_meta - m_meta).astype(BF16), vm_ref[0], preferred_element_type=F32)

    scores(0, s0_ref)

    def body(j, carry):
        c = 2 * j
        scores(c + 1, s1_ref)
        update(c, s0_ref)
        scores(jnp.minimum(c + 2, n_chunks - 1), s0_ref)
        update(c + 1, s1_ref)
        return carry

    lax.fori_loop(0, n_chunks // 2, body, 0)

    acc = acc_ref[...]
    out = acc[:, :d] / acc[:, d:]
    o_ref[0, :, :d] = out[:tq]
    o_ref[0, :, d:] = out[tq:]


def _attention(q, k, v, n_tok):
    b = q.shape[0]
    tq = min(ATTN_Q_TILE, n_tok)
    chunk = min(ATTN_KV_CHUNK, n_tok)
    d = HEAD_DIM
    gq = Q_GROUP * d
    meta_blk = n_tok // N_META
    assert (n_tok // chunk) % 2 == 0
    est = 2 * (tq * gq * 2 + n_tok * d * 2 + n_tok * 2 * d * 2 + tq * gq * 4) + Q_GROUP * tq * (
        chunk * 4 * 3 + chunk * 2 + 2 * d * 4 * 3)
    return pl.pallas_call(
        functools.partial(_attention_kernel, n_chunks=n_tok // chunk, chunk=chunk),
        grid=(b, N_KV_HEADS, n_tok // tq),
        in_specs=[
            pl.BlockSpec((1, tq, gq), lambda bi, g, i: (bi, i, g)),
            pl.BlockSpec((1, n_tok, d), lambda bi, g, i: (bi, 0, g)),
            pl.BlockSpec((1, n_tok, 2 * d), lambda bi, g, i: (bi, 0, g)),
            pl.BlockSpec((1, N_META, d), lambda bi, g, i: (bi, meta_blk, g)),
            pl.BlockSpec((1, N_META, 2 * d), lambda bi, g, i: (bi, meta_blk, g)),
        ],
        out_specs=pl.BlockSpec((1, tq, gq), lambda bi, g, i: (bi, i, g)),
        out_shape=jax.ShapeDtypeStruct((b, n_tok, ATTN_W), F32),
        scratch_shapes=[
            pltpu.VMEM((Q_GROUP * tq, d), BF16),
            pltpu.VMEM((Q_GROUP * tq, chunk), F32),
            pltpu.VMEM((Q_GROUP * tq, chunk), F32),
            pltpu.VMEM((Q_GROUP * tq, 1), F32),
            pltpu.VMEM((Q_GROUP * tq, 2 * d), F32),
        ],
        compiler_params=pltpu.CompilerParams(
            dimension_semantics=("arbitrary", "arbitrary", "arbitrary"), vmem_limit_bytes=_vmem_limit(est)),
    )(q, k, v, k, v)


def _dft1_kernel(f_ref, t_ref, p_ref, q_ref, g_ref, *, n2, cols):
    step = pl.program_id(0)
    n1 = t_ref.shape[2]
    n_sub = t_ref.shape[3]
    f = f_ref[...].astype(BF16)
    for j in range(n_sub):
        z = jnp.concatenate(
            [p_ref[0, :, j * cols:(j + 1) * cols], q_ref[0, :, j * cols:(j + 1) * cols]], axis=0)
        res = jnp.dot(f, z, preferred_element_type=F32)
        re, im = res[:n1], res[n1:]
        tr = t_ref[0, 0, :, j:j + 1]
        ti = t_ref[0, 1, :, j:j + 1]
        valid = step * n_sub + j < n2
        g_ref[0, j, :n1] = jnp.where(valid, tr * re - ti * im, 0.0).astype(BF16)
        g_ref[0, j, n1:] = jnp.where(valid, tr * im + ti * re, 0.0).astype(BF16)


def _dft2_kernel(f_ref, g_ref, y_ref, *, n2):
    res = jnp.dot(f_ref[...].astype(BF16), g_ref[0], preferred_element_type=F32)
    y_ref[0] = res[:n2]


def _sequence_dft(p, q, l):
    b = p.shape[0]
    cols = p.shape[2]
    n1, n2 = _dft_factors(l)
    tn = DFT_N2_TILE
    n2_pad = -(-n2 // tn) * tn
    stage1, twiddle, stage2 = _dft_stage_tables(l, n1, n2, n2_pad)
    twiddle = twiddle.reshape(2, n1, n2_pad // tn, tn).transpose(2, 0, 1, 3)
    pv = p.reshape(b, n1, n2 * cols)
    qv = q.reshape(b, n1, n2 * cols)
    est1 = 2 * (4 * n1 * n1 * 4 + 2 * n1 * tn * cols * 2 + tn * 2 * n1 * cols * 2) + 8 * n1 * cols * 4
    g = pl.pallas_call(
        functools.partial(_dft1_kernel, n2=n2, cols=cols),
        grid=(n2_pad // tn, b),
        in_specs=[
            pl.BlockSpec((2 * n1, 2 * n1), lambda i, bi: (0, 0)),
            pl.BlockSpec((1, 2, n1, tn), lambda i, bi: (i, 0, 0, 0)),
            pl.BlockSpec((1, n1, tn * cols), lambda i, bi: (bi, 0, i)),
            pl.BlockSpec((1, n1, tn * cols), lambda i, bi: (bi, 0, i)),
        ],
        out_specs=pl.BlockSpec((1, tn, 2 * n1, cols), lambda i, bi: (bi, i, 0, 0)),
        out_shape=jax.ShapeDtypeStruct((b, n2_pad, 2 * n1, cols), BF16),
        compiler_params=pltpu.CompilerParams(
            dimension_semantics=("arbitrary", "arbitrary"), vmem_limit_bytes=_vmem_limit(est1)),
    )(jnp.asarray(stage1), jnp.asarray(twiddle), pv, qv)

    gv = g.reshape(b, 2 * n2_pad, n1 * cols)
    tc = DFT_COL_TILE
    assert (n1 * cols) % tc == 0
    est2 = 2 * (2 * n2_pad * n2_pad * 4 + 2 * n2_pad * tc * 2 + n2 * tc * 4) + n2_pad * tc * 4
    y = pl.pallas_call(
        functools.partial(_dft2_kernel, n2=n2),
        grid=(b, n1 * cols // tc),
        in_specs=[
            pl.BlockSpec((n2_pad, 2 * n2_pad), lambda bi, i: (0, 0)),
            pl.BlockSpec((1, 2 * n2_pad, tc), lambda bi, i: (bi, 0, i)),
        ],
        out_specs=pl.BlockSpec((1, n2, tc), lambda bi, i: (bi, 0, i)),
        out_shape=jax.ShapeDtypeStruct((b, n2, n1 * cols), F32),
        compiler_params=pltpu.CompilerParams(
            dimension_semantics=("arbitrary", "arbitrary"), vmem_limit_bytes=_vmem_limit(est2)),
    )(jnp.asarray(stage2), gv)
    return y.reshape(b, l, cols)


def _out_mlp_kernel(
    x_ref, a_ref, f_ref, ga_ref, gf_ref, wout_ref, gmlp_ref, wup_ref, wdown_ref, gfin_ref, o_ref,
):
    an = _rms(a_ref[0], ga_ref[...]).astype(BF16)
    fn = _rms(f_ref[0], gf_ref[...]).astype(BF16)
    h = x_ref[0]
    h = h + jnp.dot(an, wout_ref[:ATTN_W, :], preferred_element_type=F32)
    h = h + jnp.dot(fn, wout_ref[ATTN_W:, :], preferred_element_type=F32)
    m = _rms(h, gmlp_ref[...]).astype(BF16)
    act = jnp.maximum(jnp.dot(m, wup_ref[...], preferred_element_type=F32), 0.0)
    out = h + jnp.dot((act * act).astype(BF16), wdown_ref[...], preferred_element_type=F32)
    o_ref[0] = _rms(out, gfin_ref[...])


def _out_mlp(x, attn, four, g_attn_out, g_fourier_out, w_out, g_mlp, w_up, w_down, g_final):
    b, n_tok, d_model = x.shape
    d_ff = w_up.shape[1]
    t = TOKEN_TILE
    const2 = lambda bi, i: (0, 0)
    tile3 = lambda bi, i: (bi, i, 0)
    est = (2 * (2 * t * d_model * 4 + t * ATTN_W * 4 + t * FOURIER_W * 4)
           + 2 * (d_model * d_model + 2 * d_model * d_ff) * 2 + t * (4 * d_model * 4 + d_ff * 6))
    return pl.pallas_call(
        _out_mlp_kernel,
        grid=(b, n_tok // t),
        in_specs=[
            pl.BlockSpec((1, t, d_model), tile3),
            pl.BlockSpec((1, t, ATTN_W), tile3),
            pl.BlockSpec((1, t, FOURIER_W), tile3),
            pl.BlockSpec((1, ATTN_W), const2),
            pl.BlockSpec((1, FOURIER_W), const2),
            pl.BlockSpec((d_model, d_model), const2),
            pl.BlockSpec((1, d_model), const2),
            pl.BlockSpec((d_model, d_ff), const2),
            pl.BlockSpec((d_ff, d_model), const2),
            pl.BlockSpec((1, d_model), const2),
        ],
        out_specs=pl.BlockSpec((1, t, d_model), tile3),
        out_shape=jax.ShapeDtypeStruct((b, n_tok, d_model), F32),
        compiler_params=pltpu.CompilerParams(
            dimension_semantics=("arbitrary", "arbitrary"), vmem_limit_bytes=_vmem_limit(est)),
    )(x, attn, four, g_attn_out, g_fourier_out, w_out, g_mlp, w_up, w_down, g_final)


def kernel(x, meta_tokens, g_mix, w_in, g_q, g_k, w_fourier, g_attn_out, g_fourier_out, w_out,
           g_mlp, w_up, w_down, g_final):
    assert g_mix.shape[0] == 1, "meta-token rows are only carried for a single layer"
    b, n_tok, d_model = x.shape
    l = n_tok + N_META
    t = TOKEN_TILE
    assert n_tok % t == 0 and n_tok % GRID_W == 0

    cos_tab, sin_tab = _rope_tables(n_tok, n_tok + t)
    meta_pad = jnp.pad(meta_tokens.astype(F32), ((0, t - N_META), (0, 0)))
    ab = _fourier_weights(w_fourier[0], l)
    q, k, v, p, qq = _in_proj(
        x, meta_pad, cos_tab, sin_tab, g_mix, w_in[0].astype(BF16), g_q, g_k, ab, l)
    attn = _attention(q, k, v, n_tok)
    four = _sequence_dft(p, qq, l)
    return _out_mlp(
        x, attn, four, g_attn_out, g_fourier_out, w_out[0].astype(BF16), g_mlp,
        w_up[0].astype(BF16), w_down[0].astype(BF16), g_final[None, :])
```

```python
import functools
import math

import jax
import jax.numpy as jnp
import numpy as np
from jax import lax
from jax.experimental import pallas as pl
from jax.experimental.pallas import tpu as pltpu

N_META = 16
GRID_W = 64
HEAD_DIM = 128
N_Q_HEADS = 4
N_KV_HEADS = 2
Q_GROUP = N_Q_HEADS // N_KV_HEADS
ATTN_W = N_Q_HEADS * HEAD_DIM
KV_W = N_KV_HEADS * HEAD_DIM
N_FOURIER_GROUPS = 4
FOURIER_GROUP_W = 128
FOURIER_W = N_FOURIER_GROUPS * FOURIER_GROUP_W
ROPE_THETA = 10000.0
ROPE_AXIS_DIM = HEAD_DIM // 2
RMS_EPS = 1e-6

V7X_VMEM_BYTES = 64 * 1024 * 1024
V7X_BF16_SUBLANES = 16

TOKEN_TILE = 512
ATTN_Q_TILE = 512
ATTN_KV_CHUNK = 512
DFT_N2_TILE = 16
DFT_K1_TILE = 8

F32 = jnp.float32
BF16 = jnp.bfloat16


def _vmem_limit(estimate_bytes):
    return int(min(max(2 * estimate_bytes, 32 * 1024 * 1024), V7X_VMEM_BYTES - 8 * 1024 * 1024))


def _rms(x, g):
    return x * lax.rsqrt(jnp.mean(x * x, axis=-1, keepdims=True) + RMS_EPS) * g


def _dft_factors(l):
    best = None
    for n1 in range(V7X_BF16_SUBLANES, l, V7X_BF16_SUBLANES):
        if l % n1 == 0:
            n2 = l // n1
            if best is None or abs(n1 - n2) < abs(best[0] - best[1]):
                best = (n1, n2)
    assert best is not None
    return best


def _rope_tables(n_tok, n_rows):
    rows_count = n_tok // GRID_W
    real_row = jnp.repeat(jnp.arange(rows_count, dtype=F32), GRID_W)
    real_col = jnp.tile(jnp.arange(GRID_W, dtype=F32), rows_count)
    meta_row = jnp.full((N_META,), -1.0, dtype=F32)
    meta_col = jnp.arange(N_META, dtype=F32)
    pad = jnp.zeros((n_rows - n_tok - N_META,), F32)
    row = jnp.concatenate([real_row, meta_row, pad])
    col = jnp.concatenate([real_col, meta_col, pad])
    inv_freq = ROPE_THETA ** (-jnp.arange(0, ROPE_AXIS_DIM, 2, dtype=F32) / ROPE_AXIS_DIM)
    ang_r = row[:, None] * inv_freq[None, :]
    ang_c = col[:, None] * inv_freq[None, :]
    cos = jnp.concatenate([jnp.cos(ang_r)] * 2 + [jnp.cos(ang_c)] * 2, axis=-1)
    sin = jnp.concatenate([-jnp.sin(ang_r), jnp.sin(ang_r), -jnp.sin(ang_c), jnp.sin(ang_c)], axis=-1)
    return cos, sin


def _channel_dft_tables(l):
    n = np.arange(FOURIER_GROUP_W)
    ang = 2.0 * np.pi * ((n[:, None] * n[None, :]) % FOURIER_GROUP_W) / FOURIER_GROUP_W
    scale = 1.0 / math.sqrt(l * FOURIER_GROUP_W)
    return (np.cos(ang) * scale).astype(np.float32), (np.sin(ang) * scale).astype(np.float32)


def _dft_stage_tables(l, n1, n2, n2_pad):
    s = N_META
    k1 = np.arange(n1)[:, None]
    ang1 = 2.0 * np.pi * (((k1 + s) * n2 * np.arange(n1)[None, :]) % l) / l
    fr, fi = np.cos(ang1), np.sin(ang1)
    stage1 = np.concatenate(
        [np.concatenate([fr, -fi], axis=1), np.concatenate([fi, fr], axis=1)], axis=0).astype(np.float32)
    ang_t = 2.0 * np.pi * (((k1 + s) * (np.arange(n2)[None, :] + s)) % l) / l
    twiddle = np.zeros((2, n1, n2_pad), np.float32)
    twiddle[0, :, :n2] = np.cos(ang_t)
    twiddle[1, :, :n2] = np.sin(ang_t)
    k2 = np.arange(n2)[:, None]
    jj = np.arange(n2)[None, :]
    ang2 = 2.0 * np.pi * ((k2 * (jj + s)) % n2) / n2
    stage2 = np.zeros((n2_pad, n2_pad, 2), np.float32)
    stage2[:n2, :n2, 0] = np.cos(ang2)
    stage2[:n2, :n2, 1] = -np.sin(ang2)
    return stage1, twiddle, stage2.reshape(n2_pad, 2 * n2_pad)


def _fourier_weight_kernel(c_ref, s_ref, w_ref, ab_ref):
    w = w_ref[0]
    a = jnp.dot(c_ref[...], w, preferred_element_type=F32, precision=lax.Precision.HIGHEST)
    b = jnp.dot(s_ref[...], w, preferred_element_type=F32, precision=lax.Precision.HIGHEST)
    ab_ref[0, :, :FOURIER_GROUP_W] = a.astype(BF16)
    ab_ref[0, :, FOURIER_GROUP_W:] = b.astype(BF16)


def _fourier_weights(w_f, l):
    c, s = _channel_dft_tables(l)
    gw = FOURIER_GROUP_W
    return pl.pallas_call(
        _fourier_weight_kernel,
        grid=(N_FOURIER_GROUPS,),
        in_specs=[
            pl.BlockSpec((gw, gw), lambda g: (0, 0)),
            pl.BlockSpec((gw, gw), lambda g: (0, 0)),
            pl.BlockSpec((1, gw, gw), lambda g: (g, 0, 0)),
        ],
        out_specs=pl.BlockSpec((1, gw, 2 * gw), lambda g: (g, 0, 0)),
        out_shape=jax.ShapeDtypeStruct((N_FOURIER_GROUPS, gw, 2 * gw), BF16),
    )(jnp.asarray(c), jnp.asarray(s), w_f)


def _rope(x, cos, sin, lower_half):
    partner = jnp.where(lower_half, pltpu.roll(x, HEAD_DIM - 32, axis=1), pltpu.roll(x, 32, axis=1))
    return x * cos + partner * sin


def _in_proj_kernel(
    x_ref, meta_ref, cos_ref, sin_ref, gmix_ref, win_ref, gq_ref, gk_ref, ab_ref,
    q_ref, k_ref, v_ref, p_ref, qq_ref, *, n_real_tiles, q_scale,
):
    i = pl.program_id(1)
    xt = jnp.where(i == n_real_tiles, meta_ref[...], x_ref[0])
    hn = _rms(xt, gmix_ref[...]).astype(BF16)
    proj = jnp.dot(hn, win_ref[...], preferred_element_type=F32)
    cos = cos_ref[...]
    sin = sin_ref[...]
    lane = lax.broadcasted_iota(jnp.int32, cos.shape, 1)
    lower_half = (lane % ROPE_AXIS_DIM) < (ROPE_AXIS_DIM // 2)
    d = HEAD_DIM
    for h in range(N_Q_HEADS):
        qh = _rope(_rms(proj[:, h * d:(h + 1) * d], gq_ref[...]), cos, sin, lower_half)
        q_ref[0, :, h * d:(h + 1) * d] = (qh * q_scale).astype(BF16)
    for h in range(N_KV_HEADS):
        off = ATTN_W + h * d
        kh = _rope(_rms(proj[:, off:off + d], gk_ref[...]), cos, sin, lower_half)
        k_ref[0, :, h * d:(h + 1) * d] = kh.astype(BF16)
        off = ATTN_W + KV_W + h * d
        v_ref[0, :, 2 * h * d:(2 * h + 1) * d] = proj[:, off:off + d].astype(BF16)
        v_ref[0, :, (2 * h + 1) * d:(2 * h + 2) * d] = jnp.ones((proj.shape[0], d), BF16)
    gw = FOURIER_GROUP_W
    for g in range(N_FOURIER_GROUPS):
        off = ATTN_W + 2 * KV_W + g * gw
        pq = jnp.dot(proj[:, off:off + gw].astype(BF16), ab_ref[g], preferred_element_type=F32)
        p_ref[0, :, g * gw:(g + 1) * gw] = pq[:, :gw].astype(BF16)
        qq_ref[0, :, g * gw:(g + 1) * gw] = pq[:, gw:].astype(BF16)


def _in_proj(x, meta_pad, cos_tab, sin_tab, g_mix, w_in, g_q, g_k, ab, l):
    b, n_tok, d_model = x.shape
    t = TOKEN_TILE
    n_real_tiles = n_tok // t
    n_rows = (n_real_tiles + 1) * t
    in_w = w_in.shape[1]
    q_scale = math.log2(math.e) / math.sqrt(HEAD_DIM)
    const2 = lambda bi, i: (0, 0)
    est = (2 * t * d_model * 4 * 2 + d_model * in_w * 2 * 2 + t * in_w * 4 * 3
           + 2 * t * (2 * ATTN_W + KV_W + 2 * FOURIER_W) * 2)
    return pl.pallas_call(
        functools.partial(_in_proj_kernel, n_real_tiles=n_real_tiles, q_scale=q_scale),
        grid=(b, n_real_tiles + 1),
        in_specs=[
            pl.BlockSpec((1, t, d_model), lambda bi, i: (bi, jnp.minimum(i, n_real_tiles - 1), 0)),
            pl.BlockSpec((t, d_model), const2),
            pl.BlockSpec((t, HEAD_DIM), lambda bi, i: (i, 0)),
            pl.BlockSpec((t, HEAD_DIM), lambda bi, i: (i, 0)),
            pl.BlockSpec((1, d_model), const2),
            pl.BlockSpec((d_model, in_w), const2),
            pl.BlockSpec((1, HEAD_DIM), const2),
            pl.BlockSpec((1, HEAD_DIM), const2),
            pl.BlockSpec((N_FOURIER_GROUPS, FOURIER_GROUP_W, 2 * FOURIER_GROUP_W), lambda bi, i: (0, 0, 0)),
        ],
        out_specs=[
            pl.BlockSpec((1, t, ATTN_W), lambda bi, i: (bi, i, 0)),
            pl.BlockSpec((1, t, KV_W), lambda bi, i: (bi, i, 0)),
            pl.BlockSpec((1, t, 2 * KV_W), lambda bi, i: (bi, i, 0)),
            pl.BlockSpec((1, t, FOURIER_W), lambda bi, i: (bi, i, 0)),
            pl.BlockSpec((1, t, FOURIER_W), lambda bi, i: (bi, i, 0)),
        ],
        out_shape=[
            jax.ShapeDtypeStruct((b, n_rows, ATTN_W), BF16),
            jax.ShapeDtypeStruct((b, n_rows, KV_W), BF16),
            jax.ShapeDtypeStruct((b, n_rows, 2 * KV_W), BF16),
            jax.ShapeDtypeStruct((b, l, FOURIER_W), BF16),
            jax.ShapeDtypeStruct((b, l, FOURIER_W), BF16),
        ],
        compiler_params=pltpu.CompilerParams(
            dimension_semantics=("arbitrary", "arbitrary"), vmem_limit_bytes=_vmem_limit(est)),
    )(x, meta_pad, cos_tab, sin_tab, g_mix, w_in, g_q, g_k, ab)


_NT = (((1,), (1,)), ((), ()))


def _attention_kernel(q_ref, k_ref, v_ref, km_ref, vm_ref, o_ref, q2_ref, s0_ref, s1_ref, m_ref, acc_ref,
                      *, n_chunks, chunk):
    d = HEAD_DIM
    tq = q_ref.shape[1]
    q2_ref[:tq] = q_ref[0, :, :d]
    q2_ref[tq:] = q_ref[0, :, d:]

    def scores(c, s_ref):
        start = pl.multiple_of(c * chunk, chunk)
        s_ref[...] = lax.dot_general(
            q2_ref[...], k_ref[0, pl.ds(start, chunk), :], _NT, preferred_element_type=F32)

    def update(c, s_ref):
        start = pl.multiple_of(c * chunk, chunk)
        s = s_ref[...]
        m_old = m_ref[...]
        m_new = jnp.maximum(m_old, jnp.max(s, axis=-1, keepdims=True))
        p = jnp.exp2(s - m_new).astype(BF16)
        pv = jnp.dot(p, v_ref[0, pl.ds(start, chunk), :], preferred_element_type=F32)
        acc_ref[...] = acc_ref[...] * jnp.exp2(m_old - m_new) + pv
        m_ref[...] = m_new

    s_meta = lax.dot_general(q2_ref[...], km_ref[0], _NT, preferred_element_type=F32)
    m_meta = jnp.max(s_meta, axis=-1, keepdims=True)
    m_ref[...] = m_meta
    acc_ref[...] = jnp.dot(jnp.exp2(s_meta - m_meta).astype(BF16), vm_ref[0], preferred_element_type=F32)

    scores(0, s0_ref)

    def body(j, carry):
        c = 2 * j
        scores(c + 1, s1_ref)
        update(c, s0_ref)
        scores(jnp.minimum(c + 2, n_chunks - 1), s0_ref)
        update(c + 1, s1_ref)
        return carry

    lax.fori_loop(0, n_chunks // 2, body, 0)

    acc = acc_ref[...]
    out = acc[:, :d] / acc[:, d:]
    o_ref[0, :, :d] = out[:tq]
    o_ref[0, :, d:] = out[tq:]


def _attention(q, k, v, n_tok):
    b = q.shape[0]
    tq = min(ATTN_Q_TILE, n_tok)
    chunk = min(ATTN_KV_CHUNK, n_tok)
    d = HEAD_DIM
    gq = Q_GROUP * d
    meta_blk = n_tok // N_META
    assert (n_tok // chunk) % 2 == 0
    est = 2 * (tq * gq * 2 + n_tok * d * 2 + n_tok * 2 * d * 2 + tq * gq * 4) + Q_GROUP * tq * (
        chunk * 4 * 3 + chunk * 2 + 2 * d * 4 * 3)
    return pl.pallas_call(
        functools.partial(_attention_kernel, n_chunks=n_tok // chunk, chunk=chunk),
        grid=(b, N_KV_HEADS, n_tok // tq),
        in_specs=[
            pl.BlockSpec((1, tq, gq), lambda bi, g, i: (bi, i, g)),
            pl.BlockSpec((1, n_tok, d), lambda bi, g, i: (bi, 0, g)),
            pl.BlockSpec((1, n_tok, 2 * d), lambda bi, g, i: (bi, 0, g)),
            pl.BlockSpec((1, N_META, d), lambda bi, g, i: (bi, meta_blk, g)),
            pl.BlockSpec((1, N_META, 2 * d), lambda bi, g, i: (bi, meta_blk, g)),
        ],
        out_specs=pl.BlockSpec((1, tq, gq), lambda bi, g, i: (bi, i, g)),
        out_shape=jax.ShapeDtypeStruct((b, n_tok, ATTN_W), F32),
        scratch_shapes=[
            pltpu.VMEM((Q_GROUP * tq, d), BF16),
            pltpu.VMEM((Q_GROUP * tq, chunk), F32),
            pltpu.VMEM((Q_GROUP * tq, chunk), F32),
            pltpu.VMEM((Q_GROUP * tq, 1), F32),
            pltpu.VMEM((Q_GROUP * tq, 2 * d), F32),
        ],
        compiler_params=pltpu.CompilerParams(
            dimension_semantics=("arbitrary", "arbitrary", "arbitrary"), vmem_limit_bytes=_vmem_limit(est)),
    )(q, k, v, k, v)


def _dft1_kernel(f_ref, t_ref, p_ref, q_ref, g_ref, *, n2, cols):
    step = pl.program_id(0)
    n1 = t_ref.shape[2]
    n_sub = t_ref.shape[3]
    f = f_ref[...].astype(BF16)
    for j in range(n_sub):
        z = jnp.concatenate(
            [p_ref[0, :, j * cols:(j + 1) * cols], q_ref[0, :, j * cols:(j + 1) * cols]], axis=0)
        res = jnp.dot(f, z, preferred_element_type=F32)
        re, im = res[:n1], res[n1:]
        tr = t_ref[0, 0, :, j:j + 1]
        ti = t_ref[0, 1, :, j:j + 1]
        valid = step * n_sub + j < n2
        g_ref[0, j, :n1] = jnp.where(valid, tr * re - ti * im, 0.0).astype(BF16)
        g_ref[0, j, n1:] = jnp.where(valid, tr * im + ti * re, 0.0).astype(BF16)


def _dft2_kernel(f_ref, g_ref, y_ref, *, n2):
    f = f_ref[...].astype(BF16)
    for kk in range(g_ref.shape[1]):
        res = jnp.dot(f, g_ref[0, kk], preferred_element_type=F32)
        y_ref[0, kk] = res[:n2]


def _sequence_dft(p, q, l):
    b = p.shape[0]
    cols = p.shape[2]
    n1, n2 = _dft_factors(l)
    tn = DFT_N2_TILE
    n2_pad = -(-n2 // tn) * tn
    stage1, twiddle, stage2 = _dft_stage_tables(l, n1, n2, n2_pad)
    twiddle = twiddle.reshape(2, n1, n2_pad // tn, tn).transpose(2, 0, 1, 3)
    pv = p.reshape(b, n1, n2 * cols)
    qv = q.reshape(b, n1, n2 * cols)
    est1 = 2 * (4 * n1 * n1 * 4 + 2 * n1 * tn * cols * 2 + tn * 2 * n1 * cols * 4) + 8 * n1 * cols * 4
    g = pl.pallas_call(
        functools.partial(_dft1_kernel, n2=n2, cols=cols),
        grid=(n2_pad // tn, b),
        in_specs=[
            pl.BlockSpec((2 * n1, 2 * n1), lambda i, bi: (0, 0)),
            pl.BlockSpec((1, 2, n1, tn), lambda i, bi: (i, 0, 0, 0)),
            pl.BlockSpec((1, n1, tn * cols), lambda i, bi: (bi, 0, i)),
            pl.BlockSpec((1, n1, tn * cols), lambda i, bi: (bi, 0, i)),
        ],
        out_specs=pl.BlockSpec((1, tn, 2 * n1, cols), lambda i, bi: (bi, i, 0, 0)),
        out_shape=jax.ShapeDtypeStruct((b, n2_pad, 2 * n1, cols), BF16),
        compiler_params=pltpu.CompilerParams(
            dimension_semantics=("arbitrary", "arbitrary"), vmem_limit_bytes=_vmem_limit(est1)),
    )(jnp.asarray(stage1), jnp.asarray(twiddle), pv, qv)

    gv = g.reshape(b, 2 * n2_pad, n1, cols).transpose(0, 2, 1, 3)
    tk = DFT_K1_TILE
    assert n1 % tk == 0
    est2 = 2 * (2 * n2_pad * n2_pad * 4 + tk * 2 * n2_pad * cols * 2 + tk * n2 * cols * 4) + 3 * n2_pad * cols * 4
    y = pl.pallas_call(
        functools.partial(_dft2_kernel, n2=n2),
        grid=(b, n1 // tk),
        in_specs=[
            pl.BlockSpec((n2_pad, 2 * n2_pad), lambda bi, i: (0, 0)),
            pl.BlockSpec((1, tk, 2 * n2_pad, cols), lambda bi, i: (bi, i, 0, 0)),
        ],
        out_specs=pl.BlockSpec((1, tk, n2, cols), lambda bi, i: (bi, i, 0, 0)),
        out_shape=jax.ShapeDtypeStruct((b, n1, n2, cols), F32),
        compiler_params=pltpu.CompilerParams(
            dimension_semantics=("arbitrary", "arbitrary"), vmem_limit_bytes=_vmem_limit(est2)),
    )(jnp.asarray(stage2), gv)
    return y.transpose(0, 2, 1, 3).reshape(b, l, cols)


def _out_mlp_kernel(
    x_ref, a_ref, f_ref, ga_ref, gf_ref, wout_ref, gmlp_ref, wup_ref, wdown_ref, gfin_ref, o_ref,
):
    an = _rms(a_ref[0], ga_ref[...]).astype(BF16)
    fn = _rms(f_ref[0], gf_ref[...]).astype(BF16)
    h = x_ref[0]
    h = h + jnp.dot(an, wout_ref[:ATTN_W, :], preferred_element_type=F32)
    h = h + jnp.dot(fn, wout_ref[ATTN_W:, :], preferred_element_type=F32)
    m = _rms(h, gmlp_ref[...]).astype(BF16)
    act = jnp.maximum(jnp.dot(m, wup_ref[...], preferred_element_type=F32), 0.0)
    out = h + jnp.dot((act * act).astype(BF16), wdown_ref[...], preferred_element_type=F32)
    o_ref[0] = _rms(out, gfin_ref[...])


def _out_mlp(x, attn, four, g_attn_out, g_fourier_out, w_out, g_mlp, w_up, w_down, g_final):
    b, n_tok, d_model = x.shape
    d_ff = w_up.shape[1]
    t = TOKEN_TILE
    const2 = lambda bi, i: (0, 0)
    tile3 = lambda bi, i: (bi, i, 0)
    est = (2 * (2 * t * d_model * 4 + t * ATTN_W * 4 + t * FOURIER_W * 4)
           + 2 * (d_model * d_model + 2 * d_model * d_ff) * 2 + t * (4 * d_model * 4 + d_ff * 6))
    return pl.pallas_call(
        _out_mlp_kernel,
        grid=(b, n_tok // t),
        in_specs=[
            pl.BlockSpec((1, t, d_model), tile3),
            pl.BlockSpec((1, t, ATTN_W), tile3),
            pl.BlockSpec((1, t, FOURIER_W), tile3),
            pl.BlockSpec((1, ATTN_W), const2),
            pl.BlockSpec((1, FOURIER_W), const2),
            pl.BlockSpec((d_model, d_model), const2),
            pl.BlockSpec((1, d_model), const2),
            pl.BlockSpec((d_model, d_ff), const2),
            pl.BlockSpec((d_ff, d_model), const2),
            pl.BlockSpec((1, d_model), const2),
        ],
        out_specs=pl.BlockSpec((1, t, d_model), tile3),
        out_shape=jax.ShapeDtypeStruct((b, n_tok, d_model), F32),
        compiler_params=pltpu.CompilerParams(
            dimension_semantics=("arbitrary", "arbitrary"), vmem_limit_bytes=_vmem_limit(est)),
    )(x, attn, four, g_attn_out, g_fourier_out, w_out, g_mlp, w_up, w_down, g_final)


def kernel(x, meta_tokens, g_mix, w_in, g_q, g_k, w_fourier, g_attn_out, g_fourier_out, w_out,
           g_mlp, w_up, w_down, g_final):
    assert g_mix.shape[0] == 1, "meta-token rows are only carried for a single layer"
    b, n_tok, d_model = x.shape
    l = n_tok + N_META
    t = TOKEN_TILE
    assert n_tok % t == 0 and n_tok % GRID_W == 0

    cos_tab, sin_tab = _rope_tables(n_tok, n_tok + t)
    meta_pad = jnp.pad(meta_tokens.astype(F32), ((0, t - N_META), (0, 0)))
    ab = _fourier_weights(w_fourier[0], l)
    q, k, v, p, qq = _in_proj(
        x, meta_pad, cos_tab, sin_tab, g_mix, w_in[0].astype(BF16), g_q, g_k, ab, l)
    attn = _attention(q, k, v, n_tok)
    four = _sequence_dft(p, qq, l)
    return _out_mlp(
        x, attn, four, g_attn_out, g_fourier_out, w_out[0].astype(BF16), g_mlp,
        w_up[0].astype(BF16), w_down[0].astype(BF16), g_final[None, :])
```

```python
import functools
import math

import jax
import jax.numpy as jnp
import numpy as np
from jax import lax
from jax.experimental import pallas as pl
from jax.experimental.pallas import tpu as pltpu

N_META = 16
GRID_W = 64
HEAD_DIM = 128
N_Q_HEADS = 4
N_KV_HEADS = 2
Q_GROUP = N_Q_HEADS // N_KV_HEADS
ATTN_W = N_Q_HEADS * HEAD_DIM
KV_W = N_KV_HEADS * HEAD_DIM
N_FOURIER_GROUPS = 4
FOURIER_GROUP_W = 128
FOURIER_W = N_FOURIER_GROUPS * FOURIER_GROUP_W
ROPE_THETA = 10000.0
ROPE_AXIS_DIM = HEAD_DIM // 2
RMS_EPS = 1e-6

V7X_VMEM_BYTES = 64 * 1024 * 1024
V7X_BF16_SUBLANES = 16

TOKEN_TILE = 512
ATTN_Q_TILE = 512
ATTN_KV_CHUNK = 1024
DFT_N2_TILE = 16
DFT_K1_TILE = 8
MAX_UNSHIFTED_SCORE = 64.0
BOUND_SLACK = 1.02

F32 = jnp.float32
BF16 = jnp.bfloat16


def _vmem_limit(estimate_bytes):
    return int(min(max(2 * estimate_bytes, 32 * 1024 * 1024), V7X_VMEM_BYTES - 8 * 1024 * 1024))


def _rms(x, g):
    return x * lax.rsqrt(jnp.mean(x * x, axis=-1, keepdims=True) + RMS_EPS) * g


def _dft_factors(l):
    best = None
    for n1 in range(V7X_BF16_SUBLANES, l, V7X_BF16_SUBLANES):
        if l % n1 == 0:
            n2 = l // n1
            if best is None or abs(n1 - n2) < abs(best[0] - best[1]):
                best = (n1, n2)
    assert best is not None
    return best


def _rope_tables(n_tok, n_rows):
    rows_count = n_tok // GRID_W
    real_row = np.repeat(np.arange(rows_count, dtype=np.float64), GRID_W)
    real_col = np.tile(np.arange(GRID_W, dtype=np.float64), rows_count)
    meta_row = np.full((N_META,), -1.0)
    meta_col = np.arange(N_META, dtype=np.float64)
    pad = np.zeros((n_rows - n_tok - N_META,))
    row = np.concatenate([real_row, meta_row, pad])
    col = np.concatenate([real_col, meta_col, pad])
    inv_freq = ROPE_THETA ** (-np.arange(0, ROPE_AXIS_DIM, 2, dtype=np.float64) / ROPE_AXIS_DIM)
    ang_r = row[:, None] * inv_freq[None, :]
    ang_c = col[:, None] * inv_freq[None, :]
    cos = np.concatenate([np.cos(ang_r)] * 2 + [np.cos(ang_c)] * 2, axis=-1)
    sin = np.concatenate([-np.sin(ang_r), np.sin(ang_r), -np.sin(ang_c), np.sin(ang_c)], axis=-1)
    return jnp.asarray(cos.astype(np.float32)), jnp.asarray(sin.astype(np.float32))


def _channel_dft_tables(l):
    n = np.arange(FOURIER_GROUP_W)
    ang = 2.0 * np.pi * ((n[:, None] * n[None, :]) % FOURIER_GROUP_W) / FOURIER_GROUP_W
    scale = 1.0 / math.sqrt(l * FOURIER_GROUP_W)
    return (np.cos(ang) * scale).astype(np.float32), (np.sin(ang) * scale).astype(np.float32)


def _dft_stage_tables(l, n1, n2, n2_pad):
    s = N_META
    k1 = np.arange(n1)[:, None]
    ang1 = 2.0 * np.pi * (((k1 + s) * n2 * np.arange(n1)[None, :]) % l) / l
    fr, fi = np.cos(ang1), np.sin(ang1)
    stage1 = np.concatenate(
        [np.concatenate([fr, -fi], axis=1), np.concatenate([fi, fr], axis=1)], axis=0).astype(np.float32)
    ang_t = 2.0 * np.pi * (((k1 + s) * (np.arange(n2)[None, :] + s)) % l) / l
    twiddle = np.zeros((2, n1, n2_pad), np.float32)
    twiddle[0, :, :n2] = np.cos(ang_t)
    twiddle[1, :, :n2] = np.sin(ang_t)
    k2 = np.arange(n2)[:, None]
    jj = np.arange(n2)[None, :]
    ang2 = 2.0 * np.pi * ((k2 * (jj + s)) % n2) / n2
    stage2 = np.zeros((n2_pad, n2_pad, 2), np.float32)
    stage2[:n2, :n2, 0] = np.cos(ang2)
    stage2[:n2, :n2, 1] = -np.sin(ang2)
    return stage1, twiddle, stage2.reshape(n2_pad, 2 * n2_pad)


def _fourier_weight_kernel(c_ref, s_ref, w_ref, ab_ref):
    w = w_ref[0]
    a = jnp.dot(c_ref[...], w, preferred_element_type=F32, precision=lax.Precision.HIGHEST)
    b = jnp.dot(s_ref[...], w, preferred_element_type=F32, precision=lax.Precision.HIGHEST)
    ab_ref[0, :, :FOURIER_GROUP_W] = a.astype(BF16)
    ab_ref[0, :, FOURIER_GROUP_W:] = b.astype(BF16)


def _fourier_weights(w_f, l):
    c, s = _channel_dft_tables(l)
    gw = FOURIER_GROUP_W
    return pl.pallas_call(
        _fourier_weight_kernel,
        grid=(N_FOURIER_GROUPS,),
        in_specs=[
            pl.BlockSpec((gw, gw), lambda g: (0, 0)),
            pl.BlockSpec((gw, gw), lambda g: (0, 0)),
            pl.BlockSpec((1, gw, gw), lambda g: (g, 0, 0)),
        ],
        out_specs=pl.BlockSpec((1, gw, 2 * gw), lambda g: (g, 0, 0)),
        out_shape=jax.ShapeDtypeStruct((N_FOURIER_GROUPS, gw, 2 * gw), BF16),
    )(jnp.asarray(c), jnp.asarray(s), w_f)


def _rope(x, cos, sin, lower_half):
    partner = jnp.where(lower_half, pltpu.roll(x, HEAD_DIM - 32, axis=1), pltpu.roll(x, 32, axis=1))
    return x * cos + partner * sin


def _in_proj_kernel(
    x_ref, meta_ref, cos_ref, sin_ref, gmix_ref, win_ref, gq_ref, gk_ref, ab_ref,
    q_ref, k_ref, v_ref, p_ref, qq_ref, *, n_real_tiles, q_scale,
):
    i = pl.program_id(1)
    xt = jnp.where(i == n_real_tiles, meta_ref[...], x_ref[0])
    hn = _rms(xt, gmix_ref[...]).astype(BF16)
    proj = jnp.dot(hn, win_ref[...], preferred_element_type=F32)
    cos = cos_ref[...]
    sin = sin_ref[...]
    lane = lax.broadcasted_iota(jnp.int32, cos.shape, 1)
    lower_half = (lane % ROPE_AXIS_DIM) < (ROPE_AXIS_DIM // 2)
    d = HEAD_DIM
    for h in range(N_Q_HEADS):
        qh = _rope(_rms(proj[:, h * d:(h + 1) * d], gq_ref[...]), cos, sin, lower_half)
        q_ref[0, :, h * d:(h + 1) * d] = (qh * q_scale).astype(BF16)
    for h in range(N_KV_HEADS):
        off = ATTN_W + h * d
        kh = _rope(_rms(proj[:, off:off + d], gk_ref[...]), cos, sin, lower_half)
        k_ref[0, :, h * d:(h + 1) * d] = kh.astype(BF16)
        off = ATTN_W + KV_W + h * d
        v_ref[0, :, 2 * h * d:(2 * h + 1) * d] = proj[:, off:off + d].astype(BF16)
        v_ref[0, :, (2 * h + 1) * d:(2 * h + 2) * d] = jnp.ones((proj.shape[0], d), BF16)
    gw = FOURIER_GROUP_W
    for g in range(N_FOURIER_GROUPS):
        off = ATTN_W + 2 * KV_W + g * gw
        pq = jnp.dot(proj[:, off:off + gw].astype(BF16), ab_ref[g], preferred_element_type=F32)
        p_ref[0, :, g * gw:(g + 1) * gw] = pq[:, :gw].astype(BF16)
        qq_ref[0, :, g * gw:(g + 1) * gw] = pq[:, gw:].astype(BF16)


def _in_proj(x, meta_pad, cos_tab, sin_tab, g_mix, w_in, g_q, g_k, ab, l):
    b, n_tok, d_model = x.shape
    t = TOKEN_TILE
    n_real_tiles = n_tok // t
    n_rows = (n_real_tiles + 1) * t
    in_w = w_in.shape[1]
    q_scale = math.log2(math.e) / math.sqrt(HEAD_DIM)
    const2 = lambda bi, i: (0, 0)
    est = (2 * t * d_model * 4 * 2 + d_model * in_w * 2 * 2 + t * in_w * 4 * 3
           + 2 * t * (2 * ATTN_W + KV_W + 2 * FOURIER_W) * 2)
    return pl.pallas_call(
        functools.partial(_in_proj_kernel, n_real_tiles=n_real_tiles, q_scale=q_scale),
        grid=(b, n_real_tiles + 1),
        in_specs=[
            pl.BlockSpec((1, t, d_model), lambda bi, i: (bi, jnp.minimum(i, n_real_tiles - 1), 0)),
            pl.BlockSpec((t, d_model), const2),
            pl.BlockSpec((t, HEAD_DIM), lambda bi, i: (i, 0)),
            pl.BlockSpec((t, HEAD_DIM), lambda bi, i: (i, 0)),
            pl.BlockSpec((1, d_model), const2),
            pl.BlockSpec((d_model, in_w), const2),
            pl.BlockSpec((1, HEAD_DIM), const2),
            pl.BlockSpec((1, HEAD_DIM), const2),
            pl.BlockSpec((N_FOURIER_GROUPS, FOURIER_GROUP_W, 2 * FOURIER_GROUP_W), lambda bi, i: (0, 0, 0)),
        ],
        out_specs=[
            pl.BlockSpec((1, t, ATTN_W), lambda bi, i: (bi, i, 0)),
            pl.BlockSpec((1, t, KV_W), lambda bi, i: (bi, i, 0)),
            pl.BlockSpec((1, t, 2 * KV_W), lambda bi, i: (bi, i, 0)),
            pl.BlockSpec((1, t, FOURIER_W), lambda bi, i: (bi, i, 0)),
            pl.BlockSpec((1, t, FOURIER_W), lambda bi, i: (bi, i, 0)),
        ],
        out_shape=[
            jax.ShapeDtypeStruct((b, n_rows, ATTN_W), BF16),
            jax.ShapeDtypeStruct((b, n_rows, KV_W), BF16),
            jax.ShapeDtypeStruct((b, n_rows, 2 * KV_W), BF16),
            jax.ShapeDtypeStruct((b, l, FOURIER_W), BF16),
            jax.ShapeDtypeStruct((b, l, FOURIER_W), BF16),
        ],
        compiler_params=pltpu.CompilerParams(
            dimension_semantics=("arbitrary", "arbitrary"), vmem_limit_bytes=_vmem_limit(est)),
    )(x, meta_pad, cos_tab, sin_tab, g_mix, w_in, g_q, g_k, ab)


_NT = (((1,), (1,)), ((), ()))


def _attention_bounded_kernel(q_ref, k_ref, v_ref, km_ref, vm_ref, o_ref, q2_ref, p0_ref, p1_ref, acc_ref,
                              *, n_chunks, chunk):
    d = HEAD_DIM
    tq = q_ref.shape[1]
    q2_ref[:tq] = q_ref[0, :, :d]
    q2_ref[tq:] = q_ref[0, :, d:]

    def probs(c, p_ref):
        start = pl.multiple_of(c * chunk, chunk)
        s = lax.dot_general(q2_ref[...], k_ref[0, pl.ds(start, chunk), :], _NT, preferred_element_type=F32)
        p_ref[...] = jnp.exp2(s).astype(BF16)

    def accumulate(c, p_ref):
        start = pl.multiple_of(c * chunk, chunk)
        acc_ref[...] += jnp.dot(p_ref[...], v_ref[0, pl.ds(start, chunk), :], preferred_element_type=F32)

    s_meta = lax.dot_general(q2_ref[...], km_ref[0], _NT, preferred_element_type=F32)
    acc_ref[...] = jnp.dot(jnp.exp2(s_meta).astype(BF16), vm_ref[0], preferred_element_type=F32)

    probs(0, p0_ref)

    def body(j, carry):
        c = 2 * j
        probs(c + 1, p1_ref)
        accumulate(c, p0_ref)
        probs(jnp.minimum(c + 2, n_chunks - 1), p0_ref)
        accumulate(c + 1, p1_ref)
        return carry

    lax.fori_loop(0, n_chunks // 2, body, 0)

    acc = acc_ref[...]
    out = acc[:, :d] / acc[:, d:]
    o_ref[0, :, :d] = out[:tq]
    o_ref[0, :, d:] = out[tq:]


def _attention_kernel(q_ref, k_ref, v_ref, km_ref, vm_ref, o_ref, q2_ref, s0_ref, s1_ref, m_ref, acc_ref,
                      *, n_chunks, chunk):
    d = HEAD_DIM
    tq = q_ref.shape[1]
    q2_ref[:tq] = q_ref[0, :, :d]
    q2_ref[tq:] = q_ref[0, :, d:]

    def scores(c, s_ref):
        start = pl.multiple_of(c * chunk, chunk)
        s_ref[...] = lax.dot_general(
            q2_ref[...], k_ref[0, pl.ds(start, chunk), :], _NT, preferred_element_type=F32)

    def update(c, s_ref):
        start = pl.multiple_of(c * chunk, chunk)
        s = s_ref[...]
        m_old = m_ref[...]
        m_new = jnp.maximum(m_old, jnp.max(s, axis=-1, keepdims=True))
        p = jnp.exp2(s - m_new).astype(BF16)
        pv = jnp.dot(p, v_ref[0, pl.ds(start, chunk), :], preferred_element_type=F32)
        acc_ref[...] = acc_ref[...] * jnp.exp2(m_old - m_new) + pv
        m_ref[...] = m_new

    s_meta = lax.dot_general(q2_ref[...], km_ref[0], _NT, preferred_element_type=F32)
    m_meta = jnp.max(s_meta, axis=-1, keepdims=True)
    m_ref[...] = m_meta
    acc_ref[...] = jnp.dot(jnp.exp2(s_meta - m_meta).astype(BF16), vm_ref[0], preferred_element_type=F32)

    scores(0, s0_ref)

    def body(j, carry):
        c = 2 * j
        scores(c + 1, s1_ref)
        update(c, s0_ref)
        scores(jnp.minimum(c + 2, n_chunks - 1), s0_ref)
        update(c + 1, s1_ref)
        return carry

    lax.fori_loop(0, n_chunks // 2, body, 0)

    acc = acc_ref[...]
    out = acc[:, :d] / acc[:, d:]
    o_ref[0, :, :d] = out[:tq]
    o_ref[0, :, d:] = out[tq:]


def _attention_call(q, k, v, n_tok, bounded):
    b = q.shape[0]
    tq = min(ATTN_Q_TILE, n_tok)
    chunk = min(ATTN_KV_CHUNK, n_tok)
    d = HEAD_DIM
    gq = Q_GROUP * d
    m = Q_GROUP * tq
    meta_blk = n_tok // N_META
    assert (n_tok // chunk) % 2 == 0
    est = 2 * (tq * gq * 2 + n_tok * d * 2 + n_tok * 2 * d * 2 + tq * gq * 4) + m * (
        chunk * 4 * 3 + chunk * 2 + 2 * d * 4 * 3)
    if bounded:
        body = _attention_bounded_kernel
        scratch = [pltpu.VMEM((m, d), BF16), pltpu.VMEM((m, chunk), BF16), pltpu.VMEM((m, chunk), BF16),
                   pltpu.VMEM((m, 2 * d), F32)]
    else:
        body = _attention_kernel
        scratch = [pltpu.VMEM((m, d), BF16), pltpu.VMEM((m, chunk), F32), pltpu.VMEM((m, chunk), F32),
                   pltpu.VMEM((m, 1), F32), pltpu.VMEM((m, 2 * d), F32)]
    return pl.pallas_call(
        functools.partial(body, n_chunks=n_tok // chunk, chunk=chunk),
        grid=(b, N_KV_HEADS, n_tok // tq),
        in_specs=[
            pl.BlockSpec((1, tq, gq), lambda bi, g, i: (bi, i, g)),
            pl.BlockSpec((1, n_tok, d), lambda bi, g, i: (bi, 0, g)),
            pl.BlockSpec((1, n_tok, 2 * d), lambda bi, g, i: (bi, 0, g)),
            pl.BlockSpec((1, N_META, d), lambda bi, g, i: (bi, meta_blk, g)),
            pl.BlockSpec((1, N_META, 2 * d), lambda bi, g, i: (bi, meta_blk, g)),
        ],
        out_specs=pl.BlockSpec((1, tq, gq), lambda bi, g, i: (bi, i, g)),
        out_shape=jax.ShapeDtypeStruct((b, n_tok, ATTN_W), F32),
        scratch_shapes=scratch,
        compiler_params=pltpu.CompilerParams(
            dimension_semantics=("arbitrary", "arbitrary", "arbitrary"), vmem_limit_bytes=_vmem_limit(est)),
    )(q, k, v, k, v)


def _attention(q, k, v, g_q, g_k, n_tok):
    score_bound = (math.log2(math.e) / math.sqrt(HEAD_DIM)) * HEAD_DIM * BOUND_SLACK * (
        jnp.max(jnp.abs(g_q)) * jnp.max(jnp.abs(g_k)))
    return lax.cond(
        score_bound <= MAX_UNSHIFTED_SCORE,
        lambda: _attention_call(q, k, v, n_tok, True),
        lambda: _attention_call(q, k, v, n_tok, False))


def _dft1_kernel(f_ref, t_ref, p_ref, q_ref, g_ref, *, n2, cols):
    step = pl.program_id(0)
    n1 = t_ref.shape[2]
    n_sub = t_ref.shape[3]
    f = f_ref[...].astype(BF16)
    for j in range(n_sub):
        z = jnp.concatenate(
            [p_ref[0, :, j * cols:(j + 1) * cols], q_ref[0, :, j * cols:(j + 1) * cols]], axis=0)
        res = jnp.dot(f, z, preferred_element_type=F32)
        re, im = res[:n1], res[n1:]
        tr = t_ref[0, 0, :, j:j + 1]
        ti = t_ref[0, 1, :, j:j + 1]
        valid = step * n_sub + j < n2
        g_ref[0, j, :n1] = jnp.where(valid, tr * re - ti * im, 0.0).astype(BF16)
        g_ref[0, j, n1:] = jnp.where(valid, tr * im + ti * re, 0.0).astype(BF16)


def _dft2_kernel(f_ref, g_ref, y_ref, *, n2):
    f = f_ref[...].astype(BF16)
    for kk in range(g_ref.shape[1]):
        res = jnp.dot(f, g_ref[0, kk], preferred_element_type=F32)
        y_ref[0, kk] = res[:n2]


def _sequence_dft(p, q, l):
    b = p.shape[0]
    cols = p.shape[2]
    n1, n2 = _dft_factors(l)
    tn = DFT_N2_TILE
    n2_pad = -(-n2 // tn) * tn
    stage1, twiddle, stage2 = _dft_stage_tables(l, n1, n2, n2_pad)
    twiddle = twiddle.reshape(2, n1, n2_pad // tn, tn).transpose(2, 0, 1, 3)
    pv = p.reshape(b, n1, n2 * cols)
    qv = q.reshape(b, n1, n2 * cols)
    est1 = 2 * (4 * n1 * n1 * 4 + 2 * n1 * tn * cols * 2 + tn * 2 * n1 * cols * 4) + 8 * n1 * cols * 4
    g = pl.pallas_call(
        functools.partial(_dft1_kernel, n2=n2, cols=cols),
        grid=(n2_pad // tn, b),
        in_specs=[
            pl.BlockSpec((2 * n1, 2 * n1), lambda i, bi: (0, 0)),
            pl.BlockSpec((1, 2, n1, tn), lambda i, bi: (i, 0, 0, 0)),
            pl.BlockSpec((1, n1, tn * cols), lambda i, bi: (bi, 0, i)),
            pl.BlockSpec((1, n1, tn * cols), lambda i, bi: (bi, 0, i)),
        ],
        out_specs=pl.BlockSpec((1, tn, 2 * n1, cols), lambda i, bi: (bi, i, 0, 0)),
        out_shape=jax.ShapeDtypeStruct((b, n2_pad, 2 * n1, cols), BF16),
        compiler_params=pltpu.CompilerParams(
            dimension_semantics=("arbitrary", "arbitrary"), vmem_limit_bytes=_vmem_limit(est1)),
    )(jnp.asarray(stage1), jnp.asarray(twiddle), pv, qv)

    gv = g.reshape(b, 2 * n2_pad, n1, cols).transpose(0, 2, 1, 3)
    tk = DFT_K1_TILE
    assert n1 % tk == 0
    est2 = 2 * (2 * n2_pad * n2_pad * 4 + tk * 2 * n2_pad * cols * 2 + tk * n2 * cols * 4) + 3 * n2_pad * cols * 4
    y = pl.pallas_call(
        functools.partial(_dft2_kernel, n2=n2),
        grid=(b, n1 // tk),
        in_specs=[
            pl.BlockSpec((n2_pad, 2 * n2_pad), lambda bi, i: (0, 0)),
            pl.BlockSpec((1, tk, 2 * n2_pad, cols), lambda bi, i: (bi, i, 0, 0)),
        ],
        out_specs=pl.BlockSpec((1, tk, n2, cols), lambda bi, i: (bi, i, 0, 0)),
        out_shape=jax.ShapeDtypeStruct((b, n1, n2, cols), F32),
        compiler_params=pltpu.CompilerParams(
            dimension_semantics=("arbitrary", "arbitrary"), vmem_limit_bytes=_vmem_limit(est2)),
    )(jnp.asarray(stage2), gv)
    return y.transpose(0, 2, 1, 3).reshape(b, l, cols)


def _out_mlp_kernel(
    x_ref, a_ref, f_ref, ga_ref, gf_ref, wout_ref, gmlp_ref, wup_ref, wdown_ref, gfin_ref, o_ref,
):
    an = _rms(a_ref[0], ga_ref[...]).astype(BF16)
    fn = _rms(f_ref[0], gf_ref[...]).astype(BF16)
    h = x_ref[0]
    h = h + jnp.dot(an, wout_ref[:ATTN_W, :], preferred_element_type=F32)
    h = h + jnp.dot(fn, wout_ref[ATTN_W:, :], preferred_element_type=F32)
    m = _rms(h, gmlp_ref[...]).astype(BF16)
    act = jnp.maximum(jnp.dot(m, wup_ref[...], preferred_element_type=F32), 0.0)
    out = h + jnp.dot((act * act).astype(BF16), wdown_ref[...], preferred_element_type=F32)
    o_ref[0] = _rms(out, gfin_ref[...])


def _out_mlp(x, attn, four, g_attn_out, g_fourier_out, w_out, g_mlp, w_up, w_down, g_final):
    b, n_tok, d_model = x.shape
    d_ff = w_up.shape[1]
    t = TOKEN_TILE
    const2 = lambda bi, i: (0, 0)
    tile3 = lambda bi, i: (bi, i, 0)
    est = (2 * (2 * t * d_model * 4 + t * ATTN_W * 4 + t * FOURIER_W * 4)
           + 2 * (d_model * d_model + 2 * d_model * d_ff) * 2 + t * (4 * d_model * 4 + d_ff * 6))
    return pl.pallas_call(
        _out_mlp_kernel,
        grid=(b, n_tok // t),
        in_specs=[
            pl.BlockSpec((1, t, d_model), tile3),
            pl.BlockSpec((1, t, ATTN_W), tile3),
            pl.BlockSpec((1, t, FOURIER_W), tile3),
            pl.BlockSpec((1, ATTN_W), const2),
            pl.BlockSpec((1, FOURIER_W), const2),
            pl.BlockSpec((d_model, d_model), const2),
            pl.BlockSpec((1, d_model), const2),
            pl.BlockSpec((d_model, d_ff), const2),
            pl.BlockSpec((d_ff, d_model), const2),
            pl.BlockSpec((1, d_model), const2),
        ],
        out_specs=pl.BlockSpec((1, t, d_model), tile3),
        out_shape=jax.ShapeDtypeStruct((b, n_tok, d_model), F32),
        compiler_params=pltpu.CompilerParams(
            dimension_semantics=("arbitrary", "arbitrary"), vmem_limit_bytes=_vmem_limit(est)),
    )(x, attn, four, g_attn_out, g_fourier_out, w_out, g_mlp, w_up, w_down, g_final)


def kernel(x, meta_tokens, g_mix, w_in, g_q, g_k, w_fourier, g_attn_out, g_fourier_out, w_out,
           g_mlp, w_up, w_down, g_final):
    assert g_mix.shape[0] == 1, "meta-token rows are only carried for a single layer"
    b, n_tok, d_model = x.shape
    l = n_tok + N_META
    t = TOKEN_TILE
    assert n_tok % t == 0 and n_tok % GRID_W == 0

    cos_tab, sin_tab = _rope_tables(n_tok, n_tok + t)
    meta_pad = jnp.pad(meta_tokens.astype(F32), ((0, t - N_META), (0, 0)))
    ab = _fourier_weights(w_fourier[0], l)
    q, k, v, p, qq = _in_proj(
        x, meta_pad, cos_tab, sin_tab, g_mix, w_in[0].astype(BF16), g_q, g_k, ab, l)
    attn = _attention(q, k, v, g_q, g_k, n_tok)
    four = _sequence_dft(p, qq, l)
    return _out_mlp(
        x, attn, four, g_attn_out, g_fourier_out, w_out[0].astype(BF16), g_mlp,
        w_up[0].astype(BF16), w_down[0].astype(BF16), g_final[None, :])
```

```python
import functools
import math

import jax
import jax.numpy as jnp
import numpy as np
from jax import lax
from jax.experimental import pallas as pl
from jax.experimental.pallas import tpu as pltpu

N_META = 16
GRID_W = 64
HEAD_DIM = 128
N_Q_HEADS = 4
N_KV_HEADS = 2
Q_GROUP = N_Q_HEADS // N_KV_HEADS
ATTN_W = N_Q_HEADS * HEAD_DIM
KV_W = N_KV_HEADS * HEAD_DIM
N_FOURIER_GROUPS = 4
FOURIER_GROUP_W = 128
FOURIER_W = N_FOURIER_GROUPS * FOURIER_GROUP_W
ROPE_THETA = 10000.0
ROPE_AXIS_DIM = HEAD_DIM // 2
RMS_EPS = 1e-6

V7X_VMEM_BYTES = 64 * 1024 * 1024
V7X_BF16_SUBLANES = 16

TOKEN_TILE = 512
IN_PROJ_ROWS = 256
ATTN_Q_TILE = 512
ATTN_KV_CHUNK = 1024
DFT_N2_TILE = 16
DFT_K1_TILE = 8
MAX_UNSHIFTED_SCORE = 64.0
BOUND_SLACK = 1.02

F32 = jnp.float32
BF16 = jnp.bfloat16


def _vmem_limit(estimate_bytes):
    return int(min(max(2 * estimate_bytes, 32 * 1024 * 1024), V7X_VMEM_BYTES - 8 * 1024 * 1024))


def _rms(x, g):
    return x * lax.rsqrt(jnp.mean(x * x, axis=-1, keepdims=True) + RMS_EPS) * g


def _dft_factors(l):
    best = None
    for n1 in range(V7X_BF16_SUBLANES, l, V7X_BF16_SUBLANES):
        if l % n1 == 0:
            n2 = l // n1
            if best is None or abs(n1 - n2) < abs(best[0] - best[1]):
                best = (n1, n2)
    assert best is not None
    return best


def _rope_tables(n_tok, n_rows):
    rows_count = n_tok // GRID_W
    real_row = np.repeat(np.arange(rows_count, dtype=np.float64), GRID_W)
    real_col = np.tile(np.arange(GRID_W, dtype=np.float64), rows_count)
    meta_row = np.full((N_META,), -1.0)
    meta_col = np.arange(N_META, dtype=np.float64)
    pad = np.zeros((n_rows - n_tok - N_META,))
    row = np.concatenate([real_row, meta_row, pad])
    col = np.concatenate([real_col, meta_col, pad])
    inv_freq = ROPE_THETA ** (-np.arange(0, ROPE_AXIS_DIM, 2, dtype=np.float64) / ROPE_AXIS_DIM)
    ang_r = row[:, None] * inv_freq[None, :]
    ang_c = col[:, None] * inv_freq[None, :]
    cos = np.concatenate([np.cos(ang_r)] * 2 + [np.cos(ang_c)] * 2, axis=-1)
    sin = np.concatenate([-np.sin(ang_r), np.sin(ang_r), -np.sin(ang_c), np.sin(ang_c)], axis=-1)
    return jnp.asarray(cos.astype(np.float32)), jnp.asarray(sin.astype(np.float32))


def _channel_dft_tables(l):
    n = np.arange(FOURIER_GROUP_W)
    ang = 2.0 * np.pi * ((n[:, None] * n[None, :]) % FOURIER_GROUP_W) / FOURIER_GROUP_W
    scale = 1.0 / math.sqrt(l * FOURIER_GROUP_W)
    return (np.cos(ang) * scale).astype(np.float32), (np.sin(ang) * scale).astype(np.float32)


def _dft_stage_tables(l, n1, n2, n2_pad):
    s = N_META
    k1 = np.arange(n1)[:, None]
    ang1 = 2.0 * np.pi * (((k1 + s) * n2 * np.arange(n1)[None, :]) % l) / l
    fr, fi = np.cos(ang1), np.sin(ang1)
    stage1 = np.concatenate(
        [np.concatenate([fr, -fi], axis=1), np.concatenate([fi, fr], axis=1)], axis=0).astype(np.float32)
    ang_t = 2.0 * np.pi * (((k1 + s) * (np.arange(n2)[None, :] + s)) % l) / l
    twiddle = np.zeros((2, n1, n2_pad), np.float32)
    twiddle[0, :, :n2] = np.cos(ang_t)
    twiddle[1, :, :n2] = np.sin(ang_t)
    k2 = np.arange(n2)[:, None]
    jj = np.arange(n2)[None, :]
    ang2 = 2.0 * np.pi * ((k2 * (jj + s)) % n2) / n2
    stage2 = np.zeros((n2_pad, n2_pad, 2), np.float32)
    stage2[:n2, :n2, 0] = np.cos(ang2)
    stage2[:n2, :n2, 1] = -np.sin(ang2)
    return stage1, twiddle, stage2.reshape(n2_pad, 2 * n2_pad)


def _fourier_weight_kernel(c_ref, s_ref, w_ref, ab_ref):
    w = w_ref[0]
    a = jnp.dot(c_ref[...], w, preferred_element_type=F32, precision=lax.Precision.HIGHEST)
    b = jnp.dot(s_ref[...], w, preferred_element_type=F32, precision=lax.Precision.HIGHEST)
    ab_ref[0, :, :FOURIER_GROUP_W] = a.astype(BF16)
    ab_ref[0, :, FOURIER_GROUP_W:] = b.astype(BF16)


def _fourier_weights(w_f, l):
    c, s = _channel_dft_tables(l)
    gw = FOURIER_GROUP_W
    return pl.pallas_call(
        _fourier_weight_kernel,
        grid=(N_FOURIER_GROUPS,),
        in_specs=[
            pl.BlockSpec((gw, gw), lambda g: (0, 0)),
            pl.BlockSpec((gw, gw), lambda g: (0, 0)),
            pl.BlockSpec((1, gw, gw), lambda g: (g, 0, 0)),
        ],
        out_specs=pl.BlockSpec((1, gw, 2 * gw), lambda g: (g, 0, 0)),
        out_shape=jax.ShapeDtypeStruct((N_FOURIER_GROUPS, gw, 2 * gw), BF16),
    )(jnp.asarray(c), jnp.asarray(s), w_f)


def _rope(x, cos, sin, lower_half):
    partner = jnp.where(lower_half, pltpu.roll(x, HEAD_DIM - 32, axis=1), pltpu.roll(x, 32, axis=1))
    return x * cos + partner * sin


def _in_proj_kernel(
    x_ref, meta_ref, cos_ref, sin_ref, gmix_ref, win_ref, gq_ref, gk_ref, ab_ref,
    q_ref, k_ref, v_ref, p_ref, qq_ref, *, n_real_tiles, q_scale,
):
    i = pl.program_id(1)
    is_meta = i == n_real_tiles
    d = HEAD_DIM
    gw = FOURIER_GROUP_W
    lane = lax.broadcasted_iota(jnp.int32, (IN_PROJ_ROWS, d), 1)
    lower_half = (lane % ROPE_AXIS_DIM) < (ROPE_AXIS_DIM // 2)
    for r in range(x_ref.shape[1] // IN_PROJ_ROWS):
        rows = pl.ds(r * IN_PROJ_ROWS, IN_PROJ_ROWS)
        xt = jnp.where(is_meta, meta_ref[rows, :], x_ref[0, rows, :])
        hn = _rms(xt, gmix_ref[...]).astype(BF16)
        proj = jnp.dot(hn, win_ref[...], preferred_element_type=F32)
        cos = cos_ref[rows, :]
        sin = sin_ref[rows, :]
        for h in range(N_Q_HEADS):
            qh = _rope(_rms(proj[:, h * d:(h + 1) * d], gq_ref[...]), cos, sin, lower_half)
            q_ref[0, rows, h * d:(h + 1) * d] = (qh * q_scale).astype(BF16)
        for h in range(N_KV_HEADS):
            off = ATTN_W + h * d
            kh = _rope(_rms(proj[:, off:off + d], gk_ref[...]), cos, sin, lower_half)
            k_ref[0, rows, h * d:(h + 1) * d] = kh.astype(BF16)
            off = ATTN_W + KV_W + h * d
            v_ref[0, rows, 2 * h * d:(2 * h + 1) * d] = proj[:, off:off + d].astype(BF16)
            v_ref[0, rows, (2 * h + 1) * d:(2 * h + 2) * d] = jnp.ones((IN_PROJ_ROWS, d), BF16)
        for g in range(N_FOURIER_GROUPS):
            off = ATTN_W + 2 * KV_W + g * gw
            pq = jnp.dot(proj[:, off:off + gw].astype(BF16), ab_ref[g], preferred_element_type=F32)
            p_ref[0, rows, g * gw:(g + 1) * gw] = pq[:, :gw].astype(BF16)
            qq_ref[0, rows, g * gw:(g + 1) * gw] = pq[:, gw:].astype(BF16)


def _in_proj(x, meta_pad, cos_tab, sin_tab, g_mix, w_in, g_q, g_k, ab, l):
    b, n_tok, d_model = x.shape
    t = TOKEN_TILE
    n_real_tiles = n_tok // t
    n_rows = (n_real_tiles + 1) * t
    in_w = w_in.shape[1]
    q_scale = math.log2(math.e) / math.sqrt(HEAD_DIM)
    const2 = lambda bi, i: (0, 0)
    est = (2 * t * d_model * 4 * 2 + d_model * in_w * 2 * 2 + t * in_w * 4 * 3
           + 2 * t * (2 * ATTN_W + KV_W + 2 * FOURIER_W) * 2)
    return pl.pallas_call(
        functools.partial(_in_proj_kernel, n_real_tiles=n_real_tiles, q_scale=q_scale),
        grid=(b, n_real_tiles + 1),
        in_specs=[
            pl.BlockSpec((1, t, d_model), lambda bi, i: (bi, jnp.minimum(i, n_real_tiles - 1), 0)),
            pl.BlockSpec((t, d_model), const2),
            pl.BlockSpec((t, HEAD_DIM), lambda bi, i: (i, 0)),
            pl.BlockSpec((t, HEAD_DIM), lambda bi, i: (i, 0)),
            pl.BlockSpec((1, d_model), const2),
            pl.BlockSpec((d_model, in_w), const2),
            pl.BlockSpec((1, HEAD_DIM), const2),
            pl.BlockSpec((1, HEAD_DIM), const2),
            pl.BlockSpec((N_FOURIER_GROUPS, FOURIER_GROUP_W, 2 * FOURIER_GROUP_W), lambda bi, i: (0, 0, 0)),
        ],
        out_specs=[
            pl.BlockSpec((1, t, ATTN_W), lambda bi, i: (bi, i, 0)),
            pl.BlockSpec((1, t, KV_W), lambda bi, i: (bi, i, 0)),
            pl.BlockSpec((1, t, 2 * KV_W), lambda bi, i: (bi, i, 0)),
            pl.BlockSpec((1, t, FOURIER_W), lambda bi, i: (bi, i, 0)),
            pl.BlockSpec((1, t, FOURIER_W), lambda bi, i: (bi, i, 0)),
        ],
        out_shape=[
            jax.ShapeDtypeStruct((b, n_rows, ATTN_W), BF16),
            jax.ShapeDtypeStruct((b, n_rows, KV_W), BF16),
            jax.ShapeDtypeStruct((b, n_rows, 2 * KV_W), BF16),
            jax.ShapeDtypeStruct((b, l, FOURIER_W), BF16),
            jax.ShapeDtypeStruct((b, l, FOURIER_W), BF16),
        ],
        compiler_params=pltpu.CompilerParams(
            dimension_semantics=("arbitrary", "arbitrary"), vmem_limit_bytes=_vmem_limit(est)),
    )(x, meta_pad, cos_tab, sin_tab, g_mix, w_in, g_q, g_k, ab)


_NT = (((1,), (1,)), ((), ()))


def _attention_bounded_kernel(q_ref, qn_ref, k_ref, v_ref, km_ref, vm_ref, o_ref, q2_ref, p0_ref, p1_ref, acc_ref,
                              *, n_chunks, chunk):
    d = HEAD_DIM
    tq = q_ref.shape[1]
    first_tile = pl.program_id(2) == 0
    q2_ref[0, :tq] = q_ref[0, :, :d]
    q2_ref[0, tq:] = q_ref[0, :, d:]
    q2_ref[1, :tq] = qn_ref[0, :, :d]
    q2_ref[1, tq:] = qn_ref[0, :, d:]

    def probs(tile, c, p_ref):
        start = pl.multiple_of(c * chunk, chunk)
        s = lax.dot_general(q2_ref[tile], k_ref[0, pl.ds(start, chunk), :], _NT, preferred_element_type=F32)
        p_ref[...] = jnp.exp2(s).astype(BF16)

    def accumulate(c, p_ref):
        start = pl.multiple_of(c * chunk, chunk)
        acc_ref[...] += jnp.dot(p_ref[...], v_ref[0, pl.ds(start, chunk), :], preferred_element_type=F32)

    s_meta = lax.dot_general(q2_ref[0], km_ref[0], _NT, preferred_element_type=F32)
    acc_ref[...] = jnp.dot(jnp.exp2(s_meta).astype(BF16), vm_ref[0], preferred_element_type=F32)

    @pl.when(first_tile)
    def _():
        probs(0, 0, p0_ref)

    def body(j, carry):
        c = 2 * j
        probs(0, c + 1, p1_ref)
        accumulate(c, p0_ref)
        wraps = c + 2 >= n_chunks
        probs(wraps.astype(jnp.int32), jnp.where(wraps, 0, c + 2), p0_ref)
        accumulate(c + 1, p1_ref)
        return carry

    lax.fori_loop(0, n_chunks // 2, body, 0)

    acc = acc_ref[...]
    out = acc[:, :d] / acc[:, d:]
    o_ref[0, :, :d] = out[:tq]
    o_ref[0, :, d:] = out[tq:]


def _attention_kernel(q_ref, k_ref, v_ref, km_ref, vm_ref, o_ref, q2_ref, s0_ref, s1_ref, m_ref, acc_ref,
                      *, n_chunks, chunk):
    d = HEAD_DIM
    tq = q_ref.shape[1]
    q2_ref[:tq] = q_ref[0, :, :d]
    q2_ref[tq:] = q_ref[0, :, d:]

    def scores(c, s_ref):
        start = pl.multiple_of(c * chunk, chunk)
        s_ref[...] = lax.dot_general(
            q2_ref[...], k_ref[0, pl.ds(start, chunk), :], _NT, preferred_element_type=F32)

    def update(c, s_ref):
        start = pl.multiple_of(c * chunk, chunk)
        s = s_ref[...]
        m_old = m_ref[...]
        m_new = jnp.maximum(m_old, jnp.max(s, axis=-1, keepdims=True))
        p = jnp.exp2(s - m_new).astype(BF16)
        pv = jnp.dot(p, v_ref[0, pl.ds(start, chunk), :], preferred_element_type=F32)
        acc_ref[...] = acc_ref[...] * jnp.exp2(m_old - m_new) + pv
        m_ref[...] = m_new

    s_meta = lax.dot_general(q2_ref[...], km_ref[0], _NT, preferred_element_type=F32)
    m_meta = jnp.max(s_meta, axis=-1, keepdims=True)
    m_ref[...] = m_meta
    acc_ref[...] = jnp.dot(jnp.exp2(s_meta - m_meta).astype(BF16), vm_ref[0], preferred_element_type=F32)

    scores(0, s0_ref)

    def body(j, carry):
        c = 2 * j
        scores(c + 1, s1_ref)
        update(c, s0_ref)
        scores(jnp.minimum(c + 2, n_chunks - 1), s0_ref)
        update(c + 1, s1_ref)
        return carry

    lax.fori_loop(0, n_chunks // 2, body, 0)

    acc = acc_ref[...]
    out = acc[:, :d] / acc[:, d:]
    o_ref[0, :, :d] = out[:tq]
    o_ref[0, :, d:] = out[tq:]


def _attention_call(q, k, v, n_tok, bounded):
    b = q.shape[0]
    tq = min(ATTN_Q_TILE, n_tok)
    chunk = min(ATTN_KV_CHUNK, n_tok)
    d = HEAD_DIM
    gq = Q_GROUP * d
    m = Q_GROUP * tq
    meta_blk = n_tok // N_META
    assert (n_tok // chunk) % 2 == 0
    est = 2 * (tq * gq * 2 + n_tok * d * 2 + n_tok * 2 * d * 2 + tq * gq * 4) + m * (
        chunk * 4 * 3 + chunk * 2 + 2 * d * 4 * 3)
    n_tiles = n_tok // tq
    q_spec = pl.BlockSpec((1, tq, gq), lambda bi, g, i: (bi, i, g))
    if bounded:
        body = _attention_bounded_kernel
        scratch = [pltpu.VMEM((2, m, d), BF16), pltpu.VMEM((m, chunk), BF16), pltpu.VMEM((m, chunk), BF16),
                   pltpu.VMEM((m, 2 * d), F32)]
        q_specs = [q_spec, pl.BlockSpec((1, tq, gq), lambda bi, g, i: (bi, jnp.minimum(i + 1, n_tiles - 1), g))]
        q_args = (q, q)
    else:
        body = _attention_kernel
        scratch = [pltpu.VMEM((m, d), BF16), pltpu.VMEM((m, chunk), F32), pltpu.VMEM((m, chunk), F32),
                   pltpu.VMEM((m, 1), F32), pltpu.VMEM((m, 2 * d), F32)]
        q_specs = [q_spec]
        q_args = (q,)
    return pl.pallas_call(
        functools.partial(body, n_chunks=n_tok // chunk, chunk=chunk),
        grid=(b, N_KV_HEADS, n_tiles),
        in_specs=q_specs + [
            pl.BlockSpec((1, n_tok, d), lambda bi, g, i: (bi, 0, g)),
            pl.BlockSpec((1, n_tok, 2 * d), lambda bi, g, i: (bi, 0, g)),
            pl.BlockSpec((1, N_META, d), lambda bi, g, i: (bi, meta_blk, g)),
            pl.BlockSpec((1, N_META, 2 * d), lambda bi, g, i: (bi, meta_blk, g)),
        ],
        out_specs=pl.BlockSpec((1, tq, gq), lambda bi, g, i: (bi, i, g)),
        out_shape=jax.ShapeDtypeStruct((b, n_tok, ATTN_W), F32),
        scratch_shapes=scratch,
        compiler_params=pltpu.CompilerParams(
            dimension_semantics=("arbitrary", "arbitrary", "arbitrary"), vmem_limit_bytes=_vmem_limit(est)),
    )(*q_args, k, v, k, v)


def _attention(q, k, v, g_q, g_k, n_tok):
    score_bound = (math.log2(math.e) / math.sqrt(HEAD_DIM)) * HEAD_DIM * BOUND_SLACK * (
        jnp.max(jnp.abs(g_q)) * jnp.max(jnp.abs(g_k)))
    return lax.cond(
        score_bound <= MAX_UNSHIFTED_SCORE,
        lambda: _attention_call(q, k, v, n_tok, True),
        lambda: _attention_call(q, k, v, n_tok, False))


def _dft1_kernel(f_ref, t_ref, p_ref, q_ref, g_ref, *, n2, cols):
    step = pl.program_id(0)
    n1 = t_ref.shape[2]
    n_sub = t_ref.shape[3]
    f = f_ref[...].astype(BF16)
    for j in range(n_sub):
        z = jnp.concatenate(
            [p_ref[0, :, j * cols:(j + 1) * cols], q_ref[0, :, j * cols:(j + 1) * cols]], axis=0)
        res = jnp.dot(f, z, preferred_element_type=F32)
        re, im = res[:n1], res[n1:]
        tr = t_ref[0, 0, :, j:j + 1]
        ti = t_ref[0, 1, :, j:j + 1]
        valid = step * n_sub + j < n2
        g_ref[0, j, :n1] = jnp.where(valid, tr * re - ti * im, 0.0).astype(BF16)
        g_ref[0, j, n1:] = jnp.where(valid, tr * im + ti * re, 0.0).astype(BF16)


def _dft2_kernel(f_ref, g_ref, y_ref, *, n2):
    f = f_ref[...].astype(BF16)
    for kk in range(g_ref.shape[1]):
        res = jnp.dot(f, g_ref[0, kk], preferred_element_type=F32)
        y_ref[0, kk] = res[:n2]


def _sequence_dft(p, q, l):
    b = p.shape[0]
    cols = p.shape[2]
    n1, n2 = _dft_factors(l)
    tn = DFT_N2_TILE
    n2_pad = -(-n2 // tn) * tn
    stage1, twiddle, stage2 = _dft_stage_tables(l, n1, n2, n2_pad)
    twiddle = twiddle.reshape(2, n1, n2_pad // tn, tn).transpose(2, 0, 1, 3)
    pv = p.reshape(b, n1, n2 * cols)
    qv = q.reshape(b, n1, n2 * cols)
    est1 = 2 * (4 * n1 * n1 * 4 + 2 * n1 * tn * cols * 2 + tn * 2 * n1 * cols * 4) + 8 * n1 * cols * 4
    g = pl.pallas_call(
        functools.partial(_dft1_kernel, n2=n2, cols=cols),
        grid=(n2_pad // tn, b),
        in_specs=[
            pl.BlockSpec((2 * n1, 2 * n1), lambda i, bi: (0, 0)),
            pl.BlockSpec((1, 2, n1, tn), lambda i, bi: (i, 0, 0, 0)),
            pl.BlockSpec((1, n1, tn * cols), lambda i, bi: (bi, 0, i)),
            pl.BlockSpec((1, n1, tn * cols), lambda i, bi: (bi, 0, i)),
        ],
        out_specs=pl.BlockSpec((1, tn, 2 * n1, cols), lambda i, bi: (bi, i, 0, 0)),
        out_shape=jax.ShapeDtypeStruct((b, n2_pad, 2 * n1, cols), BF16),
        compiler_params=pltpu.CompilerParams(
            dimension_semantics=("arbitrary", "arbitrary"), vmem_limit_bytes=_vmem_limit(est1)),
    )(jnp.asarray(stage1), jnp.asarray(twiddle), pv, qv)

    gv = g.reshape(b, 2 * n2_pad, n1, cols).transpose(0, 2, 1, 3)
    tk = DFT_K1_TILE
    assert n1 % tk == 0
    est2 = 2 * (2 * n2_pad * n2_pad * 4 + tk * 2 * n2_pad * cols * 2 + tk * n2 * cols * 4) + 3 * n2_pad * cols * 4
    y = pl.pallas_call(
        functools.partial(_dft2_kernel, n2=n2),
        grid=(b, n1 // tk),
        in_specs=[
            pl.BlockSpec((n2_pad, 2 * n2_pad), lambda bi, i: (0, 0)),
            pl.BlockSpec((1, tk, 2 * n2_pad, cols), lambda bi, i: (bi, i, 0, 0)),
        ],
        out_specs=pl.BlockSpec((1, tk, n2, cols), lambda bi, i: (bi, i, 0, 0)),
        out_shape=jax.ShapeDtypeStruct((b, n1, n2, cols), F32),
        compiler_params=pltpu.CompilerParams(
            dimension_semantics=("arbitrary", "arbitrary"), vmem_limit_bytes=_vmem_limit(est2)),
    )(jnp.asarray(stage2), gv)
    return y.transpose(0, 2, 1, 3).reshape(b, l, cols)


def _out_mlp_kernel(
    x_ref, a_ref, f_ref, ga_ref, gf_ref, wout_ref, gmlp_ref, wup_ref, wdown_ref, gfin_ref, o_ref,
):
    an = _rms(a_ref[0], ga_ref[...]).astype(BF16)
    fn = _rms(f_ref[0], gf_ref[...]).astype(BF16)
    h = x_ref[0]
    h = h + jnp.dot(an, wout_ref[:ATTN_W, :], preferred_element_type=F32)
    h = h + jnp.dot(fn, wout_ref[ATTN_W:, :], preferred_element_type=F32)
    m = _rms(h, gmlp_ref[...]).astype(BF16)
    act = jnp.maximum(jnp.dot(m, wup_ref[...], preferred_element_type=F32), 0.0)
    out = h + jnp.dot((act * act).astype(BF16), wdown_ref[...], preferred_element_type=F32)
    o_ref[0] = _rms(out, gfin_ref[...])


def _out_mlp(x, attn, four, g_attn_out, g_fourier_out, w_out, g_mlp, w_up, w_down, g_final):
    b, n_tok, d_model = x.shape
    d_ff = w_up.shape[1]
    t = TOKEN_TILE
    const2 = lambda bi, i: (0, 0)
    tile3 = lambda bi, i: (bi, i, 0)
    est = (2 * (2 * t * d_model * 4 + t * ATTN_W * 4 + t * FOURIER_W * 4)
           + 2 * (d_model * d_model + 2 * d_model * d_ff) * 2 + t * (4 * d_model * 4 + d_ff * 6))
    return pl.pallas_call(
        _out_mlp_kernel,
        grid=(b, n_tok // t),
        in_specs=[
            pl.BlockSpec((1, t, d_model), tile3),
            pl.BlockSpec((1, t, ATTN_W), tile3),
            pl.BlockSpec((1, t, FOURIER_W), tile3),
            pl.BlockSpec((1, ATTN_W), const2),
            pl.BlockSpec((1, FOURIER_W), const2),
            pl.BlockSpec((d_model, d_model), const2),
            pl.BlockSpec((1, d_model), const2),
            pl.BlockSpec((d_model, d_ff), const2),
            pl.BlockSpec((d_ff, d_model), const2),
            pl.BlockSpec((1, d_model), const2),
        ],
        out_specs=pl.BlockSpec((1, t, d_model), tile3),
        out_shape=jax.ShapeDtypeStruct((b, n_tok, d_model), F32),
        compiler_params=pltpu.CompilerParams(
            dimension_semantics=("arbitrary", "arbitrary"), vmem_limit_bytes=_vmem_limit(est)),
    )(x, attn, four, g_attn_out, g_fourier_out, w_out, g_mlp, w_up, w_down, g_final)


def kernel(x, meta_tokens, g_mix, w_in, g_q, g_k, w_fourier, g_attn_out, g_fourier_out, w_out,
           g_mlp, w_up, w_down, g_final):
    assert g_mix.shape[0] == 1, "meta-token rows are only carried for a single layer"
    b, n_tok, d_model = x.shape
    l = n_tok + N_META
    t = TOKEN_TILE
    assert n_tok % t == 0 and n_tok % GRID_W == 0

    cos_tab, sin_tab = _rope_tables(n_tok, n_tok + t)
    meta_pad = jnp.pad(meta_tokens.astype(F32), ((0, t - N_META), (0, 0)))
    ab = _fourier_weights(w_fourier[0], l)
    q, k, v, p, qq = _in_proj(
        x, meta_pad, cos_tab, sin_tab, g_mix, w_in[0].astype(BF16), g_q, g_k, ab, l)
    attn = _attention(q, k, v, g_q, g_k, n_tok)
    four = _sequence_dft(p, qq, l)
    return _out_mlp(
        x, attn, four, g_attn_out, g_fourier_out, w_out[0].astype(BF16), g_mlp,
        w_up[0].astype(BF16), w_down[0].astype(BF16), g_final[None, :])
```

```python
import functools
import math

import jax
import jax.numpy as jnp
import numpy as np
from jax import lax
from jax.experimental import pallas as pl
from jax.experimental.pallas import tpu as pltpu

N_META = 16
GRID_W = 64
HEAD_DIM = 128
N_Q_HEADS = 4
N_KV_HEADS = 2
Q_GROUP = N_Q_HEADS // N_KV_HEADS
ATTN_W = N_Q_HEADS * HEAD_DIM
KV_W = N_KV_HEADS * HEAD_DIM
N_FOURIER_GROUPS = 4
FOURIER_GROUP_W = 128
FOURIER_W = N_FOURIER_GROUPS * FOURIER_GROUP_W
ROPE_THETA = 10000.0
ROPE_AXIS_DIM = HEAD_DIM // 2
RMS_EPS = 1e-6

V7X_VMEM_BYTES = 64 * 1024 * 1024
V7X_BF16_SUBLANES = 16

TOKEN_TILE = 512
IN_PROJ_ROWS = 256
ATTN_Q_TILE = 512
ATTN_KV_CHUNK = 2048
DFT_N2_TILE = 16
DFT_K1_TILE = 8
MAX_UNSHIFTED_SCORE = 64.0
BOUND_SLACK = 1.02

F32 = jnp.float32
BF16 = jnp.bfloat16


def _vmem_limit(estimate_bytes):
    return int(min(max(2 * estimate_bytes, 32 * 1024 * 1024), V7X_VMEM_BYTES - 8 * 1024 * 1024))


def _rms(x, g):
    return x * lax.rsqrt(jnp.mean(x * x, axis=-1, keepdims=True) + RMS_EPS) * g


def _dft_factors(l):
    best = None
    for n1 in range(V7X_BF16_SUBLANES, l, V7X_BF16_SUBLANES):
        if l % n1 == 0:
            n2 = l // n1
            if best is None or abs(n1 - n2) < abs(best[0] - best[1]):
                best = (n1, n2)
    assert best is not None
    return best


def _rope_tables(n_tok, n_rows):
    rows_count = n_tok // GRID_W
    real_row = np.repeat(np.arange(rows_count, dtype=np.float64), GRID_W)
    real_col = np.tile(np.arange(GRID_W, dtype=np.float64), rows_count)
    meta_row = np.full((N_META,), -1.0)
    meta_col = np.arange(N_META, dtype=np.float64)
    pad = np.zeros((n_rows - n_tok - N_META,))
    row = np.concatenate([real_row, meta_row, pad])
    col = np.concatenate([real_col, meta_col, pad])
    inv_freq = ROPE_THETA ** (-np.arange(0, ROPE_AXIS_DIM, 2, dtype=np.float64) / ROPE_AXIS_DIM)
    ang_r = row[:, None] * inv_freq[None, :]
    ang_c = col[:, None] * inv_freq[None, :]
    cos = np.concatenate([np.cos(ang_r)] * 2 + [np.cos(ang_c)] * 2, axis=-1)
    sin = np.concatenate([-np.sin(ang_r), np.sin(ang_r), -np.sin(ang_c), np.sin(ang_c)], axis=-1)
    return jnp.asarray(cos.astype(np.float32)), jnp.asarray(sin.astype(np.float32))


def _channel_dft_tables(l):
    n = np.arange(FOURIER_GROUP_W)
    ang = 2.0 * np.pi * ((n[:, None] * n[None, :]) % FOURIER_GROUP_W) / FOURIER_GROUP_W
    scale = 1.0 / math.sqrt(l * FOURIER_GROUP_W)
    return (np.cos(ang) * scale).astype(np.float32), (np.sin(ang) * scale).astype(np.float32)


def _dft_stage_tables(l, n1, n2, n2_pad):
    s = N_META
    k1 = np.arange(n1)[:, None]
    ang1 = 2.0 * np.pi * (((k1 + s) * n2 * np.arange(n1)[None, :]) % l) / l
    fr, fi = np.cos(ang1), np.sin(ang1)
    stage1 = np.concatenate(
        [np.concatenate([fr, -fi], axis=1), np.concatenate([fi, fr], axis=1)], axis=0).astype(np.float32)
    ang_t = 2.0 * np.pi * (((k1 + s) * (np.arange(n2)[None, :] + s)) % l) / l
    twiddle = np.zeros((2, n1, n2_pad), np.float32)
    twiddle[0, :, :n2] = np.cos(ang_t)
    twiddle[1, :, :n2] = np.sin(ang_t)
    k2 = np.arange(n2)[:, None]
    jj = np.arange(n2)[None, :]
    ang2 = 2.0 * np.pi * ((k2 * (jj + s)) % n2) / n2
    stage2 = np.zeros((n2_pad, n2_pad, 2), np.float32)
    stage2[:n2, :n2, 0] = np.cos(ang2)
    stage2[:n2, :n2, 1] = -np.sin(ang2)
    return stage1, twiddle, stage2.reshape(n2_pad, 2 * n2_pad)


def _fourier_weight_kernel(c_ref, s_ref, w_ref, ab_ref):
    w = w_ref[0]
    a = jnp.dot(c_ref[...], w, preferred_element_type=F32, precision=lax.Precision.HIGHEST)
    b = jnp.dot(s_ref[...], w, preferred_element_type=F32, precision=lax.Precision.HIGHEST)
    ab_ref[0, :, :FOURIER_GROUP_W] = a.astype(BF16)
    ab_ref[0, :, FOURIER_GROUP_W:] = b.astype(BF16)


def _fourier_weights(w_f, l):
    c, s = _channel_dft_tables(l)
    gw = FOURIER_GROUP_W
    return pl.pallas_call(
        _fourier_weight_kernel,
        grid=(N_FOURIER_GROUPS,),
        in_specs=[
            pl.BlockSpec((gw, gw), lambda g: (0, 0)),
            pl.BlockSpec((gw, gw), lambda g: (0, 0)),
            pl.BlockSpec((1, gw, gw), lambda g: (g, 0, 0)),
        ],
        out_specs=pl.BlockSpec((1, gw, 2 * gw), lambda g: (g, 0, 0)),
        out_shape=jax.ShapeDtypeStruct((N_FOURIER_GROUPS, gw, 2 * gw), BF16),
    )(jnp.asarray(c), jnp.asarray(s), w_f)


def _rope(x, cos, sin, lower_half):
    partner = jnp.where(lower_half, pltpu.roll(x, HEAD_DIM - 32, axis=1), pltpu.roll(x, 32, axis=1))
    return x * cos + partner * sin


def _in_proj_kernel(
    x_ref, meta_ref, cos_ref, sin_ref, gmix_ref, win_ref, gq_ref, gk_ref, ab_ref,
    q_ref, k_ref, v_ref, p_ref, qq_ref, *, n_real_tiles, q_scale,
):
    i = pl.program_id(1)
    is_meta = i == n_real_tiles
    d = HEAD_DIM
    gw = FOURIER_GROUP_W
    lane = lax.broadcasted_iota(jnp.int32, (IN_PROJ_ROWS, d), 1)
    lower_half = (lane % ROPE_AXIS_DIM) < (ROPE_AXIS_DIM // 2)
    for r in range(x_ref.shape[1] // IN_PROJ_ROWS):
        rows = pl.ds(r * IN_PROJ_ROWS, IN_PROJ_ROWS)
        xt = jnp.where(is_meta, meta_ref[rows, :], x_ref[0, rows, :])
        hn = _rms(xt, gmix_ref[...]).astype(BF16)
        proj = jnp.dot(hn, win_ref[...], preferred_element_type=F32)
        cos = cos_ref[rows, :]
        sin = sin_ref[rows, :]
        for h in range(N_Q_HEADS):
            qh = _rope(_rms(proj[:, h * d:(h + 1) * d], gq_ref[...]), cos, sin, lower_half)
            q_ref[0, rows, h * d:(h + 1) * d] = (qh * q_scale).astype(BF16)
        for h in range(N_KV_HEADS):
            off = ATTN_W + h * d
            kh = _rope(_rms(proj[:, off:off + d], gk_ref[...]), cos, sin, lower_half)
            k_ref[0, rows, h * d:(h + 1) * d] = kh.astype(BF16)
            off = ATTN_W + KV_W + h * d
            v_ref[0, rows, 2 * h * d:(2 * h + 1) * d] = proj[:, off:off + d].astype(BF16)
            v_ref[0, rows, (2 * h + 1) * d:(2 * h + 2) * d] = jnp.ones((IN_PROJ_ROWS, d), BF16)
        for g in range(N_FOURIER_GROUPS):
            off = ATTN_W + 2 * KV_W + g * gw
            pq = jnp.dot(proj[:, off:off + gw].astype(BF16), ab_ref[g], preferred_element_type=F32)
            p_ref[0, rows, g * gw:(g + 1) * gw] = pq[:, :gw].astype(BF16)
            qq_ref[0, rows, g * gw:(g + 1) * gw] = pq[:, gw:].astype(BF16)


def _in_proj(x, meta_pad, cos_tab, sin_tab, g_mix, w_in, g_q, g_k, ab, l):
    b, n_tok, d_model = x.shape
    t = TOKEN_TILE
    n_real_tiles = n_tok // t
    n_rows = (n_real_tiles + 1) * t
    in_w = w_in.shape[1]
    q_scale = math.log2(math.e) / math.sqrt(HEAD_DIM)
    const2 = lambda bi, i: (0, 0)
    est = (2 * t * d_model * 4 * 2 + d_model * in_w * 2 * 2 + t * in_w * 4 * 3
           + 2 * t * (2 * ATTN_W + KV_W + 2 * FOURIER_W) * 2)
    return pl.pallas_call(
        functools.partial(_in_proj_kernel, n_real_tiles=n_real_tiles, q_scale=q_scale),
        grid=(b, n_real_tiles + 1),
        in_specs=[
            pl.BlockSpec((1, t, d_model), lambda bi, i: (bi, jnp.minimum(i, n_real_tiles - 1), 0)),
            pl.BlockSpec((t, d_model), const2),
            pl.BlockSpec((t, HEAD_DIM), lambda bi, i: (i, 0)),
            pl.BlockSpec((t, HEAD_DIM), lambda bi, i: (i, 0)),
            pl.BlockSpec((1, d_model), const2),
            pl.BlockSpec((d_model, in_w), const2),
            pl.BlockSpec((1, HEAD_DIM), const2),
            pl.BlockSpec((1, HEAD_DIM), const2),
            pl.BlockSpec((N_FOURIER_GROUPS, FOURIER_GROUP_W, 2 * FOURIER_GROUP_W), lambda bi, i: (0, 0, 0)),
        ],
        out_specs=[
            pl.BlockSpec((1, t, ATTN_W), lambda bi, i: (bi, i, 0)),
            pl.BlockSpec((1, t, KV_W), lambda bi, i: (bi, i, 0)),
            pl.BlockSpec((1, t, 2 * KV_W), lambda bi, i: (bi, i, 0)),
            pl.BlockSpec((1, t, FOURIER_W), lambda bi, i: (bi, i, 0)),
            pl.BlockSpec((1, t, FOURIER_W), lambda bi, i: (bi, i, 0)),
        ],
        out_shape=[
            jax.ShapeDtypeStruct((b, n_rows, ATTN_W), BF16),
            jax.ShapeDtypeStruct((b, n_rows, KV_W), BF16),
            jax.ShapeDtypeStruct((b, n_rows, 2 * KV_W), BF16),
            jax.ShapeDtypeStruct((b, l, FOURIER_W), BF16),
            jax.ShapeDtypeStruct((b, l, FOURIER_W), BF16),
        ],
        compiler_params=pltpu.CompilerParams(
            dimension_semantics=("arbitrary", "arbitrary"), vmem_limit_bytes=_vmem_limit(est)),
    )(x, meta_pad, cos_tab, sin_tab, g_mix, w_in, g_q, g_k, ab)


_NT = (((1,), (1,)), ((), ()))


def _attention_bounded_kernel(q_ref, qn_ref, k_ref, v_ref, km_ref, vm_ref, o_ref, q2_ref, p0_ref, p1_ref, acc_ref,
                              *, n_chunks, chunk):
    d = HEAD_DIM
    tq = q_ref.shape[1]
    first_tile = pl.program_id(2) == 0
    q2_ref[0, :tq] = q_ref[0, :, :d]
    q2_ref[0, tq:] = q_ref[0, :, d:]
    q2_ref[1, :tq] = qn_ref[0, :, :d]
    q2_ref[1, tq:] = qn_ref[0, :, d:]

    def probs(tile, c, p_ref):
        start = pl.multiple_of(c * chunk, chunk)
        s = lax.dot_general(q2_ref[tile], k_ref[0, pl.ds(start, chunk), :], _NT, preferred_element_type=F32)
        p_ref[...] = jnp.exp2(s).astype(BF16)

    def accumulate(c, p_ref):
        start = pl.multiple_of(c * chunk, chunk)
        acc_ref[...] += jnp.dot(p_ref[...], v_ref[0, pl.ds(start, chunk), :], preferred_element_type=F32)

    s_meta = lax.dot_general(q2_ref[0], km_ref[0], _NT, preferred_element_type=F32)
    acc_ref[...] = jnp.dot(jnp.exp2(s_meta).astype(BF16), vm_ref[0], preferred_element_type=F32)

    @pl.when(first_tile)
    def _():
        probs(0, 0, p0_ref)

    def body(j, carry):
        c = 2 * j
        probs(0, c + 1, p1_ref)
        accumulate(c, p0_ref)
        wraps = c + 2 >= n_chunks
        probs(wraps.astype(jnp.int32), jnp.where(wraps, 0, c + 2), p0_ref)
        accumulate(c + 1, p1_ref)
        return carry

    lax.fori_loop(0, n_chunks // 2, body, 0)

    acc = acc_ref[...]
    out = acc[:, :d] / acc[:, d:]
    o_ref[0, :, :d] = out[:tq]
    o_ref[0, :, d:] = out[tq:]


def _attention_kernel(q_ref, k_ref, v_ref, km_ref, vm_ref, o_ref, q2_ref, s0_ref, s1_ref, m_ref, acc_ref,
                      *, n_chunks, chunk):
    d = HEAD_DIM
    tq = q_ref.shape[1]
    q2_ref[:tq] = q_ref[0, :, :d]
    q2_ref[tq:] = q_ref[0, :, d:]

    def scores(c, s_ref):
        start = pl.multiple_of(c * chunk, chunk)
        s_ref[...] = lax.dot_general(
            q2_ref[...], k_ref[0, pl.ds(start, chunk), :], _NT, preferred_element_type=F32)

    def update(c, s_ref):
        start = pl.multiple_of(c * chunk, chunk)
        s = s_ref[...]
        m_old = m_ref[...]
        m_new = jnp.maximum(m_old, jnp.max(s, axis=-1, keepdims=True))
        p = jnp.exp2(s - m_new).astype(BF16)
        pv = jnp.dot(p, v_ref[0, pl.ds(start, chunk), :], preferred_element_type=F32)
        acc_ref[...] = acc_ref[...] * jnp.exp2(m_old - m_new) + pv
        m_ref[...] = m_new

    s_meta = lax.dot_general(q2_ref[...], km_ref[0], _NT, preferred_element_type=F32)
    m_meta = jnp.max(s_meta, axis=-1, keepdims=True)
    m_ref[...] = m_meta
    acc_ref[...] = jnp.dot(jnp.exp2(s_meta - m_meta).astype(BF16), vm_ref[0], preferred_element_type=F32)

    scores(0, s0_ref)

    def body(j, carry):
        c = 2 * j
        scores(c + 1, s1_ref)
        update(c, s0_ref)
        scores(jnp.minimum(c + 2, n_chunks - 1), s0_ref)
        update(c + 1, s1_ref)
        return carry

    lax.fori_loop(0, n_chunks // 2, body, 0)

    acc = acc_ref[...]
    out = acc[:, :d] / acc[:, d:]
    o_ref[0, :, :d] = out[:tq]
    o_ref[0, :, d:] = out[tq:]


def _attention_call(q, k, v, n_tok, bounded):
    b = q.shape[0]
    tq = min(ATTN_Q_TILE, n_tok)
    chunk = min(ATTN_KV_CHUNK, n_tok)
    d = HEAD_DIM
    gq = Q_GROUP * d
    m = Q_GROUP * tq
    meta_blk = n_tok // N_META
    assert (n_tok // chunk) % 2 == 0
    est = 2 * (tq * gq * 2 + n_tok * d * 2 + n_tok * 2 * d * 2 + tq * gq * 4) + m * (
        chunk * 4 * 3 + chunk * 2 + 2 * d * 4 * 3)
    n_tiles = n_tok // tq
    q_spec = pl.BlockSpec((1, tq, gq), lambda bi, g, i: (bi, i, g))
    if bounded:
        body = _attention_bounded_kernel
        scratch = [pltpu.VMEM((2, m, d), BF16), pltpu.VMEM((m, chunk), BF16), pltpu.VMEM((m, chunk), BF16),
                   pltpu.VMEM((m, 2 * d), F32)]
        q_specs = [q_spec, pl.BlockSpec((1, tq, gq), lambda bi, g, i: (bi, jnp.minimum(i + 1, n_tiles - 1), g))]
        q_args = (q, q)
    else:
        body = _attention_kernel
        scratch = [pltpu.VMEM((m, d), BF16), pltpu.VMEM((m, chunk), F32), pltpu.VMEM((m, chunk), F32),
                   pltpu.VMEM((m, 1), F32), pltpu.VMEM((m, 2 * d), F32)]
        q_specs = [q_spec]
        q_args = (q,)
    return pl.pallas_call(
        functools.partial(body, n_chunks=n_tok // chunk, chunk=chunk),
        grid=(b, N_KV_HEADS, n_tiles),
        in_specs=q_specs + [
            pl.BlockSpec((1, n_tok, d), lambda bi, g, i: (bi, 0, g)),
            pl.BlockSpec((1, n_tok, 2 * d), lambda bi, g, i: (bi, 0, g)),
            pl.BlockSpec((1, N_META, d), lambda bi, g, i: (bi, meta_blk, g)),
            pl.BlockSpec((1, N_META, 2 * d), lambda bi, g, i: (bi, meta_blk, g)),
        ],
        out_specs=pl.BlockSpec((1, tq, gq), lambda bi, g, i: (bi, i, g)),
        out_shape=jax.ShapeDtypeStruct((b, n_tok, ATTN_W), F32),
        scratch_shapes=scratch,
        compiler_params=pltpu.CompilerParams(
            dimension_semantics=("arbitrary", "arbitrary", "arbitrary"), vmem_limit_bytes=_vmem_limit(est)),
    )(*q_args, k, v, k, v)


def _attention(q, k, v, g_q, g_k, n_tok):
    score_bound = (math.log2(math.e) / math.sqrt(HEAD_DIM)) * HEAD_DIM * BOUND_SLACK * (
        jnp.max(jnp.abs(g_q)) * jnp.max(jnp.abs(g_k)))
    return lax.cond(
        score_bound <= MAX_UNSHIFTED_SCORE,
        lambda: _attention_call(q, k, v, n_tok, True),
        lambda: _attention_call(q, k, v, n_tok, False))


def _dft1_kernel(f_ref, t_ref, p_ref, q_ref, g_ref, *, n2, cols):
    step = pl.program_id(0)
    n1 = t_ref.shape[2]
    n_sub = t_ref.shape[3]
    f = f_ref[...].astype(BF16)
    for j in range(n_sub):
        z = jnp.concatenate(
            [p_ref[0, :, j * cols:(j + 1) * cols], q_ref[0, :, j * cols:(j + 1) * cols]], axis=0)
        res = jnp.dot(f, z, preferred_element_type=F32)
        re, im = res[:n1], res[n1:]
        tr = t_ref[0, 0, :, j:j + 1]
        ti = t_ref[0, 1, :, j:j + 1]
        valid = step * n_sub + j < n2
        g_ref[0, j, :n1] = jnp.where(valid, tr * re - ti * im, 0.0).astype(BF16)
        g_ref[0, j, n1:] = jnp.where(valid, tr * im + ti * re, 0.0).astype(BF16)


def _dft2_kernel(f_ref, g_ref, y_ref, *, n2):
    f = f_ref[...].astype(BF16)
    for kk in range(g_ref.shape[1]):
        res = jnp.dot(f, g_ref[0, kk], preferred_element_type=F32)
        y_ref[0, kk] = res[:n2]


def _sequence_dft(p, q, l):
    b = p.shape[0]
    cols = p.shape[2]
    n1, n2 = _dft_factors(l)
    tn = DFT_N2_TILE
    n2_pad = -(-n2 // tn) * tn
    stage1, twiddle, stage2 = _dft_stage_tables(l, n1, n2, n2_pad)
    twiddle = twiddle.reshape(2, n1, n2_pad // tn, tn).transpose(2, 0, 1, 3)
    pv = p.reshape(b, n1, n2 * cols)
    qv = q.reshape(b, n1, n2 * cols)
    est1 = 2 * (4 * n1 * n1 * 4 + 2 * n1 * tn * cols * 2 + tn * 2 * n1 * cols * 4) + 8 * n1 * cols * 4
    g = pl.pallas_call(
        functools.partial(_dft1_kernel, n2=n2, cols=cols),
        grid=(n2_pad // tn, b),
        in_specs=[
            pl.BlockSpec((2 * n1, 2 * n1), lambda i, bi: (0, 0)),
            pl.BlockSpec((1, 2, n1, tn), lambda i, bi: (i, 0, 0, 0)),
            pl.BlockSpec((1, n1, tn * cols), lambda i, bi: (bi, 0, i)),
            pl.BlockSpec((1, n1, tn * cols), lambda i, bi: (bi, 0, i)),
        ],
        out_specs=pl.BlockSpec((1, tn, 2 * n1, cols), lambda i, bi: (bi, i, 0, 0)),
        out_shape=jax.ShapeDtypeStruct((b, n2_pad, 2 * n1, cols), BF16),
        compiler_params=pltpu.CompilerParams(
            dimension_semantics=("arbitrary", "arbitrary"), vmem_limit_bytes=_vmem_limit(est1)),
    )(jnp.asarray(stage1), jnp.asarray(twiddle), pv, qv)

    gv = g.reshape(b, 2 * n2_pad, n1, cols).transpose(0, 2, 1, 3)
    tk = DFT_K1_TILE
    assert n1 % tk == 0
    est2 = 2 * (2 * n2_pad * n2_pad * 4 + tk * 2 * n2_pad * cols * 2 + tk * n2 * cols * 4) + 3 * n2_pad * cols * 4
    y = pl.pallas_call(
        functools.partial(_dft2_kernel, n2=n2),
        grid=(b, n1 // tk),
        in_specs=[
            pl.BlockSpec((n2_pad, 2 * n2_pad), lambda bi, i: (0, 0)),
            pl.BlockSpec((1, tk, 2 * n2_pad, cols), lambda bi, i: (bi, i, 0, 0)),
        ],
        out_specs=pl.BlockSpec((1, tk, n2, cols), lambda bi, i: (bi, i, 0, 0)),
        out_shape=jax.ShapeDtypeStruct((b, n1, n2, cols), F32),
        compiler_params=pltpu.CompilerParams(
            dimension_semantics=("arbitrary", "arbitrary"), vmem_limit_bytes=_vmem_limit(est2)),
    )(jnp.asarray(stage2), gv)
    return y.transpose(0, 2, 1, 3).reshape(b, l, cols)


def _out_mlp_kernel(
    x_ref, a_ref, f_ref, ga_ref, gf_ref, wout_ref, gmlp_ref, wup_ref, wdown_ref, gfin_ref, o_ref,
):
    an = _rms(a_ref[0], ga_ref[...]).astype(BF16)
    fn = _rms(f_ref[0], gf_ref[...]).astype(BF16)
    h = x_ref[0]
    h = h + jnp.dot(an, wout_ref[:ATTN_W, :], preferred_element_type=F32)
    h = h + jnp.dot(fn, wout_ref[ATTN_W:, :], preferred_element_type=F32)
    m = _rms(h, gmlp_ref[...]).astype(BF16)
    act = jnp.maximum(jnp.dot(m, wup_ref[...], preferred_element_type=F32), 0.0)
    out = h + jnp.dot((act * act).astype(BF16), wdown_ref[...], preferred_element_type=F32)
    o_ref[0] = _rms(out, gfin_ref[...])


def _out_mlp(x, attn, four, g_attn_out, g_fourier_out, w_out, g_mlp, w_up, w_down, g_final):
    b, n_tok, d_model = x.shape
    d_ff = w_up.shape[1]
    t = TOKEN_TILE
    const2 = lambda bi, i: (0, 0)
    tile3 = lambda bi, i: (bi, i, 0)
    est = (2 * (2 * t * d_model * 4 + t * ATTN_W * 4 + t * FOURIER_W * 4)
           + 2 * (d_model * d_model + 2 * d_model * d_ff) * 2 + t * (4 * d_model * 4 + d_ff * 6))
    return pl.pallas_call(
        _out_mlp_kernel,
        grid=(b, n_tok // t),
        in_specs=[
            pl.BlockSpec((1, t, d_model), tile3),
            pl.BlockSpec((1, t, ATTN_W), tile3),
            pl.BlockSpec((1, t, FOURIER_W), tile3),
            pl.BlockSpec((1, ATTN_W), const2),
            pl.BlockSpec((1, FOURIER_W), const2),
            pl.BlockSpec((d_model, d_model), const2),
            pl.BlockSpec((1, d_model), const2),
            pl.BlockSpec((d_model, d_ff), const2),
            pl.BlockSpec((d_ff, d_model), const2),
            pl.BlockSpec((1, d_model), const2),
        ],
        out_specs=pl.BlockSpec((1, t, d_model), tile3),
        out_shape=jax.ShapeDtypeStruct((b, n_tok, d_model), F32),
        compiler_params=pltpu.CompilerParams(
            dimension_semantics=("arbitrary", "arbitrary"), vmem_limit_bytes=_vmem_limit(est)),
    )(x, attn, four, g_attn_out, g_fourier_out, w_out, g_mlp, w_up, w_down, g_final)


def kernel(x, meta_tokens, g_mix, w_in, g_q, g_k, w_fourier, g_attn_out, g_fourier_out, w_out,
           g_mlp, w_up, w_down, g_final):
    assert g_mix.shape[0] == 1, "meta-token rows are only carried for a single layer"
    b, n_tok, d_model = x.shape
    l = n_tok + N_META
    t = TOKEN_TILE
    assert n_tok % t == 0 and n_tok % GRID_W == 0

    cos_tab, sin_tab = _rope_tables(n_tok, n_tok + t)
    meta_pad = jnp.pad(meta_tokens.astype(F32), ((0, t - N_META), (0, 0)))
    ab = _fourier_weights(w_fourier[0], l)
    q, k, v, p, qq = _in_proj(
        x, meta_pad, cos_tab, sin_tab, g_mix, w_in[0].astype(BF16), g_q, g_k, ab, l)
    attn = _attention(q, k, v, g_q, g_k, n_tok)
    four = _sequence_dft(p, qq, l)
    return _out_mlp(
        x, attn, four, g_attn_out, g_fourier_out, w_out[0].astype(BF16), g_mlp,
        w_up[0].astype(BF16), w_down[0].astype(BF16), g_final[None, :])
```

```python
import functools
import math

import jax
import jax.numpy as jnp
import numpy as np
from jax import lax
from jax.experimental import pallas as pl
from jax.experimental.pallas import tpu as pltpu

N_META = 16
GRID_W = 64
HEAD_DIM = 128
N_Q_HEADS = 4
N_KV_HEADS = 2
Q_GROUP = N_Q_HEADS // N_KV_HEADS
ATTN_W = N_Q_HEADS * HEAD_DIM
KV_W = N_KV_HEADS * HEAD_DIM
N_FOURIER_GROUPS = 4
FOURIER_GROUP_W = 128
FOURIER_W = N_FOURIER_GROUPS * FOURIER_GROUP_W
ROPE_THETA = 10000.0
ROPE_AXIS_DIM = HEAD_DIM // 2
RMS_EPS = 1e-6

V7X_VMEM_BYTES = 64 * 1024 * 1024
V7X_BF16_SUBLANES = 16

TOKEN_TILE = 512
IN_PROJ_ROWS = 256
ATTN_Q_TILE = 1024
ATTN_KV_CHUNK = 1024
DFT_N2_TILE = 16
DFT_K1_TILE = 8
MAX_UNSHIFTED_SCORE = 64.0
BOUND_SLACK = 1.02

F32 = jnp.float32
BF16 = jnp.bfloat16


def _vmem_limit(estimate_bytes):
    return int(min(max(2 * estimate_bytes, 32 * 1024 * 1024), V7X_VMEM_BYTES - 8 * 1024 * 1024))


def _rms(x, g):
    return x * lax.rsqrt(jnp.mean(x * x, axis=-1, keepdims=True) + RMS_EPS) * g


def _dft_factors(l):
    best = None
    for n1 in range(V7X_BF16_SUBLANES, l, V7X_BF16_SUBLANES):
        if l % n1 == 0:
            n2 = l // n1
            if best is None or abs(n1 - n2) < abs(best[0] - best[1]):
                best = (n1, n2)
    assert best is not None
    return best


def _rope_tables(n_tok, n_rows):
    rows_count = n_tok // GRID_W
    real_row = np.repeat(np.arange(rows_count, dtype=np.float64), GRID_W)
    real_col = np.tile(np.arange(GRID_W, dtype=np.float64), rows_count)
    meta_row = np.full((N_META,), -1.0)
    meta_col = np.arange(N_META, dtype=np.float64)
    pad = np.zeros((n_rows - n_tok - N_META,))
    row = np.concatenate([real_row, meta_row, pad])
    col = np.concatenate([real_col, meta_col, pad])
    inv_freq = ROPE_THETA ** (-np.arange(0, ROPE_AXIS_DIM, 2, dtype=np.float64) / ROPE_AXIS_DIM)
    ang_r = row[:, None] * inv_freq[None, :]
    ang_c = col[:, None] * inv_freq[None, :]
    cos = np.concatenate([np.cos(ang_r)] * 2 + [np.cos(ang_c)] * 2, axis=-1)
    sin = np.concatenate([-np.sin(ang_r), np.sin(ang_r), -np.sin(ang_c), np.sin(ang_c)], axis=-1)
    return jnp.asarray(cos.astype(np.float32)), jnp.asarray(sin.astype(np.float32))


def _channel_dft_tables(l):
    n = np.arange(FOURIER_GROUP_W)
    ang = 2.0 * np.pi * ((n[:, None] * n[None, :]) % FOURIER_GROUP_W) / FOURIER_GROUP_W
    scale = 1.0 / math.sqrt(l * FOURIER_GROUP_W)
    return (np.cos(ang) * scale).astype(np.float32), (np.sin(ang) * scale).astype(np.float32)


def _dft_stage_tables(l, n1, n2, n2_pad):
    s = N_META
    k1 = np.arange(n1)[:, None]
    ang1 = 2.0 * np.pi * (((k1 + s) * n2 * np.arange(n1)[None, :]) % l) / l
    fr, fi = np.cos(ang1), np.sin(ang1)
    stage1 = np.concatenate(
        [np.concatenate([fr, -fi], axis=1), np.concatenate([fi, fr], axis=1)], axis=0).astype(np.float32)
    ang_t = 2.0 * np.pi * (((k1 + s) * (np.arange(n2)[None, :] + s)) % l) / l
    twiddle = np.zeros((2, n1, n2_pad), np.float32)
    twiddle[0, :, :n2] = np.cos(ang_t)
    twiddle[1, :, :n2] = np.sin(ang_t)
    k2 = np.arange(n2)[:, None]
    jj = np.arange(n2)[None, :]
    ang2 = 2.0 * np.pi * ((k2 * (jj + s)) % n2) / n2
    stage2 = np.zeros((n2_pad, n2_pad, 2), np.float32)
    stage2[:n2, :n2, 0] = np.cos(ang2)
    stage2[:n2, :n2, 1] = -np.sin(ang2)
    return stage1, twiddle, stage2.reshape(n2_pad, 2 * n2_pad)


def _fourier_weight_kernel(c_ref, s_ref, w_ref, ab_ref):
    w = w_ref[0]
    a = jnp.dot(c_ref[...], w, preferred_element_type=F32, precision=lax.Precision.HIGHEST)
    b = jnp.dot(s_ref[...], w, preferred_element_type=F32, precision=lax.Precision.HIGHEST)
    ab_ref[0, :, :FOURIER_GROUP_W] = a.astype(BF16)
    ab_ref[0, :, FOURIER_GROUP_W:] = b.astype(BF16)


def _fourier_weights(w_f, l):
    c, s = _channel_dft_tables(l)
    gw = FOURIER_GROUP_W
    return pl.pallas_call(
        _fourier_weight_kernel,
        grid=(N_FOURIER_GROUPS,),
        in_specs=[
            pl.BlockSpec((gw, gw), lambda g: (0, 0)),
            pl.BlockSpec((gw, gw), lambda g: (0, 0)),
            pl.BlockSpec((1, gw, gw), lambda g: (g, 0, 0)),
        ],
        out_specs=pl.BlockSpec((1, gw, 2 * gw), lambda g: (g, 0, 0)),
        out_shape=jax.ShapeDtypeStruct((N_FOURIER_GROUPS, gw, 2 * gw), BF16),
    )(jnp.asarray(c), jnp.asarray(s), w_f)


def _rope(x, cos, sin, lower_half):
    partner = jnp.where(lower_half, pltpu.roll(x, HEAD_DIM - 32, axis=1), pltpu.roll(x, 32, axis=1))
    return x * cos + partner * sin


def _in_proj_kernel(
    x_ref, meta_ref, cos_ref, sin_ref, gmix_ref, win_ref, gq_ref, gk_ref, ab_ref,
    q_ref, k_ref, v_ref, p_ref, qq_ref, *, n_real_tiles, q_scale,
):
    i = pl.program_id(1)
    is_meta = i == n_real_tiles
    d = HEAD_DIM
    gw = FOURIER_GROUP_W
    lane = lax.broadcasted_iota(jnp.int32, (IN_PROJ_ROWS, d), 1)
    lower_half = (lane % ROPE_AXIS_DIM) < (ROPE_AXIS_DIM // 2)
    for r in range(x_ref.shape[1] // IN_PROJ_ROWS):
        rows = pl.ds(r * IN_PROJ_ROWS, IN_PROJ_ROWS)
        xt = jnp.where(is_meta, meta_ref[rows, :], x_ref[0, rows, :])
        hn = _rms(xt, gmix_ref[...]).astype(BF16)
        proj = jnp.dot(hn, win_ref[...], preferred_element_type=F32)
        cos = cos_ref[rows, :]
        sin = sin_ref[rows, :]
        for h in range(N_Q_HEADS):
            qh = _rope(_rms(proj[:, h * d:(h + 1) * d], gq_ref[...]), cos, sin, lower_half)
            q_ref[0, rows, h * d:(h + 1) * d] = (qh * q_scale).astype(BF16)
        for h in range(N_KV_HEADS):
            off = ATTN_W + h * d
            kh = _rope(_rms(proj[:, off:off + d], gk_ref[...]), cos, sin, lower_half)
            k_ref[0, rows, h * d:(h + 1) * d] = kh.astype(BF16)
            off = ATTN_W + KV_W + h * d
            v_ref[0, rows, 2 * h * d:(2 * h + 1) * d] = proj[:, off:off + d].astype(BF16)
            v_ref[0, rows, (2 * h + 1) * d:(2 * h + 2) * d] = jnp.ones((IN_PROJ_ROWS, d), BF16)
        for g in range(N_FOURIER_GROUPS):
            off = ATTN_W + 2 * KV_W + g * gw
            pq = jnp.dot(proj[:, off:off + gw].astype(BF16), ab_ref[g], preferred_element_type=F32)
            p_ref[0, rows, g * gw:(g + 1) * gw] = pq[:, :gw].astype(BF16)
            qq_ref[0, rows, g * gw:(g + 1) * gw] = pq[:, gw:].astype(BF16)


def _in_proj(x, meta_pad, cos_tab, sin_tab, g_mix, w_in, g_q, g_k, ab, l):
    b, n_tok, d_model = x.shape
    t = TOKEN_TILE
    n_real_tiles = n_tok // t
    n_rows = (n_real_tiles + 1) * t
    in_w = w_in.shape[1]
    q_scale = math.log2(math.e) / math.sqrt(HEAD_DIM)
    const2 = lambda bi, i: (0, 0)
    est = (2 * t * d_model * 4 * 2 + d_model * in_w * 2 * 2 + t * in_w * 4 * 3
           + 2 * t * (2 * ATTN_W + KV_W + 2 * FOURIER_W) * 2)
    return pl.pallas_call(
        functools.partial(_in_proj_kernel, n_real_tiles=n_real_tiles, q_scale=q_scale),
        grid=(b, n_real_tiles + 1),
        in_specs=[
            pl.BlockSpec((1, t, d_model), lambda bi, i: (bi, jnp.minimum(i, n_real_tiles - 1), 0)),
            pl.BlockSpec((t, d_model), const2),
            pl.BlockSpec((t, HEAD_DIM), lambda bi, i: (i, 0)),
            pl.BlockSpec((t, HEAD_DIM), lambda bi, i: (i, 0)),
            pl.BlockSpec((1, d_model), const2),
            pl.BlockSpec((d_model, in_w), const2),
            pl.BlockSpec((1, HEAD_DIM), const2),
            pl.BlockSpec((1, HEAD_DIM), const2),
            pl.BlockSpec((N_FOURIER_GROUPS, FOURIER_GROUP_W, 2 * FOURIER_GROUP_W), lambda bi, i: (0, 0, 0)),
        ],
        out_specs=[
            pl.BlockSpec((1, t, ATTN_W), lambda bi, i: (bi, i, 0)),
            pl.BlockSpec((1, t, KV_W), lambda bi, i: (bi, i, 0)),
            pl.BlockSpec((1, t, 2 * KV_W), lambda bi, i: (bi, i, 0)),
            pl.BlockSpec((1, t, FOURIER_W), lambda bi, i: (bi, i, 0)),
            pl.BlockSpec((1, t, FOURIER_W), lambda bi, i: (bi, i, 0)),
        ],
        out_shape=[
            jax.ShapeDtypeStruct((b, n_rows, ATTN_W), BF16),
            jax.ShapeDtypeStruct((b, n_rows, KV_W), BF16),
            jax.ShapeDtypeStruct((b, n_rows, 2 * KV_W), BF16),
            jax.ShapeDtypeStruct((b, l, FOURIER_W), BF16),
            jax.ShapeDtypeStruct((b, l, FOURIER_W), BF16),
        ],
        compiler_params=pltpu.CompilerParams(
            dimension_semantics=("arbitrary", "arbitrary"), vmem_limit_bytes=_vmem_limit(est)),
    )(x, meta_pad, cos_tab, sin_tab, g_mix, w_in, g_q, g_k, ab)


_NT = (((1,), (1,)), ((), ()))


def _attention_bounded_kernel(q_ref, qn_ref, k_ref, v_ref, km_ref, vm_ref, o_ref, q2_ref, p0_ref, p1_ref, acc_ref,
                              *, n_chunks, chunk):
    d = HEAD_DIM
    tq = q_ref.shape[1]
    first_tile = pl.program_id(2) == 0
    q2_ref[0, :tq] = q_ref[0, :, :d]
    q2_ref[0, tq:] = q_ref[0, :, d:]
    q2_ref[1, :tq] = qn_ref[0, :, :d]
    q2_ref[1, tq:] = qn_ref[0, :, d:]

    def probs(tile, c, p_ref):
        start = pl.multiple_of(c * chunk, chunk)
        s = lax.dot_general(q2_ref[tile], k_ref[0, pl.ds(start, chunk), :], _NT, preferred_element_type=F32)
        p_ref[...] = jnp.exp2(s).astype(BF16)

    def accumulate(c, p_ref):
        start = pl.multiple_of(c * chunk, chunk)
        acc_ref[...] += jnp.dot(p_ref[...], v_ref[0, pl.ds(start, chunk), :], preferred_element_type=F32)

    s_meta = lax.dot_general(q2_ref[0], km_ref[0], _NT, preferred_element_type=F32)
    acc_ref[...] = jnp.dot(jnp.exp2(s_meta).astype(BF16), vm_ref[0], preferred_element_type=F32)

    @pl.when(first_tile)
    def _():
        probs(0, 0, p0_ref)

    def body(j, carry):
        c = 2 * j
        probs(0, c + 1, p1_ref)
        accumulate(c, p0_ref)
        wraps = c + 2 >= n_chunks
        probs(wraps.astype(jnp.int32), jnp.where(wraps, 0, c + 2), p0_ref)
        accumulate(c + 1, p1_ref)
        return carry

    lax.fori_loop(0, n_chunks // 2, body, 0)

    acc = acc_ref[...]
    out = acc[:, :d] / acc[:, d:]
    o_ref[0, :, :d] = out[:tq]
    o_ref[0, :, d:] = out[tq:]


def _attention_kernel(q_ref, k_ref, v_ref, km_ref, vm_ref, o_ref, q2_ref, s0_ref, s1_ref, m_ref, acc_ref,
                      *, n_chunks, chunk):
    d = HEAD_DIM
    tq = q_ref.shape[1]
    q2_ref[:tq] = q_ref[0, :, :d]
    q2_ref[tq:] = q_ref[0, :, d:]

    def scores(c, s_ref):
        start = pl.multiple_of(c * chunk, chunk)
        s_ref[...] = lax.dot_general(
            q2_ref[...], k_ref[0, pl.ds(start, chunk), :], _NT, preferred_element_type=F32)

    def update(c, s_ref):
        start = pl.multiple_of(c * chunk, chunk)
        s = s_ref[...]
        m_old = m_ref[...]
        m_new = jnp.maximum(m_old, jnp.max(s, axis=-1, keepdims=True))
        p = jnp.exp2(s - m_new).astype(BF16)
        pv = jnp.dot(p, v_ref[0, pl.ds(start, chunk), :], preferred_element_type=F32)
        acc_ref[...] = acc_ref[...] * jnp.exp2(m_old - m_new) + pv
        m_ref[...] = m_new

    s_meta = lax.dot_general(q2_ref[...], km_ref[0], _NT, preferred_element_type=F32)
    m_meta = jnp.max(s_meta, axis=-1, keepdims=True)
    m_ref[...] = m_meta
    acc_ref[...] = jnp.dot(jnp.exp2(s_meta - m_meta).astype(BF16), vm_ref[0], preferred_element_type=F32)

    scores(0, s0_ref)

    def body(j, carry):
        c = 2 * j
        scores(c + 1, s1_ref)
        update(c, s0_ref)
        scores(jnp.minimum(c + 2, n_chunks - 1), s0_ref)
        update(c + 1, s1_ref)
        return carry

    lax.fori_loop(0, n_chunks // 2, body, 0)

    acc = acc_ref[...]
    out = acc[:, :d] / acc[:, d:]
    o_ref[0, :, :d] = out[:tq]
    o_ref[0, :, d:] = out[tq:]


def _attention_call(q, k, v, n_tok, bounded):
    b = q.shape[0]
    tq = min(ATTN_Q_TILE, n_tok)
    chunk = min(ATTN_KV_CHUNK, n_tok)
    d = HEAD_DIM
    gq = Q_GROUP * d
    m = Q_GROUP * tq
    meta_blk = n_tok // N_META
    assert (n_tok // chunk) % 2 == 0
    est = 2 * (tq * gq * 2 + n_tok * d * 2 + n_tok * 2 * d * 2 + tq * gq * 4) + m * (
        chunk * 4 * 3 + chunk * 2 + 2 * d * 4 * 3)
    n_tiles = n_tok // tq
    q_spec = pl.BlockSpec((1, tq, gq), lambda bi, g, i: (bi, i, g))
    if bounded:
        body = _attention_bounded_kernel
        scratch = [pltpu.VMEM((2, m, d), BF16), pltpu.VMEM((m, chunk), BF16), pltpu.VMEM((m, chunk), BF16),
                   pltpu.VMEM((m, 2 * d), F32)]
        q_specs = [q_spec, pl.BlockSpec((1, tq, gq), lambda bi, g, i: (bi, jnp.minimum(i + 1, n_tiles - 1), g))]
        q_args = (q, q)
    else:
        body = _attention_kernel
        scratch = [pltpu.VMEM((m, d), BF16), pltpu.VMEM((m, chunk), F32), pltpu.VMEM((m, chunk), F32),
                   pltpu.VMEM((m, 1), F32), pltpu.VMEM((m, 2 * d), F32)]
        q_specs = [q_spec]
        q_args = (q,)
    return pl.pallas_call(
        functools.partial(body, n_chunks=n_tok // chunk, chunk=chunk),
        grid=(b, N_KV_HEADS, n_tiles),
        in_specs=q_specs + [
            pl.BlockSpec((1, n_tok, d), lambda bi, g, i: (bi, 0, g)),
            pl.BlockSpec((1, n_tok, 2 * d), lambda bi, g, i: (bi, 0, g)),
            pl.BlockSpec((1, N_META, d), lambda bi, g, i: (bi, meta_blk, g)),
            pl.BlockSpec((1, N_META, 2 * d), lambda bi, g, i: (bi, meta_blk, g)),
        ],
        out_specs=pl.BlockSpec((1, tq, gq), lambda bi, g, i: (bi, i, g)),
        out_shape=jax.ShapeDtypeStruct((b, n_tok, ATTN_W), F32),
        scratch_shapes=scratch,
        compiler_params=pltpu.CompilerParams(
            dimension_semantics=("arbitrary", "arbitrary", "arbitrary"), vmem_limit_bytes=_vmem_limit(est)),
    )(*q_args, k, v, k, v)


def _attention(q, k, v, g_q, g_k, n_tok):
    score_bound = (math.log2(math.e) / math.sqrt(HEAD_DIM)) * HEAD_DIM * BOUND_SLACK * (
        jnp.max(jnp.abs(g_q)) * jnp.max(jnp.abs(g_k)))
    return lax.cond(
        score_bound <= MAX_UNSHIFTED_SCORE,
        lambda: _attention_call(q, k, v, n_tok, True),
        lambda: _attention_call(q, k, v, n_tok, False))


def _dft1_kernel(f_ref, t_ref, p_ref, q_ref, g_ref, *, n2, cols):
    step = pl.program_id(0)
    n1 = t_ref.shape[2]
    n_sub = t_ref.shape[3]
    f_top = f_ref[:n1, :]
    f_bot = f_ref[n1:, :]
    for j in range(n_sub):
        z = jnp.concatenate(
            [p_ref[0, :, j * cols:(j + 1) * cols], q_ref[0, :, j * cols:(j + 1) * cols]], axis=0)
        valid = step * n_sub + j < n2
        z = jnp.where(valid, z, jnp.zeros_like(z))
        tr = t_ref[0, 0, :, j:j + 1]
        ti = t_ref[0, 1, :, j:j + 1]
        m = jnp.concatenate([tr * f_top - ti * f_bot, ti * f_top + tr * f_bot], axis=0).astype(BF16)
        g_ref[0, j] = jnp.dot(m, z, preferred_element_type=F32).astype(BF16)


def _dft2_kernel(f_ref, g_ref, y_ref, *, n2):
    f = f_ref[...].astype(BF16)
    for kk in range(g_ref.shape[1]):
        res = jnp.dot(f, g_ref[0, kk], preferred_element_type=F32)
        y_ref[0, kk] = res[:n2]


def _sequence_dft(p, q, l):
    b = p.shape[0]
    cols = p.shape[2]
    n1, n2 = _dft_factors(l)
    tn = DFT_N2_TILE
    n2_pad = -(-n2 // tn) * tn
    stage1, twiddle, stage2 = _dft_stage_tables(l, n1, n2, n2_pad)
    twiddle = twiddle.reshape(2, n1, n2_pad // tn, tn).transpose(2, 0, 1, 3)
    pv = p.reshape(b, n1, n2 * cols)
    qv = q.reshape(b, n1, n2 * cols)
    est1 = 2 * (4 * n1 * n1 * 4 + 2 * n1 * tn * cols * 2 + tn * 2 * n1 * cols * 4) + 8 * n1 * cols * 4
    g = pl.pallas_call(
        functools.partial(_dft1_kernel, n2=n2, cols=cols),
        grid=(n2_pad // tn, b),
        in_specs=[
            pl.BlockSpec((2 * n1, 2 * n1), lambda i, bi: (0, 0)),
            pl.BlockSpec((1, 2, n1, tn), lambda i, bi: (i, 0, 0, 0)),
            pl.BlockSpec((1, n1, tn * cols), lambda i, bi: (bi, 0, i)),
            pl.BlockSpec((1, n1, tn * cols), lambda i, bi: (bi, 0, i)),
        ],
        out_specs=pl.BlockSpec((1, tn, 2 * n1, cols), lambda i, bi: (bi, i, 0, 0)),
        out_shape=jax.ShapeDtypeStruct((b, n2_pad, 2 * n1, cols), BF16),
        compiler_params=pltpu.CompilerParams(
            dimension_semantics=("arbitrary", "arbitrary"), vmem_limit_bytes=_vmem_limit(est1)),
    )(jnp.asarray(stage1), jnp.asarray(twiddle), pv, qv)

    gv = g.reshape(b, 2 * n2_pad, n1, cols).transpose(0, 2, 1, 3)
    tk = DFT_K1_TILE
    assert n1 % tk == 0
    est2 = 2 * (2 * n2_pad * n2_pad * 4 + tk * 2 * n2_pad * cols * 2 + tk * n2 * cols * 4) + 3 * n2_pad * cols * 4
    y = pl.pallas_call(
        functools.partial(_dft2_kernel, n2=n2),
        grid=(b, n1 // tk),
        in_specs=[
            pl.BlockSpec((n2_pad, 2 * n2_pad), lambda bi, i: (0, 0)),
            pl.BlockSpec((1, tk, 2 * n2_pad, cols), lambda bi, i: (bi, i, 0, 0)),
        ],
        out_specs=pl.BlockSpec((1, tk, n2, cols), lambda bi, i: (bi, i, 0, 0)),
        out_shape=jax.ShapeDtypeStruct((b, n1, n2, cols), F32),
        compiler_params=pltpu.CompilerParams(
            dimension_semantics=("arbitrary", "arbitrary"), vmem_limit_bytes=_vmem_limit(est2)),
    )(jnp.asarray(stage2), gv)
    return y.transpose(0, 2, 1, 3).reshape(b, l, cols)


def _out_mlp_kernel(
    x_ref, a_ref, f_ref, ga_ref, gf_ref, wout_ref, gmlp_ref, wup_ref, wdown_ref, gfin_ref, o_ref,
):
    an = _rms(a_ref[0], ga_ref[...]).astype(BF16)
    fn = _rms(f_ref[0], gf_ref[...]).astype(BF16)
    h = x_ref[0]
    h = h + jnp.dot(an, wout_ref[:ATTN_W, :], preferred_element_type=F32)
    h = h + jnp.dot(fn, wout_ref[ATTN_W:, :], preferred_element_type=F32)
    m = _rms(h, gmlp_ref[...]).astype(BF16)
    act = jnp.maximum(jnp.dot(m, wup_ref[...], preferred_element_type=F32), 0.0)
    out = h + jnp.dot((act * act).astype(BF16), wdown_ref[...], preferred_element_type=F32)
    o_ref[0] = _rms(out, gfin_ref[...])


def _out_mlp(x, attn, four, g_attn_out, g_fourier_out, w_out, g_mlp, w_up, w_down, g_final):
    b, n_tok, d_model = x.shape
    d_ff = w_up.shape[1]
    t = TOKEN_TILE
    const2 = lambda bi, i: (0, 0)
    tile3 = lambda bi, i: (bi, i, 0)
    est = (2 * (2 * t * d_model * 4 + t * ATTN_W * 4 + t * FOURIER_W * 4)
           + 2 * (d_model * d_model + 2 * d_model * d_ff) * 2 + t * (4 * d_model * 4 + d_ff * 6))
    return pl.pallas_call(
        _out_mlp_kernel,
        grid=(b, n_tok // t),
        in_specs=[
            pl.BlockSpec((1, t, d_model), tile3),
            pl.BlockSpec((1, t, ATTN_W), tile3),
            pl.BlockSpec((1, t, FOURIER_W), tile3),
            pl.BlockSpec((1, ATTN_W), const2),
            pl.BlockSpec((1, FOURIER_W), const2),
            pl.BlockSpec((d_model, d_model), const2),
            pl.BlockSpec((1, d_model), const2),
            pl.BlockSpec((d_model, d_ff), const2),
            pl.BlockSpec((d_ff, d_model), const2),
            pl.BlockSpec((1, d_model), const2),
        ],
        out_specs=pl.BlockSpec((1, t, d_model), tile3),
        out_shape=jax.ShapeDtypeStruct((b, n_tok, d_model), F32),
        compiler_params=pltpu.CompilerParams(
            dimension_semantics=("arbitrary", "arbitrary"), vmem_limit_bytes=_vmem_limit(est)),
    )(x, attn, four, g_attn_out, g_fourier_out, w_out, g_mlp, w_up, w_down, g_final)


def kernel(x, meta_tokens, g_mix, w_in, g_q, g_k, w_fourier, g_attn_out, g_fourier_out, w_out,
           g_mlp, w_up, w_down, g_final):
    assert g_mix.shape[0] == 1, "meta-token rows are only carried for a single layer"
    b, n_tok, d_model = x.shape
    l = n_tok + N_META
    t = TOKEN_TILE
    assert n_tok % t == 0 and n_tok % GRID_W == 0

    cos_tab, sin_tab = _rope_tables(n_tok, n_tok + t)
    meta_pad = jnp.pad(meta_tokens.astype(F32), ((0, t - N_META), (0, 0)))
    ab = _fourier_weights(w_fourier[0], l)
    q, k, v, p, qq = _in_proj(
        x, meta_pad, cos_tab, sin_tab, g_mix, w_in[0].astype(BF16), g_q, g_k, ab, l)
    attn = _attention(q, k, v, g_q, g_k, n_tok)
    four = _sequence_dft(p, qq, l)
    return _out_mlp(
        x, attn, four, g_attn_out, g_fourier_out, w_out[0].astype(BF16), g_mlp,
        w_up[0].astype(BF16), w_down[0].astype(BF16), g_final[None, :])
```

```python
import functools
import math

import jax
import jax.numpy as jnp
import numpy as np
from jax import lax
from jax.experimental import pallas as pl
from jax.experimental.pallas import tpu as pltpu

N_META = 16
GRID_W = 64
HEAD_DIM = 128
N_Q_HEADS = 4
N_KV_HEADS = 2
Q_GROUP = N_Q_HEADS // N_KV_HEADS
ATTN_W = N_Q_HEADS * HEAD_DIM
KV_W = N_KV_HEADS * HEAD_DIM
N_FOURIER_GROUPS = 4
FOURIER_GROUP_W = 128
FOURIER_W = N_FOURIER_GROUPS * FOURIER_GROUP_W
ROPE_THETA = 10000.0
ROPE_AXIS_DIM = HEAD_DIM // 2
RMS_EPS = 1e-6

V7X_VMEM_BYTES = 64 * 1024 * 1024
V7X_BF16_SUBLANES = 16

TOKEN_TILE = 512
IN_PROJ_ROWS = 256
ATTN_Q_TILE = 1024
ATTN_KV_CHUNK = 1024
ATTN_CHUNKS_PER_TRIP = 4
DFT_N2_TILE = 16
DFT_K1_TILE = 8
MAX_UNSHIFTED_SCORE = 64.0
BOUND_SLACK = 1.02

F32 = jnp.float32
BF16 = jnp.bfloat16


def _vmem_limit(estimate_bytes):
    return int(min(max(2 * estimate_bytes, 32 * 1024 * 1024), V7X_VMEM_BYTES - 8 * 1024 * 1024))


def _rms(x, g):
    return x * lax.rsqrt(jnp.mean(x * x, axis=-1, keepdims=True) + RMS_EPS) * g


def _dft_factors(l):
    best = None
    for n1 in range(V7X_BF16_SUBLANES, l, V7X_BF16_SUBLANES):
        if l % n1 == 0:
            n2 = l // n1
            if best is None or abs(n1 - n2) < abs(best[0] - best[1]):
                best = (n1, n2)
    assert best is not None
    return best


def _rope_tables(n_tok, n_rows):
    rows_count = n_tok // GRID_W
    real_row = np.repeat(np.arange(rows_count, dtype=np.float64), GRID_W)
    real_col = np.tile(np.arange(GRID_W, dtype=np.float64), rows_count)
    meta_row = np.full((N_META,), -1.0)
    meta_col = np.arange(N_META, dtype=np.float64)
    pad = np.zeros((n_rows - n_tok - N_META,))
    row = np.concatenate([real_row, meta_row, pad])
    col = np.concatenate([real_col, meta_col, pad])
    inv_freq = ROPE_THETA ** (-np.arange(0, ROPE_AXIS_DIM, 2, dtype=np.float64) / ROPE_AXIS_DIM)
    ang_r = row[:, None] * inv_freq[None, :]
    ang_c = col[:, None] * inv_freq[None, :]
    cos = np.concatenate([np.cos(ang_r)] * 2 + [np.cos(ang_c)] * 2, axis=-1)
    sin = np.concatenate([-np.sin(ang_r), np.sin(ang_r), -np.sin(ang_c), np.sin(ang_c)], axis=-1)
    return jnp.asarray(cos.astype(np.float32)), jnp.asarray(sin.astype(np.float32))


def _channel_dft_tables(l):
    n = np.arange(FOURIER_GROUP_W)
    ang = 2.0 * np.pi * ((n[:, None] * n[None, :]) % FOURIER_GROUP_W) / FOURIER_GROUP_W
    scale = 1.0 / math.sqrt(l * FOURIER_GROUP_W)
    return (np.cos(ang) * scale).astype(np.float32), (np.sin(ang) * scale).astype(np.float32)


def _dft_stage_tables(l, n1, n2, n2_pad):
    s = N_META
    k1 = np.arange(n1)[:, None]
    ang1 = 2.0 * np.pi * (((k1 + s) * n2 * np.arange(n1)[None, :]) % l) / l
    fr, fi = np.cos(ang1), np.sin(ang1)
    stage1 = np.concatenate(
        [np.concatenate([fr, -fi], axis=1), np.concatenate([fi, fr], axis=1)], axis=0).astype(np.float32)
    ang_t = 2.0 * np.pi * (((k1 + s) * (np.arange(n2)[None, :] + s)) % l) / l
    twiddle = np.zeros((2, n1, n2_pad), np.float32)
    twiddle[0, :, :n2] = np.cos(ang_t)
    twiddle[1, :, :n2] = np.sin(ang_t)
    k2 = np.arange(n2)[:, None]
    jj = np.arange(n2)[None, :]
    ang2 = 2.0 * np.pi * ((k2 * (jj + s)) % n2) / n2
    stage2 = np.zeros((n2_pad, n2_pad, 2), np.float32)
    stage2[:n2, :n2, 0] = np.cos(ang2)
    stage2[:n2, :n2, 1] = -np.sin(ang2)
    return stage1, twiddle, stage2.reshape(n2_pad, 2 * n2_pad)


def _fourier_weight_kernel(c_ref, s_ref, w_ref, ab_ref):
    w = w_ref[0]
    a = jnp.dot(c_ref[...], w, preferred_element_type=F32, precision=lax.Precision.HIGHEST)
    b = jnp.dot(s_ref[...], w, preferred_element_type=F32, precision=lax.Precision.HIGHEST)
    ab_ref[0, :, :FOURIER_GROUP_W] = a.astype(BF16)
    ab_ref[0, :, FOURIER_GROUP_W:] = b.astype(BF16)


def _fourier_weights(w_f, l):
    c, s = _channel_dft_tables(l)
    gw = FOURIER_GROUP_W
    return pl.pallas_call(
        _fourier_weight_kernel,
        grid=(N_FOURIER_GROUPS,),
        in_specs=[
            pl.BlockSpec((gw, gw), lambda g: (0, 0)),
            pl.BlockSpec((gw, gw), lambda g: (0, 0)),
            pl.BlockSpec((1, gw, gw), lambda g: (g, 0, 0)),
        ],
        out_specs=pl.BlockSpec((1, gw, 2 * gw), lambda g: (g, 0, 0)),
        out_shape=jax.ShapeDtypeStruct((N_FOURIER_GROUPS, gw, 2 * gw), BF16),
    )(jnp.asarray(c), jnp.asarray(s), w_f)


def _rope(x, cos, sin, lower_half):
    partner = jnp.where(lower_half, pltpu.roll(x, HEAD_DIM - 32, axis=1), pltpu.roll(x, 32, axis=1))
    return x * cos + partner * sin


def _in_proj_kernel(
    x_ref, meta_ref, cos_ref, sin_ref, gmix_ref, win_ref, gq_ref, gk_ref, ab_ref,
    q_ref, k_ref, v_ref, p_ref, qq_ref, *, n_real_tiles, q_scale,
):
    i = pl.program_id(1)
    is_meta = i == n_real_tiles
    d = HEAD_DIM
    gw = FOURIER_GROUP_W
    lane = lax.broadcasted_iota(jnp.int32, (IN_PROJ_ROWS, d), 1)
    lower_half = (lane % ROPE_AXIS_DIM) < (ROPE_AXIS_DIM // 2)
    for r in range(x_ref.shape[1] // IN_PROJ_ROWS):
        rows = pl.ds(r * IN_PROJ_ROWS, IN_PROJ_ROWS)
        xt = jnp.where(is_meta, meta_ref[rows, :], x_ref[0, rows, :])
        hn = _rms(xt, gmix_ref[...]).astype(BF16)
        proj = jnp.dot(hn, win_ref[...], preferred_element_type=F32)
        cos = cos_ref[rows, :]
        sin = sin_ref[rows, :]
        for h in range(N_Q_HEADS):
            qh = _rope(_rms(proj[:, h * d:(h + 1) * d], gq_ref[...]), cos, sin, lower_half)
            q_ref[0, rows, h * d:(h + 1) * d] = (qh * q_scale).astype(BF16)
        for h in range(N_KV_HEADS):
            off = ATTN_W + h * d
            kh = _rope(_rms(proj[:, off:off + d], gk_ref[...]), cos, sin, lower_half)
            k_ref[0, rows, h * d:(h + 1) * d] = kh.astype(BF16)
            off = ATTN_W + KV_W + h * d
            v_ref[0, rows, 2 * h * d:(2 * h + 1) * d] = proj[:, off:off + d].astype(BF16)
            v_ref[0, rows, (2 * h + 1) * d:(2 * h + 2) * d] = jnp.ones((IN_PROJ_ROWS, d), BF16)
        for g in range(N_FOURIER_GROUPS):
            off = ATTN_W + 2 * KV_W + g * gw
            pq = jnp.dot(proj[:, off:off + gw].astype(BF16), ab_ref[g], preferred_element_type=F32)
            p_ref[0, rows, g * gw:(g + 1) * gw] = pq[:, :gw].astype(BF16)
            qq_ref[0, rows, g * gw:(g + 1) * gw] = pq[:, gw:].astype(BF16)


def _in_proj(x, meta_pad, cos_tab, sin_tab, g_mix, w_in, g_q, g_k, ab, l):
    b, n_tok, d_model = x.shape
    t = TOKEN_TILE
    n_real_tiles = n_tok // t
    n_rows = (n_real_tiles + 1) * t
    in_w = w_in.shape[1]
    q_scale = math.log2(math.e) / math.sqrt(HEAD_DIM)
    const2 = lambda bi, i: (0, 0)
    est = (2 * t * d_model * 4 * 2 + d_model * in_w * 2 * 2 + t * in_w * 4 * 3
           + 2 * t * (2 * ATTN_W + KV_W + 2 * FOURIER_W) * 2)
    return pl.pallas_call(
        functools.partial(_in_proj_kernel, n_real_tiles=n_real_tiles, q_scale=q_scale),
        grid=(b, n_real_tiles + 1),
        in_specs=[
            pl.BlockSpec((1, t, d_model), lambda bi, i: (bi, jnp.minimum(i, n_real_tiles - 1), 0)),
            pl.BlockSpec((t, d_model), const2),
            pl.BlockSpec((t, HEAD_DIM), lambda bi, i: (i, 0)),
            pl.BlockSpec((t, HEAD_DIM), lambda bi, i: (i, 0)),
            pl.BlockSpec((1, d_model), const2),
            pl.BlockSpec((d_model, in_w), const2),
            pl.BlockSpec((1, HEAD_DIM), const2),
            pl.BlockSpec((1, HEAD_DIM), const2),
            pl.BlockSpec((N_FOURIER_GROUPS, FOURIER_GROUP_W, 2 * FOURIER_GROUP_W), lambda bi, i: (0, 0, 0)),
        ],
        out_specs=[
            pl.BlockSpec((1, t, ATTN_W), lambda bi, i: (bi, i, 0)),
            pl.BlockSpec((1, t, KV_W), lambda bi, i: (bi, i, 0)),
            pl.BlockSpec((1, t, 2 * KV_W), lambda bi, i: (bi, i, 0)),
            pl.BlockSpec((1, t, FOURIER_W), lambda bi, i: (bi, i, 0)),
            pl.BlockSpec((1, t, FOURIER_W), lambda bi, i: (bi, i, 0)),
        ],
        out_shape=[
            jax.ShapeDtypeStruct((b, n_rows, ATTN_W), BF16),
            jax.ShapeDtypeStruct((b, n_rows, KV_W), BF16),
            jax.ShapeDtypeStruct((b, n_rows, 2 * KV_W), BF16),
            jax.ShapeDtypeStruct((b, l, FOURIER_W), BF16),
            jax.ShapeDtypeStruct((b, l, FOURIER_W), BF16),
        ],
        compiler_params=pltpu.CompilerParams(
            dimension_semantics=("arbitrary", "arbitrary"), vmem_limit_bytes=_vmem_limit(est)),
    )(x, meta_pad, cos_tab, sin_tab, g_mix, w_in, g_q, g_k, ab)


_NT = (((1,), (1,)), ((), ()))


def _attention_bounded_kernel(q_ref, qn_ref, k_ref, v_ref, km_ref, vm_ref, o_ref, q2_ref, p0_ref, p1_ref, acc_ref,
                              *, n_chunks, chunk):
    d = HEAD_DIM
    tq = q_ref.shape[1]
    first_tile = pl.program_id(2) == 0
    q2_ref[0, :tq] = q_ref[0, :, :d]
    q2_ref[0, tq:] = q_ref[0, :, d:]
    q2_ref[1, :tq] = qn_ref[0, :, :d]
    q2_ref[1, tq:] = qn_ref[0, :, d:]

    def probs(tile, c, p_ref):
        start = pl.multiple_of(c * chunk, chunk)
        s = lax.dot_general(q2_ref[tile], k_ref[0, pl.ds(start, chunk), :], _NT, preferred_element_type=F32)
        p_ref[...] = jnp.exp2(s).astype(BF16)

    def accumulate(c, p_ref):
        start = pl.multiple_of(c * chunk, chunk)
        acc_ref[...] += jnp.dot(p_ref[...], v_ref[0, pl.ds(start, chunk), :], preferred_element_type=F32)

    s_meta = lax.dot_general(q2_ref[0], km_ref[0], _NT, preferred_element_type=F32)
    acc_ref[...] = jnp.dot(jnp.exp2(s_meta).astype(BF16), vm_ref[0], preferred_element_type=F32)

    @pl.when(first_tile)
    def _():
        probs(0, 0, p0_ref)

    bufs = (p0_ref, p1_ref)
    per_trip = ATTN_CHUNKS_PER_TRIP if n_chunks % ATTN_CHUNKS_PER_TRIP == 0 else 2

    def body(j, carry):
        c = per_trip * j
        for u in range(per_trip):
            nxt = c + u + 1
            if u + 1 < per_trip:
                probs(0, nxt, bufs[(u + 1) % 2])
            else:
                wraps = nxt >= n_chunks
                probs(wraps.astype(jnp.int32), jnp.where(wraps, 0, nxt), bufs[(u + 1) % 2])
            accumulate(c + u, bufs[u % 2])
        return carry

    lax.fori_loop(0, n_chunks // per_trip, body, 0)

    acc = acc_ref[...]
    out = acc[:, :d] / acc[:, d:]
    o_ref[0, :, :d] = out[:tq]
    o_ref[0, :, d:] = out[tq:]


def _attention_kernel(q_ref, k_ref, v_ref, km_ref, vm_ref, o_ref, q2_ref, s0_ref, s1_ref, m_ref, acc_ref,
                      *, n_chunks, chunk):
    d = HEAD_DIM
    tq = q_ref.shape[1]
    q2_ref[:tq] = q_ref[0, :, :d]
    q2_ref[tq:] = q_ref[0, :, d:]

    def scores(c, s_ref):
        start = pl.multiple_of(c * chunk, chunk)
        s_ref[...] = lax.dot_general(
            q2_ref[...], k_ref[0, pl.ds(start, chunk), :], _NT, preferred_element_type=F32)

    def update(c, s_ref):
        start = pl.multiple_of(c * chunk, chunk)
        s = s_ref[...]
        m_old = m_ref[...]
        m_new = jnp.maximum(m_old, jnp.max(s, axis=-1, keepdims=True))
        p = jnp.exp2(s - m_new).astype(BF16)
        pv = jnp.dot(p, v_ref[0, pl.ds(start, chunk), :], preferred_element_type=F32)
        acc_ref[...] = acc_ref[...] * jnp.exp2(m_old - m_new) + pv
        m_ref[...] = m_new

    s_meta = lax.dot_general(q2_ref[...], km_ref[0], _NT, preferred_element_type=F32)
    m_meta = jnp.max(s_meta, axis=-1, keepdims=True)
    m_ref[...] = m_meta
    acc_ref[...] = jnp.dot(jnp.exp2(s_meta - m_meta).astype(BF16), vm_ref[0], preferred_element_type=F32)

    scores(0, s0_ref)

    def body(j, carry):
        c = 2 * j
        scores(c + 1, s1_ref)
        update(c, s0_ref)
        scores(jnp.minimum(c + 2, n_chunks - 1), s0_ref)
        update(c + 1, s1_ref)
        return carry

    lax.fori_loop(0, n_chunks // 2, body, 0)

    acc = acc_ref[...]
    out = acc[:, :d] / acc[:, d:]
    o_ref[0, :, :d] = out[:tq]
    o_ref[0, :, d:] = out[tq:]


def _attention_call(q, k, v, n_tok, bounded):
    b = q.shape[0]
    tq = min(ATTN_Q_TILE, n_tok)
    chunk = min(ATTN_KV_CHUNK, n_tok)
    d = HEAD_DIM
    gq = Q_GROUP * d
    m = Q_GROUP * tq
    meta_blk = n_tok // N_META
    assert (n_tok // chunk) % 2 == 0
    est = 2 * (tq * gq * 2 + n_tok * d * 2 + n_tok * 2 * d * 2 + tq * gq * 4) + m * (
        chunk * 4 * 3 + chunk * 2 + 2 * d * 4 * 3)
    n_tiles = n_tok // tq
    q_spec = pl.BlockSpec((1, tq, gq), lambda bi, g, i: (bi, i, g))
    if bounded:
        body = _attention_bounded_kernel
        scratch = [pltpu.VMEM((2, m, d), BF16), pltpu.VMEM((m, chunk), BF16), pltpu.VMEM((m, chunk), BF16),
                   pltpu.VMEM((m, 2 * d), F32)]
        q_specs = [q_spec, pl.BlockSpec((1, tq, gq), lambda bi, g, i: (bi, jnp.minimum(i + 1, n_tiles - 1), g))]
        q_args = (q, q)
    else:
        body = _attention_kernel
        scratch = [pltpu.VMEM((m, d), BF16), pltpu.VMEM((m, chunk), F32), pltpu.VMEM((m, chunk), F32),
                   pltpu.VMEM((m, 1), F32), pltpu.VMEM((m, 2 * d), F32)]
        q_specs = [q_spec]
        q_args = (q,)
    return pl.pallas_call(
        functools.partial(body, n_chunks=n_tok // chunk, chunk=chunk),
        grid=(b, N_KV_HEADS, n_tiles),
        in_specs=q_specs + [
            pl.BlockSpec((1, n_tok, d), lambda bi, g, i: (bi, 0, g)),
            pl.BlockSpec((1, n_tok, 2 * d), lambda bi, g, i: (bi, 0, g)),
            pl.BlockSpec((1, N_META, d), lambda bi, g, i: (bi, meta_blk, g)),
            pl.BlockSpec((1, N_META, 2 * d), lambda bi, g, i: (bi, meta_blk, g)),
        ],
        out_specs=pl.BlockSpec((1, tq, gq), lambda bi, g, i: (bi, i, g)),
        out_shape=jax.ShapeDtypeStruct((b, n_tok, ATTN_W), F32),
        scratch_shapes=scratch,
        compiler_params=pltpu.CompilerParams(
            dimension_semantics=("arbitrary", "arbitrary", "arbitrary"), vmem_limit_bytes=_vmem_limit(est)),
    )(*q_args, k, v, k, v)


def _attention(q, k, v, g_q, g_k, n_tok):
    score_bound = (math.log2(math.e) / math.sqrt(HEAD_DIM)) * HEAD_DIM * BOUND_SLACK * (
        jnp.max(jnp.abs(g_q)) * jnp.max(jnp.abs(g_k)))
    return lax.cond(
        score_bound <= MAX_UNSHIFTED_SCORE,
        lambda: _attention_call(q, k, v, n_tok, True),
        lambda: _attention_call(q, k, v, n_tok, False))


def _dft1_kernel(f_ref, t_ref, p_ref, q_ref, g_ref, *, n2, cols):
    step = pl.program_id(0)
    n1 = t_ref.shape[2]
    n_sub = t_ref.shape[3]
    f_top = f_ref[:n1, :]
    f_bot = f_ref[n1:, :]
    for j in range(n_sub):
        z = jnp.concatenate(
            [p_ref[0, :, j * cols:(j + 1) * cols], q_ref[0, :, j * cols:(j + 1) * cols]], axis=0)
        valid = step * n_sub + j < n2
        z = jnp.where(valid, z, jnp.zeros_like(z))
        tr = t_ref[0, 0, :, j:j + 1]
        ti = t_ref[0, 1, :, j:j + 1]
        m = jnp.concatenate([tr * f_top - ti * f_bot, ti * f_top + tr * f_bot], axis=0).astype(BF16)
        g_ref[0, j] = jnp.dot(m, z, preferred_element_type=F32).astype(BF16)


def _dft2_kernel(f_ref, g_ref, y_ref, *, n2):
    f = f_ref[...].astype(BF16)
    for kk in range(g_ref.shape[1]):
        res = jnp.dot(f, g_ref[0, kk], preferred_element_type=F32)
        y_ref[0, kk] = res[:n2]


def _sequence_dft(p, q, l):
    b = p.shape[0]
    cols = p.shape[2]
    n1, n2 = _dft_factors(l)
    tn = DFT_N2_TILE
    n2_pad = -(-n2 // tn) * tn
    stage1, twiddle, stage2 = _dft_stage_tables(l, n1, n2, n2_pad)
    twiddle = twiddle.reshape(2, n1, n2_pad // tn, tn).transpose(2, 0, 1, 3)
    pv = p.reshape(b, n1, n2 * cols)
    qv = q.reshape(b, n1, n2 * cols)
    est1 = 2 * (4 * n1 * n1 * 4 + 2 * n1 * tn * cols * 2 + tn * 2 * n1 * cols * 4) + 8 * n1 * cols * 4
    g = pl.pallas_call(
        functools.partial(_dft1_kernel, n2=n2, cols=cols),
        grid=(n2_pad // tn, b),
        in_specs=[
            pl.BlockSpec((2 * n1, 2 * n1), lambda i, bi: (0, 0)),
            pl.BlockSpec((1, 2, n1, tn), lambda i, bi: (i, 0, 0, 0)),
            pl.BlockSpec((1, n1, tn * cols), lambda i, bi: (bi, 0, i)),
            pl.BlockSpec((1, n1, tn * cols), lambda i, bi: (bi, 0, i)),
        ],
        out_specs=pl.BlockSpec((1, tn, 2 * n1, cols), lambda i, bi: (bi, i, 0, 0)),
        out_shape=jax.ShapeDtypeStruct((b, n2_pad, 2 * n1, cols), BF16),
        compiler_params=pltpu.CompilerParams(
            dimension_semantics=("arbitrary", "arbitrary"), vmem_limit_bytes=_vmem_limit(est1)),
    )(jnp.asarray(stage1), jnp.asarray(twiddle), pv, qv)

    gv = g.reshape(b, 2 * n2_pad, n1, cols).transpose(0, 2, 1, 3)
    tk = DFT_K1_TILE
    assert n1 % tk == 0
    est2 = 2 * (2 * n2_pad * n2_pad * 4 + tk * 2 * n2_pad * cols * 2 + tk * n2 * cols * 4) + 3 * n2_pad * cols * 4
    y = pl.pallas_call(
        functools.partial(_dft2_kernel, n2=n2),
        grid=(b, n1 // tk),
        in_specs=[
            pl.BlockSpec((n2_pad, 2 * n2_pad), lambda bi, i: (0, 0)),
            pl.BlockSpec((1, tk, 2 * n2_pad, cols), lambda bi, i: (bi, i, 0, 0)),
        ],
        out_specs=pl.BlockSpec((1, tk, n2, cols), lambda bi, i: (bi, i, 0, 0)),
        out_shape=jax.ShapeDtypeStruct((b, n1, n2, cols), F32),
        compiler_params=pltpu.CompilerParams(
            dimension_semantics=("arbitrary", "arbitrary"), vmem_limit_bytes=_vmem_limit(est2)),
    )(jnp.asarray(stage2), gv)
    return y.transpose(0, 2, 1, 3).reshape(b, l, cols)


def _out_mlp_kernel(
    x_ref, a_ref, f_ref, ga_ref, gf_ref, wout_ref, gmlp_ref, wup_ref, wdown_ref, gfin_ref, o_ref,
):
    an = _rms(a_ref[0], ga_ref[...]).astype(BF16)
    fn = _rms(f_ref[0], gf_ref[...]).astype(BF16)
    h = x_ref[0]
    h = h + jnp.dot(an, wout_ref[:ATTN_W, :], preferred_element_type=F32)
    h = h + jnp.dot(fn, wout_ref[ATTN_W:, :], preferred_element_type=F32)
    m = _rms(h, gmlp_ref[...]).astype(BF16)
    act = jnp.maximum(jnp.dot(m, wup_ref[...], preferred_element_type=F32), 0.0)
    out = h + jnp.dot((act * act).astype(BF16), wdown_ref[...], preferred_element_type=F32)
    o_ref[0] = _rms(out, gfin_ref[...])


def _out_mlp(x, attn, four, g_attn_out, g_fourier_out, w_out, g_mlp, w_up, w_down, g_final):
    b, n_tok, d_model = x.shape
    d_ff = w_up.shape[1]
    t = TOKEN_TILE
    const2 = lambda bi, i: (0, 0)
    tile3 = lambda bi, i: (bi, i, 0)
    est = (2 * (2 * t * d_model * 4 + t * ATTN_W * 4 + t * FOURIER_W * 4)
           + 2 * (d_model * d_model + 2 * d_model * d_ff) * 2 + t * (4 * d_model * 4 + d_ff * 6))
    return pl.pallas_call(
        _out_mlp_kernel,
        grid=(b, n_tok // t),
        in_specs=[
            pl.BlockSpec((1, t, d_model), tile3),
            pl.BlockSpec((1, t, ATTN_W), tile3),
            pl.BlockSpec((1, t, FOURIER_W), tile3),
            pl.BlockSpec((1, ATTN_W), const2),
            pl.BlockSpec((1, FOURIER_W), const2),
            pl.BlockSpec((d_model, d_model), const2),
            pl.BlockSpec((1, d_model), const2),
            pl.BlockSpec((d_model, d_ff), const2),
            pl.BlockSpec((d_ff, d_model), const2),
            pl.BlockSpec((1, d_model), const2),
        ],
        out_specs=pl.BlockSpec((1, t, d_model), tile3),
        out_shape=jax.ShapeDtypeStruct((b, n_tok, d_model), F32),
        compiler_params=pltpu.CompilerParams(
            dimension_semantics=("arbitrary", "arbitrary"), vmem_limit_bytes=_vmem_limit(est)),
    )(x, attn, four, g_attn_out, g_fourier_out, w_out, g_mlp, w_up, w_down, g_final)


def kernel(x, meta_tokens, g_mix, w_in, g_q, g_k, w_fourier, g_attn_out, g_fourier_out, w_out,
           g_mlp, w_up, w_down, g_final):
    assert g_mix.shape[0] == 1, "meta-token rows are only carried for a single layer"
    b, n_tok, d_model = x.shape
    l = n_tok + N_META
    t = TOKEN_TILE
    assert n_tok % t == 0 and n_tok % GRID_W == 0

    cos_tab, sin_tab = _rope_tables(n_tok, n_tok + t)
    meta_pad = jnp.pad(meta_tokens.astype(F32), ((0, t - N_META), (0, 0)))
    ab = _fourier_weights(w_fourier[0], l)
    q, k, v, p, qq = _in_proj(
        x, meta_pad, cos_tab, sin_tab, g_mix, w_in[0].astype(BF16), g_q, g_k, ab, l)
    attn = _attention(q, k, v, g_q, g_k, n_tok)
    four = _sequence_dft(p, qq, l)
    return _out_mlp(
        x, attn, four, g_attn_out, g_fourier_out, w_out[0].astype(BF16), g_mlp,
        w_up[0].astype(BF16), w_down[0].astype(BF16), g_final[None, :])
```

```python
import functools
import math

import jax
import jax.numpy as jnp
import numpy as np
from jax import lax
from jax.experimental import pallas as pl
from jax.experimental.pallas import tpu as pltpu

N_META = 16
GRID_W = 64
HEAD_DIM = 128
N_Q_HEADS = 4
N_KV_HEADS = 2
Q_GROUP = N_Q_HEADS // N_KV_HEADS
ATTN_W = N_Q_HEADS * HEAD_DIM
KV_W = N_KV_HEADS * HEAD_DIM
N_FOURIER_GROUPS = 4
FOURIER_GROUP_W = 128
FOURIER_W = N_FOURIER_GROUPS * FOURIER_GROUP_W
ROPE_THETA = 10000.0
ROPE_AXIS_DIM = HEAD_DIM // 2
RMS_EPS = 1e-6

V7X_VMEM_BYTES = 64 * 1024 * 1024
V7X_BF16_SUBLANES = 16

TOKEN_TILE = 512
IN_PROJ_ROWS = 256
ATTN_Q_TILE = 1024
ATTN_KV_CHUNK = 1024
ATTN_CHUNKS_PER_TRIP = 8
DFT_N2_TILE = 16
DFT_K1_TILE = 8
MAX_UNSHIFTED_SCORE = 64.0
BOUND_SLACK = 1.02

F32 = jnp.float32
BF16 = jnp.bfloat16


def _vmem_limit(estimate_bytes):
    return int(min(max(2 * estimate_bytes, 32 * 1024 * 1024), V7X_VMEM_BYTES - 8 * 1024 * 1024))


def _rms(x, g):
    return x * lax.rsqrt(jnp.mean(x * x, axis=-1, keepdims=True) + RMS_EPS) * g


def _dft_factors(l):
    best = None
    for n1 in range(V7X_BF16_SUBLANES, l, V7X_BF16_SUBLANES):
        if l % n1 == 0:
            n2 = l // n1
            if best is None or abs(n1 - n2) < abs(best[0] - best[1]):
                best = (n1, n2)
    assert best is not None
    return best


def _rope_tables(n_tok, n_rows):
    rows_count = n_tok // GRID_W
    real_row = np.repeat(np.arange(rows_count, dtype=np.float64), GRID_W)
    real_col = np.tile(np.arange(GRID_W, dtype=np.float64), rows_count)
    meta_row = np.full((N_META,), -1.0)
    meta_col = np.arange(N_META, dtype=np.float64)
    pad = np.zeros((n_rows - n_tok - N_META,))
    row = np.concatenate([real_row, meta_row, pad])
    col = np.concatenate([real_col, meta_col, pad])
    inv_freq = ROPE_THETA ** (-np.arange(0, ROPE_AXIS_DIM, 2, dtype=np.float64) / ROPE_AXIS_DIM)
    ang_r = row[:, None] * inv_freq[None, :]
    ang_c = col[:, None] * inv_freq[None, :]
    cos = np.concatenate([np.cos(ang_r)] * 2 + [np.cos(ang_c)] * 2, axis=-1)
    sin = np.concatenate([-np.sin(ang_r), np.sin(ang_r), -np.sin(ang_c), np.sin(ang_c)], axis=-1)
    return jnp.asarray(cos.astype(np.float32)), jnp.asarray(sin.astype(np.float32))


def _channel_dft_tables(l):
    n = np.arange(FOURIER_GROUP_W)
    ang = 2.0 * np.pi * ((n[:, None] * n[None, :]) % FOURIER_GROUP_W) / FOURIER_GROUP_W
    scale = 1.0 / math.sqrt(l * FOURIER_GROUP_W)
    return (np.cos(ang) * scale).astype(np.float32), (np.sin(ang) * scale).astype(np.float32)


def _dft_stage_tables(l, n1, n2, n2_pad):
    s = N_META
    k1 = np.arange(n1)[:, None]
    ang1 = 2.0 * np.pi * (((k1 + s) * n2 * np.arange(n1)[None, :]) % l) / l
    fr, fi = np.cos(ang1), np.sin(ang1)
    stage1 = np.concatenate(
        [np.concatenate([fr, -fi], axis=1), np.concatenate([fi, fr], axis=1)], axis=0).astype(np.float32)
    ang_t = 2.0 * np.pi * (((k1 + s) * (np.arange(n2)[None, :] + s)) % l) / l
    twiddle = np.zeros((2, n1, n2_pad), np.float32)
    twiddle[0, :, :n2] = np.cos(ang_t)
    twiddle[1, :, :n2] = np.sin(ang_t)
    k2 = np.arange(n2)[:, None]
    jj = np.arange(n2)[None, :]
    ang2 = 2.0 * np.pi * ((k2 * (jj + s)) % n2) / n2
    stage2 = np.zeros((n2_pad, n2_pad, 2), np.float32)
    stage2[:n2, :n2, 0] = np.cos(ang2)
    stage2[:n2, :n2, 1] = -np.sin(ang2)
    return stage1, twiddle, stage2.reshape(n2_pad, 2 * n2_pad)


def _fourier_weight_kernel(c_ref, s_ref, w_ref, ab_ref):
    w = w_ref[0]
    a = jnp.dot(c_ref[...], w, preferred_element_type=F32, precision=lax.Precision.HIGHEST)
    b = jnp.dot(s_ref[...], w, preferred_element_type=F32, precision=lax.Precision.HIGHEST)
    ab_ref[0, :, :FOURIER_GROUP_W] = a.astype(BF16)
    ab_ref[0, :, FOURIER_GROUP_W:] = b.astype(BF16)


def _fourier_weights(w_f, l):
    c, s = _channel_dft_tables(l)
    gw = FOURIER_GROUP_W
    return pl.pallas_call(
        _fourier_weight_kernel,
        grid=(N_FOURIER_GROUPS,),
        in_specs=[
            pl.BlockSpec((gw, gw), lambda g: (0, 0)),
            pl.BlockSpec((gw, gw), lambda g: (0, 0)),
            pl.BlockSpec((1, gw, gw), lambda g: (g, 0, 0)),
        ],
        out_specs=pl.BlockSpec((1, gw, 2 * gw), lambda g: (g, 0, 0)),
        out_shape=jax.ShapeDtypeStruct((N_FOURIER_GROUPS, gw, 2 * gw), BF16),
    )(jnp.asarray(c), jnp.asarray(s), w_f)


def _rope(x, cos, sin, lower_half):
    partner = jnp.where(lower_half, pltpu.roll(x, HEAD_DIM - 32, axis=1), pltpu.roll(x, 32, axis=1))
    return x * cos + partner * sin


def _in_proj_kernel(
    x_ref, meta_ref, cos_ref, sin_ref, gmix_ref, win_ref, gq_ref, gk_ref, ab_ref,
    q_ref, k_ref, v_ref, p_ref, qq_ref, *, n_real_tiles, q_scale,
):
    i = pl.program_id(1)
    is_meta = i == n_real_tiles
    d = HEAD_DIM
    gw = FOURIER_GROUP_W
    lane = lax.broadcasted_iota(jnp.int32, (IN_PROJ_ROWS, d), 1)
    lower_half = (lane % ROPE_AXIS_DIM) < (ROPE_AXIS_DIM // 2)
    for r in range(x_ref.shape[1] // IN_PROJ_ROWS):
        rows = pl.ds(r * IN_PROJ_ROWS, IN_PROJ_ROWS)
        xt = jnp.where(is_meta, meta_ref[rows, :], x_ref[0, rows, :])
        hn = _rms(xt, gmix_ref[...]).astype(BF16)
        proj = jnp.dot(hn, win_ref[...], preferred_element_type=F32)
        cos = cos_ref[rows, :]
        sin = sin_ref[rows, :]
        for h in range(N_Q_HEADS):
            qh = _rope(_rms(proj[:, h * d:(h + 1) * d], gq_ref[...]), cos, sin, lower_half)
            q_ref[0, rows, h * d:(h + 1) * d] = (qh * q_scale).astype(BF16)
        for h in range(N_KV_HEADS):
            off = ATTN_W + h * d
            kh = _rope(_rms(proj[:, off:off + d], gk_ref[...]), cos, sin, lower_half)
            k_ref[0, rows, h * d:(h + 1) * d] = kh.astype(BF16)
            off = ATTN_W + KV_W + h * d
            v_ref[0, rows, 2 * h * d:(2 * h + 1) * d] = proj[:, off:off + d].astype(BF16)
            v_ref[0, rows, (2 * h + 1) * d:(2 * h + 2) * d] = jnp.ones((IN_PROJ_ROWS, d), BF16)
        for g in range(N_FOURIER_GROUPS):
            off = ATTN_W + 2 * KV_W + g * gw
            pq = jnp.dot(proj[:, off:off + gw].astype(BF16), ab_ref[g], preferred_element_type=F32)
            p_ref[0, rows, g * gw:(g + 1) * gw] = pq[:, :gw].astype(BF16)
            qq_ref[0, rows, g * gw:(g + 1) * gw] = pq[:, gw:].astype(BF16)


def _in_proj(x, meta_pad, cos_tab, sin_tab, g_mix, w_in, g_q, g_k, ab, l):
    b, n_tok, d_model = x.shape
    t = TOKEN_TILE
    n_real_tiles = n_tok // t
    n_rows = (n_real_tiles + 1) * t
    in_w = w_in.shape[1]
    q_scale = math.log2(math.e) / math.sqrt(HEAD_DIM)
    const2 = lambda bi, i: (0, 0)
    est = (2 * t * d_model * 4 * 2 + d_model * in_w * 2 * 2 + t * in_w * 4 * 3
           + 2 * t * (2 * ATTN_W + KV_W + 2 * FOURIER_W) * 2)
    return pl.pallas_call(
        functools.partial(_in_proj_kernel, n_real_tiles=n_real_tiles, q_scale=q_scale),
        grid=(b, n_real_tiles + 1),
        in_specs=[
            pl.BlockSpec((1, t, d_model), lambda bi, i: (bi, jnp.minimum(i, n_real_tiles - 1), 0)),
            pl.BlockSpec((t, d_model), const2),
            pl.BlockSpec((t, HEAD_DIM), lambda bi, i: (i, 0)),
            pl.BlockSpec((t, HEAD_DIM), lambda bi, i: (i, 0)),
            pl.BlockSpec((1, d_model), const2),
            pl.BlockSpec((d_model, in_w), const2),
            pl.BlockSpec((1, HEAD_DIM), const2),
            pl.BlockSpec((1, HEAD_DIM), const2),
            pl.BlockSpec((N_FOURIER_GROUPS, FOURIER_GROUP_W, 2 * FOURIER_GROUP_W), lambda bi, i: (0, 0, 0)),
        ],
        out_specs=[
            pl.BlockSpec((1, t, ATTN_W), lambda bi, i: (bi, i, 0)),
            pl.BlockSpec((1, t, KV_W), lambda bi, i: (bi, i, 0)),
            pl.BlockSpec((1, t, 2 * KV_W), lambda bi, i: (bi, i, 0)),
            pl.BlockSpec((1, t, FOURIER_W), lambda bi, i: (bi, i, 0)),
            pl.BlockSpec((1, t, FOURIER_W), lambda bi, i: (bi, i, 0)),
        ],
        out_shape=[
            jax.ShapeDtypeStruct((b, n_rows, ATTN_W), BF16),
            jax.ShapeDtypeStruct((b, n_rows, KV_W), BF16),
            jax.ShapeDtypeStruct((b, n_rows, 2 * KV_W), BF16),
            jax.ShapeDtypeStruct((b, l, FOURIER_W), BF16),
            jax.ShapeDtypeStruct((b, l, FOURIER_W), BF16),
        ],
        compiler_params=pltpu.CompilerParams(
            dimension_semantics=("arbitrary", "arbitrary"), vmem_limit_bytes=_vmem_limit(est)),
    )(x, meta_pad, cos_tab, sin_tab, g_mix, w_in, g_q, g_k, ab)


_NT = (((1,), (1,)), ((), ()))


def _attention_bounded_kernel(q_ref, qn_ref, k_ref, v_ref, km_ref, vm_ref, o_ref, q2_ref, p0_ref, p1_ref, acc_ref,
                              *, n_chunks, chunk):
    d = HEAD_DIM
    tq = q_ref.shape[1]
    first_tile = pl.program_id(2) == 0
    q2_ref[0, :tq] = q_ref[0, :, :d]
    q2_ref[0, tq:] = q_ref[0, :, d:]
    q2_ref[1, :tq] = qn_ref[0, :, :d]
    q2_ref[1, tq:] = qn_ref[0, :, d:]

    def probs(tile, c, p_ref):
        start = pl.multiple_of(c * chunk, chunk)
        s = lax.dot_general(q2_ref[tile], k_ref[0, pl.ds(start, chunk), :], _NT, preferred_element_type=F32)
        p_ref[...] = jnp.exp2(s).astype(BF16)

    def accumulate(c, p_ref):
        start = pl.multiple_of(c * chunk, chunk)
        acc_ref[...] += jnp.dot(p_ref[...], v_ref[0, pl.ds(start, chunk), :], preferred_element_type=F32)

    s_meta = lax.dot_general(q2_ref[0], km_ref[0], _NT, preferred_element_type=F32)
    acc_ref[...] = jnp.dot(jnp.exp2(s_meta).astype(BF16), vm_ref[0], preferred_element_type=F32)

    @pl.when(first_tile)
    def _():
        probs(0, 0, p0_ref)

    bufs = (p0_ref, p1_ref)
    per_trip = ATTN_CHUNKS_PER_TRIP if n_chunks % ATTN_CHUNKS_PER_TRIP == 0 else 2

    def body(j, carry):
        c = per_trip * j
        for u in range(per_trip):
            nxt = c + u + 1
            if u + 1 < per_trip:
                probs(0, nxt, bufs[(u + 1) % 2])
            else:
                wraps = nxt >= n_chunks
                probs(wraps.astype(jnp.int32), jnp.where(wraps, 0, nxt), bufs[(u + 1) % 2])
            accumulate(c + u, bufs[u % 2])
        return carry

    lax.fori_loop(0, n_chunks // per_trip, body, 0)

    acc = acc_ref[...]
    out = acc[:, :d] / acc[:, d:]
    o_ref[0, :, :d] = out[:tq]
    o_ref[0, :, d:] = out[tq:]


def _attention_kernel(q_ref, k_ref, v_ref, km_ref, vm_ref, o_ref, q2_ref, s0_ref, s1_ref, m_ref, acc_ref,
                      *, n_chunks, chunk):
    d = HEAD_DIM
    tq = q_ref.shape[1]
    q2_ref[:tq] = q_ref[0, :, :d]
    q2_ref[tq:] = q_ref[0, :, d:]

    def scores(c, s_ref):
        start = pl.multiple_of(c * chunk, chunk)
        s_ref[...] = lax.dot_general(
            q2_ref[...], k_ref[0, pl.ds(start, chunk), :], _NT, preferred_element_type=F32)

    def update(c, s_ref):
        start = pl.multiple_of(c * chunk, chunk)
        s = s_ref[...]
        m_old = m_ref[...]
        m_new = jnp.maximum(m_old, jnp.max(s, axis=-1, keepdims=True))
        p = jnp.exp2(s - m_new).astype(BF16)
        pv = jnp.dot(p, v_ref[0, pl.ds(start, chunk), :], preferred_element_type=F32)
        acc_ref[...] = acc_ref[...] * jnp.exp2(m_old - m_new) + pv
        m_ref[...] = m_new

    s_meta = lax.dot_general(q2_ref[...], km_ref[0], _NT, preferred_element_type=F32)
    m_meta = jnp.max(s_meta, axis=-1, keepdims=True)
    m_ref[...] = m_meta
    acc_ref[...] = jnp.dot(jnp.exp2(s_meta - m_meta).astype(BF16), vm_ref[0], preferred_element_type=F32)

    scores(0, s0_ref)

    def body(j, carry):
        c = 2 * j
        scores(c + 1, s1_ref)
        update(c, s0_ref)
        scores(jnp.minimum(c + 2, n_chunks - 1), s0_ref)
        update(c + 1, s1_ref)
        return carry

    lax.fori_loop(0, n_chunks // 2, body, 0)

    acc = acc_ref[...]
    out = acc[:, :d] / acc[:, d:]
    o_ref[0, :, :d] = out[:tq]
    o_ref[0, :, d:] = out[tq:]


def _attention_call(q, k, v, n_tok, bounded):
    b = q.shape[0]
    tq = min(ATTN_Q_TILE, n_tok)
    chunk = min(ATTN_KV_CHUNK, n_tok)
    d = HEAD_DIM
    gq = Q_GROUP * d
    m = Q_GROUP * tq
    meta_blk = n_tok // N_META
    assert (n_tok // chunk) % 2 == 0
    est = 2 * (tq * gq * 2 + n_tok * d * 2 + n_tok * 2 * d * 2 + tq * gq * 4) + m * (
        chunk * 4 * 3 + chunk * 2 + 2 * d * 4 * 3)
    n_tiles = n_tok // tq
    q_spec = pl.BlockSpec((1, tq, gq), lambda bi, g, i: (bi, i, g))
    if bounded:
        body = _attention_bounded_kernel
        scratch = [pltpu.VMEM((2, m, d), BF16), pltpu.VMEM((m, chunk), BF16), pltpu.VMEM((m, chunk), BF16),
                   pltpu.VMEM((m, 2 * d), F32)]
        q_specs = [q_spec, pl.BlockSpec((1, tq, gq), lambda bi, g, i: (bi, jnp.minimum(i + 1, n_tiles - 1), g))]
        q_args = (q, q)
    else:
        body = _attention_kernel
        scratch = [pltpu.VMEM((m, d), BF16), pltpu.VMEM((m, chunk), F32), pltpu.VMEM((m, chunk), F32),
                   pltpu.VMEM((m, 1), F32), pltpu.VMEM((m, 2 * d), F32)]
        q_specs = [q_spec]
        q_args = (q,)
    return pl.pallas_call(
        functools.partial(body, n_chunks=n_tok // chunk, chunk=chunk),
        grid=(b, N_KV_HEADS, n_tiles),
        in_specs=q_specs + [
            pl.BlockSpec((1, n_tok, d), lambda bi, g, i: (bi, 0, g)),
            pl.BlockSpec((1, n_tok, 2 * d), lambda bi, g, i: (bi, 0, g)),
            pl.BlockSpec((1, N_META, d), lambda bi, g, i: (bi, meta_blk, g)),
            pl.BlockSpec((1, N_META, 2 * d), lambda bi, g, i: (bi, meta_blk, g)),
        ],
        out_specs=pl.BlockSpec((1, tq, gq), lambda bi, g, i: (bi, i, g)),
        out_shape=jax.ShapeDtypeStruct((b, n_tok, ATTN_W), F32),
        scratch_shapes=scratch,
        compiler_params=pltpu.CompilerParams(
            dimension_semantics=("arbitrary", "arbitrary", "arbitrary"), vmem_limit_bytes=_vmem_limit(est)),
    )(*q_args, k, v, k, v)


def _attention(q, k, v, g_q, g_k, n_tok):
    score_bound = (math.log2(math.e) / math.sqrt(HEAD_DIM)) * HEAD_DIM * BOUND_SLACK * (
        jnp.max(jnp.abs(g_q)) * jnp.max(jnp.abs(g_k)))
    return lax.cond(
        score_bound <= MAX_UNSHIFTED_SCORE,
        lambda: _attention_call(q, k, v, n_tok, True),
        lambda: _attention_call(q, k, v, n_tok, False))


def _dft1_kernel(f_ref, t_ref, p_ref, q_ref, g_ref, *, n2, cols):
    step = pl.program_id(0)
    n1 = t_ref.shape[2]
    n_sub = t_ref.shape[3]
    f_top = f_ref[:n1, :]
    f_bot = f_ref[n1:, :]
    for j in range(n_sub):
        z = jnp.concatenate(
            [p_ref[0, :, j * cols:(j + 1) * cols], q_ref[0, :, j * cols:(j + 1) * cols]], axis=0)
        valid = step * n_sub + j < n2
        z = jnp.where(valid, z, jnp.zeros_like(z))
        tr = t_ref[0, 0, :, j:j + 1]
        ti = t_ref[0, 1, :, j:j + 1]
        m = jnp.concatenate([tr * f_top - ti * f_bot, ti * f_top + tr * f_bot], axis=0).astype(BF16)
        g_ref[0, j] = jnp.dot(m, z, preferred_element_type=F32).astype(BF16)


def _dft2_kernel(f_ref, g_ref, y_ref, *, n2):
    f = f_ref[...].astype(BF16)
    for kk in range(g_ref.shape[1]):
        res = jnp.dot(f, g_ref[0, kk], preferred_element_type=F32)
        y_ref[0, kk] = res[:n2]


def _sequence_dft(p, q, l):
    b = p.shape[0]
    cols = p.shape[2]
    n1, n2 = _dft_factors(l)
    tn = DFT_N2_TILE
    n2_pad = -(-n2 // tn) * tn
    stage1, twiddle, stage2 = _dft_stage_tables(l, n1, n2, n2_pad)
    twiddle = twiddle.reshape(2, n1, n2_pad // tn, tn).transpose(2, 0, 1, 3)
    pv = p.reshape(b, n1, n2 * cols)
    qv = q.reshape(b, n1, n2 * cols)
    est1 = 2 * (4 * n1 * n1 * 4 + 2 * n1 * tn * cols * 2 + tn * 2 * n1 * cols * 4) + 8 * n1 * cols * 4
    g = pl.pallas_call(
        functools.partial(_dft1_kernel, n2=n2, cols=cols),
        grid=(n2_pad // tn, b),
        in_specs=[
            pl.BlockSpec((2 * n1, 2 * n1), lambda i, bi: (0, 0)),
            pl.BlockSpec((1, 2, n1, tn), lambda i, bi: (i, 0, 0, 0)),
            pl.BlockSpec((1, n1, tn * cols), lambda i, bi: (bi, 0, i)),
            pl.BlockSpec((1, n1, tn * cols), lambda i, bi: (bi, 0, i)),
        ],
        out_specs=pl.BlockSpec((1, tn, 2 * n1, cols), lambda i, bi: (bi, i, 0, 0)),
        out_shape=jax.ShapeDtypeStruct((b, n2_pad, 2 * n1, cols), BF16),
        compiler_params=pltpu.CompilerParams(
            dimension_semantics=("arbitrary", "arbitrary"), vmem_limit_bytes=_vmem_limit(est1)),
    )(jnp.asarray(stage1), jnp.asarray(twiddle), pv, qv)

    gv = g.reshape(b, 2 * n2_pad, n1, cols).transpose(0, 2, 1, 3)
    tk = DFT_K1_TILE
    assert n1 % tk == 0
    est2 = 2 * (2 * n2_pad * n2_pad * 4 + tk * 2 * n2_pad * cols * 2 + tk * n2 * cols * 4) + 3 * n2_pad * cols * 4
    y = pl.pallas_call(
        functools.partial(_dft2_kernel, n2=n2),
        grid=(b, n1 // tk),
        in_specs=[
            pl.BlockSpec((n2_pad, 2 * n2_pad), lambda bi, i: (0, 0)),
            pl.BlockSpec((1, tk, 2 * n2_pad, cols), lambda bi, i: (bi, i, 0, 0)),
        ],
        out_specs=pl.BlockSpec((1, tk, n2, cols), lambda bi, i: (bi, i, 0, 0)),
        out_shape=jax.ShapeDtypeStruct((b, n1, n2, cols), F32),
        compiler_params=pltpu.CompilerParams(
            dimension_semantics=("arbitrary", "arbitrary"), vmem_limit_bytes=_vmem_limit(est2)),
    )(jnp.asarray(stage2), gv)
    return y.transpose(0, 2, 1, 3).reshape(b, l, cols)


def _out_mlp_kernel(
    x_ref, a_ref, f_ref, ga_ref, gf_ref, wout_ref, gmlp_ref, wup_ref, wdown_ref, gfin_ref, o_ref,
):
    an = _rms(a_ref[0], ga_ref[...]).astype(BF16)
    fn = _rms(f_ref[0], gf_ref[...]).astype(BF16)
    h = x_ref[0]
    h = h + jnp.dot(an, wout_ref[:ATTN_W, :], preferred_element_type=F32)
    h = h + jnp.dot(fn, wout_ref[ATTN_W:, :], preferred_element_type=F32)
    m = _rms(h, gmlp_ref[...]).astype(BF16)
    act = jnp.maximum(jnp.dot(m, wup_ref[...], preferred_element_type=F32), 0.0)
    out = h + jnp.dot((act * act).astype(BF16), wdown_ref[...], preferred_element_type=F32)
    o_ref[0] = _rms(out, gfin_ref[...])


def _out_mlp(x, attn, four, g_attn_out, g_fourier_out, w_out, g_mlp, w_up, w_down, g_final):
    b, n_tok, d_model = x.shape
    d_ff = w_up.shape[1]
    t = TOKEN_TILE
    const2 = lambda bi, i: (0, 0)
    tile3 = lambda bi, i: (bi, i, 0)
    est = (2 * (2 * t * d_model * 4 + t * ATTN_W * 4 + t * FOURIER_W * 4)
           + 2 * (d_model * d_model + 2 * d_model * d_ff) * 2 + t * (4 * d_model * 4 + d_ff * 6))
    return pl.pallas_call(
        _out_mlp_kernel,
        grid=(b, n_tok // t),
        in_specs=[
            pl.BlockSpec((1, t, d_model), tile3),
            pl.BlockSpec((1, t, ATTN_W), tile3),
            pl.BlockSpec((1, t, FOURIER_W), tile3),
            pl.BlockSpec((1, ATTN_W), const2),
            pl.BlockSpec((1, FOURIER_W), const2),
            pl.BlockSpec((d_model, d_model), const2),
            pl.BlockSpec((1, d_model), const2),
            pl.BlockSpec((d_model, d_ff), const2),
            pl.BlockSpec((d_ff, d_model), const2),
            pl.BlockSpec((1, d_model), const2),
        ],
        out_specs=pl.BlockSpec((1, t, d_model), tile3),
        out_shape=jax.ShapeDtypeStruct((b, n_tok, d_model), F32),
        compiler_params=pltpu.CompilerParams(
            dimension_semantics=("arbitrary", "arbitrary"), vmem_limit_bytes=_vmem_limit(est)),
    )(x, attn, four, g_attn_out, g_fourier_out, w_out, g_mlp, w_up, w_down, g_final)


def kernel(x, meta_tokens, g_mix, w_in, g_q, g_k, w_fourier, g_attn_out, g_fourier_out, w_out,
           g_mlp, w_up, w_down, g_final):
    assert g_mix.shape[0] == 1, "meta-token rows are only carried for a single layer"
    b, n_tok, d_model = x.shape
    l = n_tok + N_META
    t = TOKEN_TILE
    assert n_tok % t == 0 and n_tok % GRID_W == 0

    cos_tab, sin_tab = _rope_tables(n_tok, n_tok + t)
    meta_pad = jnp.pad(meta_tokens.astype(F32), ((0, t - N_META), (0, 0)))
    ab = _fourier_weights(w_fourier[0], l)
    q, k, v, p, qq = _in_proj(
        x, meta_pad, cos_tab, sin_tab, g_mix, w_in[0].astype(BF16), g_q, g_k, ab, l)
    attn = _attention(q, k, v, g_q, g_k, n_tok)
    four = _sequence_dft(p, qq, l)
    return _out_mlp(
        x, attn, four, g_attn_out, g_fourier_out, w_out[0].astype(BF16), g_mlp,
        w_up[0].astype(BF16), w_down[0].astype(BF16), g_final[None, :])
```

```python
import functools
import math

import jax
import jax.numpy as jnp
import numpy as np
from jax import lax
from jax.experimental import pallas as pl
from jax.experimental.pallas import tpu as pltpu

N_META = 16
GRID_W = 64
HEAD_DIM = 128
N_Q_HEADS = 4
N_KV_HEADS = 2
Q_GROUP = N_Q_HEADS // N_KV_HEADS
ATTN_W = N_Q_HEADS * HEAD_DIM
KV_W = N_KV_HEADS * HEAD_DIM
N_FOURIER_GROUPS = 4
FOURIER_GROUP_W = 128
FOURIER_W = N_FOURIER_GROUPS * FOURIER_GROUP_W
ROPE_THETA = 10000.0
ROPE_AXIS_DIM = HEAD_DIM // 2
RMS_EPS = 1e-6

V7X_VMEM_BYTES = 64 * 1024 * 1024
V7X_BF16_SUBLANES = 16
VMEM_RESERVE_BYTES = 8 * 1024 * 1024
VMEM_MIN_LIMIT_BYTES = 32 * 1024 * 1024

TOKEN_TILE = 512
IN_PROJ_ROWS = 256
ATTN_Q_TILE = 1024
ATTN_KV_CHUNK = 1024
ATTN_CHUNKS_PER_TRIP = 8
DFT_N2_TILE = 52
DFT_K1_TILE = 16
MAX_UNSHIFTED_SCORE = 64.0
BOUND_SLACK = 1.02

F32 = jnp.float32
BF16 = jnp.bfloat16


def _vmem_limit(estimate_bytes):
    return int(min(max(2 * estimate_bytes, VMEM_MIN_LIMIT_BYTES), V7X_VMEM_BYTES - VMEM_RESERVE_BYTES))


def _rms(x, g):
    return x * lax.rsqrt(jnp.mean(x * x, axis=-1, keepdims=True) + RMS_EPS) * g


def _dft_factors(l):
    best = None
    for n1 in range(V7X_BF16_SUBLANES, l, V7X_BF16_SUBLANES):
        if l % n1 == 0:
            n2 = l // n1
            if best is None or abs(n1 - n2) < abs(best[0] - best[1]):
                best = (n1, n2)
    assert best is not None
    return best


def _rope_tables(n_tok, n_rows):
    rows_count = n_tok // GRID_W
    real_row = np.repeat(np.arange(rows_count, dtype=np.float64), GRID_W)
    real_col = np.tile(np.arange(GRID_W, dtype=np.float64), rows_count)
    meta_row = np.full((N_META,), -1.0)
    meta_col = np.arange(N_META, dtype=np.float64)
    pad = np.zeros((n_rows - n_tok - N_META,))
    row = np.concatenate([real_row, meta_row, pad])
    col = np.concatenate([real_col, meta_col, pad])
    inv_freq = ROPE_THETA ** (-np.arange(0, ROPE_AXIS_DIM, 2, dtype=np.float64) / ROPE_AXIS_DIM)
    ang_r = row[:, None] * inv_freq[None, :]
    ang_c = col[:, None] * inv_freq[None, :]
    cos = np.concatenate([np.cos(ang_r)] * 2 + [np.cos(ang_c)] * 2, axis=-1)
    sin = np.concatenate([-np.sin(ang_r), np.sin(ang_r), -np.sin(ang_c), np.sin(ang_c)], axis=-1)
    return jnp.asarray(cos.astype(np.float32)), jnp.asarray(sin.astype(np.float32))


def _channel_dft_tables(l):
    n = np.arange(FOURIER_GROUP_W)
    ang = 2.0 * np.pi * ((n[:, None] * n[None, :]) % FOURIER_GROUP_W) / FOURIER_GROUP_W
    scale = 1.0 / math.sqrt(l * FOURIER_GROUP_W)
    return (np.cos(ang) * scale).astype(np.float32), (np.sin(ang) * scale).astype(np.float32)


def _dft_stage_tables(l, n1, n2, n2_pad):
    s = N_META
    k1 = np.arange(n1)[:, None]
    ang1 = 2.0 * np.pi * (((k1 + s) * n2 * np.arange(n1)[None, :]) % l) / l
    fr, fi = np.cos(ang1), np.sin(ang1)
    stage1 = np.concatenate(
        [np.concatenate([fr, -fi], axis=1), np.concatenate([fi, fr], axis=1)], axis=0).astype(np.float32)
    ang_t = 2.0 * np.pi * (((k1 + s) * (np.arange(n2)[None, :] + s)) % l) / l
    twiddle = np.zeros((2, n1, n2_pad), np.float32)
    twiddle[0, :, :n2] = np.cos(ang_t)
    twiddle[1, :, :n2] = np.sin(ang_t)
    k2 = np.arange(n2)[:, None]
    jj = np.arange(n2)[None, :]
    ang2 = 2.0 * np.pi * ((k2 * (jj + s)) % n2) / n2
    stage2 = np.zeros((n2_pad, n2_pad, 2), np.float32)
    stage2[:n2, :n2, 0] = np.cos(ang2)
    stage2[:n2, :n2, 1] = -np.sin(ang2)
    return stage1, twiddle, stage2.reshape(n2_pad, 2 * n2_pad)


def _fourier_weight_kernel(c_ref, s_ref, w_ref, ab_ref):
    w = w_ref[0]
    a = jnp.dot(c_ref[...], w, preferred_element_type=F32, precision=lax.Precision.HIGHEST)
    b = jnp.dot(s_ref[...], w, preferred_element_type=F32, precision=lax.Precision.HIGHEST)
    ab_ref[0, :, :FOURIER_GROUP_W] = a.astype(BF16)
    ab_ref[0, :, FOURIER_GROUP_W:] = b.astype(BF16)


def _fourier_weights(w_f, l):
    c, s = _channel_dft_tables(l)
    gw = FOURIER_GROUP_W
    return pl.pallas_call(
        _fourier_weight_kernel,
        grid=(N_FOURIER_GROUPS,),
        in_specs=[
            pl.BlockSpec((gw, gw), lambda g: (0, 0)),
            pl.BlockSpec((gw, gw), lambda g: (0, 0)),
            pl.BlockSpec((1, gw, gw), lambda g: (g, 0, 0)),
        ],
        out_specs=pl.BlockSpec((1, gw, 2 * gw), lambda g: (g, 0, 0)),
        out_shape=jax.ShapeDtypeStruct((N_FOURIER_GROUPS, gw, 2 * gw), BF16),
    )(jnp.asarray(c), jnp.asarray(s), w_f)


def _rope(x, cos, sin, lower_half):
    partner = jnp.where(lower_half, pltpu.roll(x, HEAD_DIM - 32, axis=1), pltpu.roll(x, 32, axis=1))
    return x * cos + partner * sin


def _in_proj_kernel(
    x_ref, meta_ref, cos_ref, sin_ref, gmix_ref, win_ref, gq_ref, gk_ref, ab_ref,
    q_ref, k_ref, v_ref, p_ref, qq_ref, *, n_real_tiles, q_scale,
):
    i = pl.program_id(1)
    is_meta = i == n_real_tiles
    d = HEAD_DIM
    gw = FOURIER_GROUP_W
    lane = lax.broadcasted_iota(jnp.int32, (IN_PROJ_ROWS, d), 1)
    lower_half = (lane % ROPE_AXIS_DIM) < (ROPE_AXIS_DIM // 2)
    for r in range(x_ref.shape[1] // IN_PROJ_ROWS):
        rows = pl.ds(r * IN_PROJ_ROWS, IN_PROJ_ROWS)
        xt = jnp.where(is_meta, meta_ref[rows, :], x_ref[0, rows, :])
        hn = _rms(xt, gmix_ref[...]).astype(BF16)
        proj = jnp.dot(hn, win_ref[...], preferred_element_type=F32)
        cos = cos_ref[rows, :]
        sin = sin_ref[rows, :]
        for h in range(N_Q_HEADS):
            qh = _rope(_rms(proj[:, h * d:(h + 1) * d], gq_ref[...]), cos, sin, lower_half)
            q_ref[0, rows, h * d:(h + 1) * d] = (qh * q_scale).astype(BF16)
        for h in range(N_KV_HEADS):
            off = ATTN_W + h * d
            kh = _rope(_rms(proj[:, off:off + d], gk_ref[...]), cos, sin, lower_half)
            k_ref[0, rows, h * d:(h + 1) * d] = kh.astype(BF16)
            off = ATTN_W + KV_W + h * d
            v_ref[0, rows, 2 * h * d:(2 * h + 1) * d] = proj[:, off:off + d].astype(BF16)
            v_ref[0, rows, (2 * h + 1) * d:(2 * h + 2) * d] = jnp.ones((IN_PROJ_ROWS, d), BF16)
        for g in range(N_FOURIER_GROUPS):
            off = ATTN_W + 2 * KV_W + g * gw
            pq = jnp.dot(proj[:, off:off + gw].astype(BF16), ab_ref[g], preferred_element_type=F32)
            p_ref[0, rows, g * gw:(g + 1) * gw] = pq[:, :gw].astype(BF16)
            qq_ref[0, rows, g * gw:(g + 1) * gw] = pq[:, gw:].astype(BF16)


def _in_proj(x, meta_pad, cos_tab, sin_tab, g_mix, w_in, g_q, g_k, ab, l):
    b, n_tok, d_model = x.shape
    t = TOKEN_TILE
    n_real_tiles = n_tok // t
    n_rows = (n_real_tiles + 1) * t
    in_w = w_in.shape[1]
    q_scale = math.log2(math.e) / math.sqrt(HEAD_DIM)
    const2 = lambda bi, i: (0, 0)
    est = (2 * t * d_model * 4 * 2 + d_model * in_w * 2 * 2 + t * in_w * 4 * 3
           + 2 * t * (2 * ATTN_W + KV_W + 2 * FOURIER_W) * 2)
    return pl.pallas_call(
        functools.partial(_in_proj_kernel, n_real_tiles=n_real_tiles, q_scale=q_scale),
        grid=(b, n_real_tiles + 1),
        in_specs=[
            pl.BlockSpec((1, t, d_model), lambda bi, i: (bi, jnp.minimum(i, n_real_tiles - 1), 0)),
            pl.BlockSpec((t, d_model), const2),
            pl.BlockSpec((t, HEAD_DIM), lambda bi, i: (i, 0)),
            pl.BlockSpec((t, HEAD_DIM), lambda bi, i: (i, 0)),
            pl.BlockSpec((1, d_model), const2),
            pl.BlockSpec((d_model, in_w), const2),
            pl.BlockSpec((1, HEAD_DIM), const2),
            pl.BlockSpec((1, HEAD_DIM), const2),
            pl.BlockSpec((N_FOURIER_GROUPS, FOURIER_GROUP_W, 2 * FOURIER_GROUP_W), lambda bi, i: (0, 0, 0)),
        ],
        out_specs=[
            pl.BlockSpec((1, t, ATTN_W), lambda bi, i: (bi, i, 0)),
            pl.BlockSpec((1, t, KV_W), lambda bi, i: (bi, i, 0)),
            pl.BlockSpec((1, t, 2 * KV_W), lambda bi, i: (bi, i, 0)),
            pl.BlockSpec((1, t, FOURIER_W), lambda bi, i: (bi, i, 0)),
            pl.BlockSpec((1, t, FOURIER_W), lambda bi, i: (bi, i, 0)),
        ],
        out_shape=[
            jax.ShapeDtypeStruct((b, n_rows, ATTN_W), BF16),
            jax.ShapeDtypeStruct((b, n_rows, KV_W), BF16),
            jax.ShapeDtypeStruct((b, n_rows, 2 * KV_W), BF16),
            jax.ShapeDtypeStruct((b, l, FOURIER_W), BF16),
            jax.ShapeDtypeStruct((b, l, FOURIER_W), BF16),
        ],
        compiler_params=pltpu.CompilerParams(
            dimension_semantics=("arbitrary", "arbitrary"), vmem_limit_bytes=_vmem_limit(est)),
    )(x, meta_pad, cos_tab, sin_tab, g_mix, w_in, g_q, g_k, ab)


_NT = (((1,), (1,)), ((), ()))


def _attention_bounded_kernel(q_ref, qn_ref, k_ref, v_ref, km_ref, vm_ref, o_ref, q2_ref, p0_ref, p1_ref, acc_ref,
                              *, n_chunks, chunk):
    d = HEAD_DIM
    tq = q_ref.shape[1]
    first_tile = pl.program_id(2) == 0
    q2_ref[0, :tq] = q_ref[0, :, :d]
    q2_ref[0, tq:] = q_ref[0, :, d:]
    q2_ref[1, :tq] = qn_ref[0, :, :d]
    q2_ref[1, tq:] = qn_ref[0, :, d:]

    def probs(tile, c, p_ref):
        start = pl.multiple_of(c * chunk, chunk)
        s = lax.dot_general(q2_ref[tile], k_ref[0, pl.ds(start, chunk), :], _NT, preferred_element_type=F32)
        p_ref[...] = jnp.exp2(s).astype(BF16)

    def accumulate(c, p_ref):
        start = pl.multiple_of(c * chunk, chunk)
        acc_ref[...] += jnp.dot(p_ref[...], v_ref[0, pl.ds(start, chunk), :], preferred_element_type=F32)

    s_meta = lax.dot_general(q2_ref[0], km_ref[0], _NT, preferred_element_type=F32)
    acc_ref[...] = jnp.dot(jnp.exp2(s_meta).astype(BF16), vm_ref[0], preferred_element_type=F32)

    @pl.when(first_tile)
    def _():
        probs(0, 0, p0_ref)

    bufs = (p0_ref, p1_ref)
    per_trip = ATTN_CHUNKS_PER_TRIP if n_chunks % ATTN_CHUNKS_PER_TRIP == 0 else 2

    def body(j, carry):
        c = per_trip * j
        for u in range(per_trip):
            nxt = c + u + 1
            if u + 1 < per_trip:
                probs(0, nxt, bufs[(u + 1) % 2])
            else:
                wraps = nxt >= n_chunks
                probs(wraps.astype(jnp.int32), jnp.where(wraps, 0, nxt), bufs[(u + 1) % 2])
            accumulate(c + u, bufs[u % 2])
        return carry

    lax.fori_loop(0, n_chunks // per_trip, body, 0)

    acc = acc_ref[...]
    out = acc[:, :d] / acc[:, d:]
    o_ref[0, :, :d] = out[:tq]
    o_ref[0, :, d:] = out[tq:]


def _attention_kernel(q_ref, k_ref, v_ref, km_ref, vm_ref, o_ref, q2_ref, s0_ref, s1_ref, m_ref, acc_ref,
                      *, n_chunks, chunk):
    d = HEAD_DIM
    tq = q_ref.shape[1]
    q2_ref[:tq] = q_ref[0, :, :d]
    q2_ref[tq:] = q_ref[0, :, d:]

    def scores(c, s_ref):
        start = pl.multiple_of(c * chunk, chunk)
        s_ref[...] = lax.dot_general(
            q2_ref[...], k_ref[0, pl.ds(start, chunk), :], _NT, preferred_element_type=F32)

    def update(c, s_ref):
        start = pl.multiple_of(c * chunk, chunk)
        s = s_ref[...]
        m_old = m_ref[...]
        m_new = jnp.maximum(m_old, jnp.max(s, axis=-1, keepdims=True))
        p = jnp.exp2(s - m_new).astype(BF16)
        pv = jnp.dot(p, v_ref[0, pl.ds(start, chunk), :], preferred_element_type=F32)
        acc_ref[...] = acc_ref[...] * jnp.exp2(m_old - m_new) + pv
        m_ref[...] = m_new

    s_meta = lax.dot_general(q2_ref[...], km_ref[0], _NT, preferred_element_type=F32)
    m_meta = jnp.max(s_meta, axis=-1, keepdims=True)
    m_ref[...] = m_meta
    acc_ref[...] = jnp.dot(jnp.exp2(s_meta - m_meta).astype(BF16), vm_ref[0], preferred_element_type=F32)

    scores(0, s0_ref)

    def body(j, carry):
        c = 2 * j
        scores(c + 1, s1_ref)
        update(c, s0_ref)
        scores(jnp.minimum(c + 2, n_chunks - 1), s0_ref)
        update(c + 1, s1_ref)
        return carry

    lax.fori_loop(0, n_chunks // 2, body, 0)

    acc = acc_ref[...]
    out = acc[:, :d] / acc[:, d:]
    o_ref[0, :, :d] = out[:tq]
    o_ref[0, :, d:] = out[tq:]


def _attention_call(q, k, v, n_tok, bounded):
    b = q.shape[0]
    tq = min(ATTN_Q_TILE, n_tok)
    chunk = min(ATTN_KV_CHUNK, n_tok)
    d = HEAD_DIM
    gq = Q_GROUP * d
    m = Q_GROUP * tq
    meta_blk = n_tok // N_META
    assert (n_tok // chunk) % 2 == 0
    est = 2 * (tq * gq * 2 + n_tok * d * 2 + n_tok * 2 * d * 2 + tq * gq * 4) + m * (
        chunk * 4 * 3 + chunk * 2 + 2 * d * 4 * 3)
    n_tiles = n_tok // tq
    q_spec = pl.BlockSpec((1, tq, gq), lambda bi, g, i: (bi, i, g))
    if bounded:
        body = _attention_bounded_kernel
        scratch = [pltpu.VMEM((2, m, d), BF16), pltpu.VMEM((m, chunk), BF16), pltpu.VMEM((m, chunk), BF16),
                   pltpu.VMEM((m, 2 * d), F32)]
        q_specs = [q_spec, pl.BlockSpec((1, tq, gq), lambda bi, g, i: (bi, jnp.minimum(i + 1, n_tiles - 1), g))]
        q_args = (q, q)
    else:
        body = _attention_kernel
        scratch = [pltpu.VMEM((m, d), BF16), pltpu.VMEM((m, chunk), F32), pltpu.VMEM((m, chunk), F32),
                   pltpu.VMEM((m, 1), F32), pltpu.VMEM((m, 2 * d), F32)]
        q_specs = [q_spec]
        q_args = (q,)
    return pl.pallas_call(
        functools.partial(body, n_chunks=n_tok // chunk, chunk=chunk),
        grid=(b, N_KV_HEADS, n_tiles),
        in_specs=q_specs + [
            pl.BlockSpec((1, n_tok, d), lambda bi, g, i: (bi, 0, g)),
            pl.BlockSpec((1, n_tok, 2 * d), lambda bi, g, i: (bi, 0, g)),
            pl.BlockSpec((1, N_META, d), lambda bi, g, i: (bi, meta_blk, g)),
            pl.BlockSpec((1, N_META, 2 * d), lambda bi, g, i: (bi, meta_blk, g)),
        ],
        out_specs=pl.BlockSpec((1, tq, gq), lambda bi, g, i: (bi, i, g)),
        out_shape=jax.ShapeDtypeStruct((b, n_tok, ATTN_W), F32),
        scratch_shapes=scratch,
        compiler_params=pltpu.CompilerParams(
            dimension_semantics=("arbitrary", "arbitrary", "arbitrary"), vmem_limit_bytes=_vmem_limit(est)),
    )(*q_args, k, v, k, v)


def _attention(q, k, v, g_q, g_k, n_tok):
    score_bound = (math.log2(math.e) / math.sqrt(HEAD_DIM)) * HEAD_DIM * BOUND_SLACK * (
        jnp.max(jnp.abs(g_q)) * jnp.max(jnp.abs(g_k)))
    return lax.cond(
        score_bound <= MAX_UNSHIFTED_SCORE,
        lambda: _attention_call(q, k, v, n_tok, True),
        lambda: _attention_call(q, k, v, n_tok, False))


def _dft1_kernel(f_ref, t_ref, p_ref, q_ref, g_ref, *, n2, cols):
    step = pl.program_id(0)
    n1 = t_ref.shape[2]
    n_sub = t_ref.shape[3]
    f_top = f_ref[:n1, :]
    f_bot = f_ref[n1:, :]
    for j in range(n_sub):
        z = jnp.concatenate(
            [p_ref[0, :, j * cols:(j + 1) * cols], q_ref[0, :, j * cols:(j + 1) * cols]], axis=0)
        valid = step * n_sub + j < n2
        z = jnp.where(valid, z, jnp.zeros_like(z))
        tr = t_ref[0, 0, :, j:j + 1]
        ti = t_ref[0, 1, :, j:j + 1]
        m = jnp.concatenate([tr * f_top - ti * f_bot, ti * f_top + tr * f_bot], axis=0).astype(BF16)
        g_ref[0, j] = jnp.dot(m, z, preferred_element_type=F32).astype(BF16)


def _dft2_kernel(f_ref, g_ref, y_ref, *, n2):
    f = f_ref[...].astype(BF16)
    for kk in range(g_ref.shape[1]):
        res = jnp.dot(f, g_ref[0, kk], preferred_element_type=F32)
        y_ref[0, kk] = res[:n2]


def _sequence_dft(p, q, l):
    b = p.shape[0]
    cols = p.shape[2]
    n1, n2 = _dft_factors(l)
    tn = DFT_N2_TILE
    n2_pad = -(-n2 // tn) * tn
    stage1, twiddle, stage2 = _dft_stage_tables(l, n1, n2, n2_pad)
    twiddle = twiddle.reshape(2, n1, n2_pad // tn, tn).transpose(2, 0, 1, 3)
    pv = p.reshape(b, n1, n2 * cols)
    qv = q.reshape(b, n1, n2 * cols)
    est1 = 2 * (4 * n1 * n1 * 4 + 2 * n1 * tn * cols * 2 + tn * 2 * n1 * cols * 4) + 8 * n1 * cols * 4
    g = pl.pallas_call(
        functools.partial(_dft1_kernel, n2=n2, cols=cols),
        grid=(n2_pad // tn, b),
        in_specs=[
            pl.BlockSpec((2 * n1, 2 * n1), lambda i, bi: (0, 0)),
            pl.BlockSpec((1, 2, n1, tn), lambda i, bi: (i, 0, 0, 0)),
            pl.BlockSpec((1, n1, tn * cols), lambda i, bi: (bi, 0, i)),
            pl.BlockSpec((1, n1, tn * cols), lambda i, bi: (bi, 0, i)),
        ],
        out_specs=pl.BlockSpec((1, tn, 2 * n1, cols), lambda i, bi: (bi, i, 0, 0)),
        out_shape=jax.ShapeDtypeStruct((b, n2_pad, 2 * n1, cols), BF16),
        compiler_params=pltpu.CompilerParams(
            dimension_semantics=("arbitrary", "arbitrary"), vmem_limit_bytes=_vmem_limit(est1)),
    )(jnp.asarray(stage1), jnp.asarray(twiddle), pv, qv)

    gv = g.reshape(b, 2 * n2_pad, n1, cols).transpose(0, 2, 1, 3)
    tk = DFT_K1_TILE
    assert n1 % tk == 0
    est2 = 2 * (2 * n2_pad * n2_pad * 4 + tk * 2 * n2_pad * cols * 2 + tk * n2 * cols * 4) + 3 * n2_pad * cols * 4
    y = pl.pallas_call(
        functools.partial(_dft2_kernel, n2=n2),
        grid=(b, n1 // tk),
        in_specs=[
            pl.BlockSpec((n2_pad, 2 * n2_pad), lambda bi, i: (0, 0)),
            pl.BlockSpec((1, tk, 2 * n2_pad, cols), lambda bi, i: (bi, i, 0, 0)),
        ],
        out_specs=pl.BlockSpec((1, tk, n2, cols), lambda bi, i: (bi, i, 0, 0)),
        out_shape=jax.ShapeDtypeStruct((b, n1, n2, cols), F32),
        compiler_params=pltpu.CompilerParams(
            dimension_semantics=("arbitrary", "arbitrary"), vmem_limit_bytes=_vmem_limit(est2)),
    )(jnp.asarray(stage2), gv)
    return y.transpose(0, 2, 1, 3).reshape(b, l, cols)


def _out_mlp_kernel(
    x_ref, a_ref, f_ref, ga_ref, gf_ref, wout_ref, gmlp_ref, wup_ref, wdown_ref, gfin_ref, o_ref,
):
    an = _rms(a_ref[0], ga_ref[...]).astype(BF16)
    fn = _rms(f_ref[0], gf_ref[...]).astype(BF16)
    h = x_ref[0]
    h = h + jnp.dot(an, wout_ref[:ATTN_W, :], preferred_element_type=F32)
    h = h + jnp.dot(fn, wout_ref[ATTN_W:, :], preferred_element_type=F32)
    m = _rms(h, gmlp_ref[...]).astype(BF16)
    act = jnp.maximum(jnp.dot(m, wup_ref[...], preferred_element_type=F32), 0.0)
    out = h + jnp.dot((act * act).astype(BF16), wdown_ref[...], preferred_element_type=F32)
    o_ref[0] = _rms(out, gfin_ref[...])


def _out_mlp(x, attn, four, g_attn_out, g_fourier_out, w_out, g_mlp, w_up, w_down, g_final):
    b, n_tok, d_model = x.shape
    d_ff = w_up.shape[1]
    t = TOKEN_TILE
    const2 = lambda bi, i: (0, 0)
    tile3 = lambda bi, i: (bi, i, 0)
    est = (2 * (2 * t * d_model * 4 + t * ATTN_W * 4 + t * FOURIER_W * 4)
           + 2 * (d_model * d_model + 2 * d_model * d_ff) * 2 + t * (4 * d_model * 4 + d_ff * 6))
    return pl.pallas_call(
        _out_mlp_kernel,
        grid=(b, n_tok // t),
        in_specs=[
            pl.BlockSpec((1, t, d_model), tile3),
            pl.BlockSpec((1, t, ATTN_W), tile3),
            pl.BlockSpec((1, t, FOURIER_W), tile3),
            pl.BlockSpec((1, ATTN_W), const2),
            pl.BlockSpec((1, FOURIER_W), const2),
            pl.BlockSpec((d_model, d_model), const2),
            pl.BlockSpec((1, d_model), const2),
            pl.BlockSpec((d_model, d_ff), const2),
            pl.BlockSpec((d_ff, d_model), const2),
            pl.BlockSpec((1, d_model), const2),
        ],
        out_specs=pl.BlockSpec((1, t, d_model), tile3),
        out_shape=jax.ShapeDtypeStruct((b, n_tok, d_model), F32),
        compiler_params=pltpu.CompilerParams(
            dimension_semantics=("arbitrary", "arbitrary"), vmem_limit_bytes=_vmem_limit(est)),
    )(x, attn, four, g_attn_out, g_fourier_out, w_out, g_mlp, w_up, w_down, g_final)


def kernel(x, meta_tokens, g_mix, w_in, g_q, g_k, w_fourier, g_attn_out, g_fourier_out, w_out,
           g_mlp, w_up, w_down, g_final):
    assert g_mix.shape[0] == 1, "meta-token rows are only carried for a single layer"
    b, n_tok, d_model = x.shape
    l = n_tok + N_META
    t = TOKEN_TILE
    assert n_tok % t == 0 and n_tok % GRID_W == 0

    cos_tab, sin_tab = _rope_tables(n_tok, n_tok + t)
    meta_pad = jnp.pad(meta_tokens.astype(F32), ((0, t - N_META), (0, 0)))
    ab = _fourier_weights(w_fourier[0], l)
    q, k, v, p, qq = _in_proj(
        x, meta_pad, cos_tab, sin_tab, g_mix, w_in[0].astype(BF16), g_q, g_k, ab, l)
    attn = _attention(q, k, v, g_q, g_k, n_tok)
    four = _sequence_dft(p, qq, l)
    return _out_mlp(
        x, attn, four, g_attn_out, g_fourier_out, w_out[0].astype(BF16), g_mlp,
        w_up[0].astype(BF16), w_down[0].astype(BF16), g_final[None, :])
```

```python
import functools
import math

import jax
import jax.numpy as jnp
import numpy as np
from jax import lax
from jax.experimental import pallas as pl
from jax.experimental.pallas import tpu as pltpu

N_META = 16
GRID_W = 64
HEAD_DIM = 128
N_Q_HEADS = 4
N_KV_HEADS = 2
Q_GROUP = N_Q_HEADS // N_KV_HEADS
ATTN_W = N_Q_HEADS * HEAD_DIM
KV_W = N_KV_HEADS * HEAD_DIM
N_FOURIER_GROUPS = 4
FOURIER_GROUP_W = 128
FOURIER_W = N_FOURIER_GROUPS * FOURIER_GROUP_W
ROPE_THETA = 10000.0
ROPE_AXIS_DIM = HEAD_DIM // 2
RMS_EPS = 1e-6

V7X_VMEM_BYTES = 64 * 1024 * 1024
V7X_BF16_SUBLANES = 16
VMEM_RESERVE_BYTES = 8 * 1024 * 1024
VMEM_MIN_LIMIT_BYTES = 32 * 1024 * 1024

TOKEN_TILE = 512
IN_PROJ_TILE = 1024
IN_PROJ_ROWS = 256
ATTN_Q_TILE = 1024
ATTN_KV_CHUNK = 1024
ATTN_CHUNKS_PER_TRIP = 8
DFT_N2_TILE = 52
DFT_K1_TILE = 16
MAX_UNSHIFTED_SCORE = 64.0
BOUND_SLACK = 1.02

F32 = jnp.float32
BF16 = jnp.bfloat16


def _vmem_limit(estimate_bytes):
    return int(min(max(2 * estimate_bytes, VMEM_MIN_LIMIT_BYTES), V7X_VMEM_BYTES - VMEM_RESERVE_BYTES))


def _rms(x, g):
    return x * lax.rsqrt(jnp.mean(x * x, axis=-1, keepdims=True) + RMS_EPS) * g


def _dft_factors(l):
    best = None
    for n1 in range(V7X_BF16_SUBLANES, l, V7X_BF16_SUBLANES):
        if l % n1 == 0:
            n2 = l // n1
            if best is None or abs(n1 - n2) < abs(best[0] - best[1]):
                best = (n1, n2)
    assert best is not None
    return best


def _rope_tables(n_tok, n_rows):
    rows_count = n_tok // GRID_W
    real_row = np.repeat(np.arange(rows_count, dtype=np.float64), GRID_W)
    real_col = np.tile(np.arange(GRID_W, dtype=np.float64), rows_count)
    meta_row = np.full((N_META,), -1.0)
    meta_col = np.arange(N_META, dtype=np.float64)
    pad = np.zeros((n_rows - n_tok - N_META,))
    row = np.concatenate([real_row, meta_row, pad])
    col = np.concatenate([real_col, meta_col, pad])
    inv_freq = ROPE_THETA ** (-np.arange(0, ROPE_AXIS_DIM, 2, dtype=np.float64) / ROPE_AXIS_DIM)
    ang_r = row[:, None] * inv_freq[None, :]
    ang_c = col[:, None] * inv_freq[None, :]
    cos = np.concatenate([np.cos(ang_r)] * 2 + [np.cos(ang_c)] * 2, axis=-1)
    sin = np.concatenate([-np.sin(ang_r), np.sin(ang_r), -np.sin(ang_c), np.sin(ang_c)], axis=-1)
    return jnp.asarray(cos.astype(np.float32)), jnp.asarray(sin.astype(np.float32))


def _channel_dft_tables(l):
    n = np.arange(FOURIER_GROUP_W)
    ang = 2.0 * np.pi * ((n[:, None] * n[None, :]) % FOURIER_GROUP_W) / FOURIER_GROUP_W
    scale = 1.0 / math.sqrt(l * FOURIER_GROUP_W)
    return (np.cos(ang) * scale).astype(np.float32), (np.sin(ang) * scale).astype(np.float32)


def _dft_stage_tables(l, n1, n2, n2_pad):
    s = N_META
    k1 = np.arange(n1)[:, None]
    ang1 = 2.0 * np.pi * (((k1 + s) * n2 * np.arange(n1)[None, :]) % l) / l
    fr, fi = np.cos(ang1), np.sin(ang1)
    stage1 = np.concatenate(
        [np.concatenate([fr, -fi], axis=1), np.concatenate([fi, fr], axis=1)], axis=0).astype(np.float32)
    ang_t = 2.0 * np.pi * (((k1 + s) * (np.arange(n2)[None, :] + s)) % l) / l
    twiddle = np.zeros((2, n1, n2_pad), np.float32)
    twiddle[0, :, :n2] = np.cos(ang_t)
    twiddle[1, :, :n2] = np.sin(ang_t)
    k2 = np.arange(n2)[:, None]
    jj = np.arange(n2)[None, :]
    ang2 = 2.0 * np.pi * ((k2 * (jj + s)) % n2) / n2
    stage2 = np.zeros((n2_pad, n2_pad, 2), np.float32)
    stage2[:n2, :n2, 0] = np.cos(ang2)
    stage2[:n2, :n2, 1] = -np.sin(ang2)
    return stage1, twiddle, stage2.reshape(n2_pad, 2 * n2_pad)


def _fourier_weight_kernel(c_ref, s_ref, w_ref, ab_ref):
    w = w_ref[0]
    a = jnp.dot(c_ref[...], w, preferred_element_type=F32, precision=lax.Precision.HIGHEST)
    b = jnp.dot(s_ref[...], w, preferred_element_type=F32, precision=lax.Precision.HIGHEST)
    ab_ref[0, :, :FOURIER_GROUP_W] = a.astype(BF16)
    ab_ref[0, :, FOURIER_GROUP_W:] = b.astype(BF16)


def _fourier_weights(w_f, l):
    c, s = _channel_dft_tables(l)
    gw = FOURIER_GROUP_W
    return pl.pallas_call(
        _fourier_weight_kernel,
        grid=(N_FOURIER_GROUPS,),
        in_specs=[
            pl.BlockSpec((gw, gw), lambda g: (0, 0)),
            pl.BlockSpec((gw, gw), lambda g: (0, 0)),
            pl.BlockSpec((1, gw, gw), lambda g: (g, 0, 0)),
        ],
        out_specs=pl.BlockSpec((1, gw, 2 * gw), lambda g: (g, 0, 0)),
        out_shape=jax.ShapeDtypeStruct((N_FOURIER_GROUPS, gw, 2 * gw), BF16),
    )(jnp.asarray(c), jnp.asarray(s), w_f)


def _rope(x, cos, sin, lower_half):
    partner = jnp.where(lower_half, pltpu.roll(x, HEAD_DIM - 32, axis=1), pltpu.roll(x, 32, axis=1))
    return x * cos + partner * sin


def _in_proj_kernel(
    x_ref, meta_ref, cos_ref, sin_ref, gmix_ref, win_ref, gq_ref, gk_ref, ab_ref,
    q_ref, k_ref, v_ref, p_ref, qq_ref, *, n_real_tiles, q_scale,
):
    i = pl.program_id(1)
    is_meta = i == n_real_tiles
    d = HEAD_DIM
    gw = FOURIER_GROUP_W
    lane = lax.broadcasted_iota(jnp.int32, (IN_PROJ_ROWS, d), 1)
    lower_half = (lane % ROPE_AXIS_DIM) < (ROPE_AXIS_DIM // 2)
    for r in range(x_ref.shape[1] // IN_PROJ_ROWS):
        rows = pl.ds(r * IN_PROJ_ROWS, IN_PROJ_ROWS)
        xt = jnp.where(is_meta, meta_ref[rows, :], x_ref[0, rows, :])
        hn = _rms(xt, gmix_ref[...]).astype(BF16)
        proj = jnp.dot(hn, win_ref[...], preferred_element_type=F32)
        cos = cos_ref[rows, :]
        sin = sin_ref[rows, :]
        for h in range(N_Q_HEADS):
            qh = _rope(_rms(proj[:, h * d:(h + 1) * d], gq_ref[...]), cos, sin, lower_half)
            q_ref[0, rows, h * d:(h + 1) * d] = (qh * q_scale).astype(BF16)
        for h in range(N_KV_HEADS):
            off = ATTN_W + h * d
            kh = _rope(_rms(proj[:, off:off + d], gk_ref[...]), cos, sin, lower_half)
            k_ref[0, rows, h * d:(h + 1) * d] = kh.astype(BF16)
            off = ATTN_W + KV_W + h * d
            v_ref[0, rows, 2 * h * d:(2 * h + 1) * d] = proj[:, off:off + d].astype(BF16)
            v_ref[0, rows, (2 * h + 1) * d:(2 * h + 2) * d] = jnp.ones((IN_PROJ_ROWS, d), BF16)
        for g in range(N_FOURIER_GROUPS):
            off = ATTN_W + 2 * KV_W + g * gw
            pq = jnp.dot(proj[:, off:off + gw].astype(BF16), ab_ref[g], preferred_element_type=F32)
            p_ref[0, rows, g * gw:(g + 1) * gw] = pq[:, :gw].astype(BF16)
            qq_ref[0, rows, g * gw:(g + 1) * gw] = pq[:, gw:].astype(BF16)


def _in_proj(x, meta_pad, cos_tab, sin_tab, g_mix, w_in, g_q, g_k, ab, l):
    b, n_tok, d_model = x.shape
    t = meta_pad.shape[0]
    n_real_tiles = n_tok // t
    n_rows = (n_real_tiles + 1) * t
    in_w = w_in.shape[1]
    q_scale = math.log2(math.e) / math.sqrt(HEAD_DIM)
    const2 = lambda bi, i: (0, 0)
    est = (2 * t * d_model * 4 * 2 + d_model * in_w * 2 * 2 + t * in_w * 4 * 3
           + 2 * t * (2 * ATTN_W + KV_W + 2 * FOURIER_W) * 2)
    return pl.pallas_call(
        functools.partial(_in_proj_kernel, n_real_tiles=n_real_tiles, q_scale=q_scale),
        grid=(b, n_real_tiles + 1),
        in_specs=[
            pl.BlockSpec((1, t, d_model), lambda bi, i: (bi, jnp.minimum(i, n_real_tiles - 1), 0)),
            pl.BlockSpec((t, d_model), const2),
            pl.BlockSpec((t, HEAD_DIM), lambda bi, i: (i, 0)),
            pl.BlockSpec((t, HEAD_DIM), lambda bi, i: (i, 0)),
            pl.BlockSpec((1, d_model), const2),
            pl.BlockSpec((d_model, in_w), const2),
            pl.BlockSpec((1, HEAD_DIM), const2),
            pl.BlockSpec((1, HEAD_DIM), const2),
            pl.BlockSpec((N_FOURIER_GROUPS, FOURIER_GROUP_W, 2 * FOURIER_GROUP_W), lambda bi, i: (0, 0, 0)),
        ],
        out_specs=[
            pl.BlockSpec((1, t, ATTN_W), lambda bi, i: (bi, i, 0)),
            pl.BlockSpec((1, t, KV_W), lambda bi, i: (bi, i, 0)),
            pl.BlockSpec((1, t, 2 * KV_W), lambda bi, i: (bi, i, 0)),
            pl.BlockSpec((1, t, FOURIER_W), lambda bi, i: (bi, i, 0)),
            pl.BlockSpec((1, t, FOURIER_W), lambda bi, i: (bi, i, 0)),
        ],
        out_shape=[
            jax.ShapeDtypeStruct((b, n_rows, ATTN_W), BF16),
            jax.ShapeDtypeStruct((b, n_rows, KV_W), BF16),
            jax.ShapeDtypeStruct((b, n_rows, 2 * KV_W), BF16),
            jax.ShapeDtypeStruct((b, l, FOURIER_W), BF16),
            jax.ShapeDtypeStruct((b, l, FOURIER_W), BF16),
        ],
        compiler_params=pltpu.CompilerParams(
            dimension_semantics=("arbitrary", "arbitrary"), vmem_limit_bytes=_vmem_limit(est)),
    )(x, meta_pad, cos_tab, sin_tab, g_mix, w_in, g_q, g_k, ab)


_NT = (((1,), (1,)), ((), ()))


def _attention_bounded_kernel(q_ref, qn_ref, k_ref, v_ref, km_ref, vm_ref, o_ref, q2_ref, p0_ref, p1_ref, acc_ref,
                              *, n_chunks, chunk):
    d = HEAD_DIM
    tq = q_ref.shape[1]
    first_tile = pl.program_id(2) == 0
    q2_ref[0, :tq] = q_ref[0, :, :d]
    q2_ref[0, tq:] = q_ref[0, :, d:]
    q2_ref[1, :tq] = qn_ref[0, :, :d]
    q2_ref[1, tq:] = qn_ref[0, :, d:]

    def probs(tile, c, p_ref):
        start = pl.multiple_of(c * chunk, chunk)
        s = lax.dot_general(q2_ref[tile], k_ref[0, pl.ds(start, chunk), :], _NT, preferred_element_type=F32)
        p_ref[...] = jnp.exp2(s).astype(BF16)

    def accumulate(c, p_ref):
        start = pl.multiple_of(c * chunk, chunk)
        acc_ref[...] += jnp.dot(p_ref[...], v_ref[0, pl.ds(start, chunk), :], preferred_element_type=F32)

    s_meta = lax.dot_general(q2_ref[0], km_ref[0], _NT, preferred_element_type=F32)
    acc_ref[...] = jnp.dot(jnp.exp2(s_meta).astype(BF16), vm_ref[0], preferred_element_type=F32)

    @pl.when(first_tile)
    def _():
        probs(0, 0, p0_ref)

    bufs = (p0_ref, p1_ref)
    per_trip = ATTN_CHUNKS_PER_TRIP if n_chunks % ATTN_CHUNKS_PER_TRIP == 0 else 2

    def body(j, carry):
        c = per_trip * j
        for u in range(per_trip):
            nxt = c + u + 1
            if u + 1 < per_trip:
                probs(0, nxt, bufs[(u + 1) % 2])
            else:
                wraps = nxt >= n_chunks
                probs(wraps.astype(jnp.int32), jnp.where(wraps, 0, nxt), bufs[(u + 1) % 2])
            accumulate(c + u, bufs[u % 2])
        return carry

    lax.fori_loop(0, n_chunks // per_trip, body, 0)

    acc = acc_ref[...]
    out = acc[:, :d] / acc[:, d:]
    o_ref[0, :, :d] = out[:tq]
    o_ref[0, :, d:] = out[tq:]


def _attention_kernel(q_ref, k_ref, v_ref, km_ref, vm_ref, o_ref, q2_ref, s0_ref, s1_ref, m_ref, acc_ref,
                      *, n_chunks, chunk):
    d = HEAD_DIM
    tq = q_ref.shape[1]
    q2_ref[:tq] = q_ref[0, :, :d]
    q2_ref[tq:] = q_ref[0, :, d:]

    def scores(c, s_ref):
        start = pl.multiple_of(c * chunk, chunk)
        s_ref[...] = lax.dot_general(
            q2_ref[...], k_ref[0, pl.ds(start, chunk), :], _NT, preferred_element_type=F32)

    def update(c, s_ref):
        start = pl.multiple_of(c * chunk, chunk)
        s = s_ref[...]
        m_old = m_ref[...]
        m_new = jnp.maximum(m_old, jnp.max(s, axis=-1, keepdims=True))
        p = jnp.exp2(s - m_new).astype(BF16)
        pv = jnp.dot(p, v_ref[0, pl.ds(start, chunk), :], preferred_element_type=F32)
        acc_ref[...] = acc_ref[...] * jnp.exp2(m_old - m_new) + pv
        m_ref[...] = m_new

    s_meta = lax.dot_general(q2_ref[...], km_ref[0], _NT, preferred_element_type=F32)
    m_meta = jnp.max(s_meta, axis=-1, keepdims=True)
    m_ref[...] = m_meta
    acc_ref[...] = jnp.dot(jnp.exp2(s_meta - m_meta).astype(BF16), vm_ref[0], preferred_element_type=F32)

    scores(0, s0_ref)

    def body(j, carry):
        c = 2 * j
        scores(c + 1, s1_ref)
        update(c, s0_ref)
        scores(jnp.minimum(c + 2, n_chunks - 1), s0_ref)
        update(c + 1, s1_ref)
        return carry

    lax.fori_loop(0, n_chunks // 2, body, 0)

    acc = acc_ref[...]
    out = acc[:, :d] / acc[:, d:]
    o_ref[0, :, :d] = out[:tq]
    o_ref[0, :, d:] = out[tq:]


def _attention_call(q, k, v, n_tok, bounded):
    b = q.shape[0]
    tq = min(ATTN_Q_TILE, n_tok)
    chunk = min(ATTN_KV_CHUNK, n_tok)
    d = HEAD_DIM
    gq = Q_GROUP * d
    m = Q_GROUP * tq
    meta_blk = n_tok // N_META
    assert (n_tok // chunk) % 2 == 0
    est = 2 * (tq * gq * 2 + n_tok * d * 2 + n_tok * 2 * d * 2 + tq * gq * 4) + m * (
        chunk * 4 * 3 + chunk * 2 + 2 * d * 4 * 3)
    n_tiles = n_tok // tq
    q_spec = pl.BlockSpec((1, tq, gq), lambda bi, g, i: (bi, i, g))
    if bounded:
        body = _attention_bounded_kernel
        scratch = [pltpu.VMEM((2, m, d), BF16), pltpu.VMEM((m, chunk), BF16), pltpu.VMEM((m, chunk), BF16),
                   pltpu.VMEM((m, 2 * d), F32)]
        q_specs = [q_spec, pl.BlockSpec((1, tq, gq), lambda bi, g, i: (bi, jnp.minimum(i + 1, n_tiles - 1), g))]
        q_args = (q, q)
    else:
        body = _attention_kernel
        scratch = [pltpu.VMEM((m, d), BF16), pltpu.VMEM((m, chunk), F32), pltpu.VMEM((m, chunk), F32),
                   pltpu.VMEM((m, 1), F32), pltpu.VMEM((m, 2 * d), F32)]
        q_specs = [q_spec]
        q_args = (q,)
    return pl.pallas_call(
        functools.partial(body, n_chunks=n_tok // chunk, chunk=chunk),
        grid=(b, N_KV_HEADS, n_tiles),
        in_specs=q_specs + [
            pl.BlockSpec((1, n_tok, d), lambda bi, g, i: (bi, 0, g)),
            pl.BlockSpec((1, n_tok, 2 * d), lambda bi, g, i: (bi, 0, g)),
            pl.BlockSpec((1, N_META, d), lambda bi, g, i: (bi, meta_blk, g)),
            pl.BlockSpec((1, N_META, 2 * d), lambda bi, g, i: (bi, meta_blk, g)),
        ],
        out_specs=pl.BlockSpec((1, tq, gq), lambda bi, g, i: (bi, i, g)),
        out_shape=jax.ShapeDtypeStruct((b, n_tok, ATTN_W), F32),
        scratch_shapes=scratch,
        compiler_params=pltpu.CompilerParams(
            dimension_semantics=("arbitrary", "arbitrary", "arbitrary"), vmem_limit_bytes=_vmem_limit(est)),
    )(*q_args, k, v, k, v)


def _attention(q, k, v, g_q, g_k, n_tok):
    score_bound = (math.log2(math.e) / math.sqrt(HEAD_DIM)) * HEAD_DIM * BOUND_SLACK * (
        jnp.max(jnp.abs(g_q)) * jnp.max(jnp.abs(g_k)))
    return lax.cond(
        score_bound <= MAX_UNSHIFTED_SCORE,
        lambda: _attention_call(q, k, v, n_tok, True),
        lambda: _attention_call(q, k, v, n_tok, False))


def _dft1_kernel(f_ref, t_ref, p_ref, q_ref, g_ref, *, n2, cols):
    step = pl.program_id(0)
    n1 = t_ref.shape[2]
    n_sub = t_ref.shape[3]
    f_top = f_ref[:n1, :]
    f_bot = f_ref[n1:, :]
    for j in range(n_sub):
        z = jnp.concatenate(
            [p_ref[0, :, j * cols:(j + 1) * cols], q_ref[0, :, j * cols:(j + 1) * cols]], axis=0)
        valid = step * n_sub + j < n2
        z = jnp.where(valid, z, jnp.zeros_like(z))
        tr = t_ref[0, 0, :, j:j + 1]
        ti = t_ref[0, 1, :, j:j + 1]
        m = jnp.concatenate([tr * f_top - ti * f_bot, ti * f_top + tr * f_bot], axis=0).astype(BF16)
        g_ref[0, j] = jnp.dot(m, z, preferred_element_type=F32).astype(BF16)


def _dft2_kernel(f_ref, g_ref, y_ref, *, n2):
    f = f_ref[...].astype(BF16)
    for kk in range(g_ref.shape[1]):
        res = jnp.dot(f, g_ref[0, kk], preferred_element_type=F32)
        y_ref[0, kk] = res[:n2]


def _sequence_dft(p, q, l):
    b = p.shape[0]
    cols = p.shape[2]
    n1, n2 = _dft_factors(l)
    tn = DFT_N2_TILE
    n2_pad = -(-n2 // tn) * tn
    stage1, twiddle, stage2 = _dft_stage_tables(l, n1, n2, n2_pad)
    twiddle = twiddle.reshape(2, n1, n2_pad // tn, tn).transpose(2, 0, 1, 3)
    pv = p.reshape(b, n1, n2 * cols)
    qv = q.reshape(b, n1, n2 * cols)
    est1 = 2 * (4 * n1 * n1 * 4 + 2 * n1 * tn * cols * 2 + tn * 2 * n1 * cols * 4) + 8 * n1 * cols * 4
    g = pl.pallas_call(
        functools.partial(_dft1_kernel, n2=n2, cols=cols),
        grid=(n2_pad // tn, b),
        in_specs=[
            pl.BlockSpec((2 * n1, 2 * n1), lambda i, bi: (0, 0)),
            pl.BlockSpec((1, 2, n1, tn), lambda i, bi: (i, 0, 0, 0)),
            pl.BlockSpec((1, n1, tn * cols), lambda i, bi: (bi, 0, i)),
            pl.BlockSpec((1, n1, tn * cols), lambda i, bi: (bi, 0, i)),
        ],
        out_specs=pl.BlockSpec((1, tn, 2 * n1, cols), lambda i, bi: (bi, i, 0, 0)),
        out_shape=jax.ShapeDtypeStruct((b, n2_pad, 2 * n1, cols), BF16),
        compiler_params=pltpu.CompilerParams(
            dimension_semantics=("arbitrary", "arbitrary"), vmem_limit_bytes=_vmem_limit(est1)),
    )(jnp.asarray(stage1), jnp.asarray(twiddle), pv, qv)

    gv = g.reshape(b, 2 * n2_pad, n1, cols).transpose(0, 2, 1, 3)
    tk = DFT_K1_TILE
    assert n1 % tk == 0
    est2 = 2 * (2 * n2_pad * n2_pad * 4 + tk * 2 * n2_pad * cols * 2 + tk * n2 * cols * 4) + 3 * n2_pad * cols * 4
    y = pl.pallas_call(
        functools.partial(_dft2_kernel, n2=n2),
        grid=(b, n1 // tk),
        in_specs=[
            pl.BlockSpec((n2_pad, 2 * n2_pad), lambda bi, i: (0, 0)),
            pl.BlockSpec((1, tk, 2 * n2_pad, cols), lambda bi, i: (bi, i, 0, 0)),
        ],
        out_specs=pl.BlockSpec((1, tk, n2, cols), lambda bi, i: (bi, i, 0, 0)),
        out_shape=jax.ShapeDtypeStruct((b, n1, n2, cols), F32),
        compiler_params=pltpu.CompilerParams(
            dimension_semantics=("arbitrary", "arbitrary"), vmem_limit_bytes=_vmem_limit(est2)),
    )(jnp.asarray(stage2), gv)
    return y.transpose(0, 2, 1, 3).reshape(b, l, cols)


def _out_mlp_kernel(
    x_ref, a_ref, f_ref, ga_ref, gf_ref, wout_ref, gmlp_ref, wup_ref, wdown_ref, gfin_ref, o_ref,
):
    an = _rms(a_ref[0], ga_ref[...]).astype(BF16)
    fn = _rms(f_ref[0], gf_ref[...]).astype(BF16)
    h = x_ref[0]
    h = h + jnp.dot(an, wout_ref[:ATTN_W, :], preferred_element_type=F32)
    h = h + jnp.dot(fn, wout_ref[ATTN_W:, :], preferred_element_type=F32)
    m = _rms(h, gmlp_ref[...]).astype(BF16)
    act = jnp.maximum(jnp.dot(m, wup_ref[...], preferred_element_type=F32), 0.0)
    out = h + jnp.dot((act * act).astype(BF16), wdown_ref[...], preferred_element_type=F32)
    o_ref[0] = _rms(out, gfin_ref[...])


def _out_mlp(x, attn, four, g_attn_out, g_fourier_out, w_out, g_mlp, w_up, w_down, g_final):
    b, n_tok, d_model = x.shape
    d_ff = w_up.shape[1]
    t = TOKEN_TILE
    const2 = lambda bi, i: (0, 0)
    tile3 = lambda bi, i: (bi, i, 0)
    est = (2 * (2 * t * d_model * 4 + t * ATTN_W * 4 + t * FOURIER_W * 4)
           + 2 * (d_model * d_model + 2 * d_model * d_ff) * 2 + t * (4 * d_model * 4 + d_ff * 6))
    return pl.pallas_call(
        _out_mlp_kernel,
        grid=(b, n_tok // t),
        in_specs=[
            pl.BlockSpec((1, t, d_model), tile3),
            pl.BlockSpec((1, t, ATTN_W), tile3),
            pl.BlockSpec((1, t, FOURIER_W), tile3),
            pl.BlockSpec((1, ATTN_W), const2),
            pl.BlockSpec((1, FOURIER_W), const2),
            pl.BlockSpec((d_model, d_model), const2),
            pl.BlockSpec((1, d_model), const2),
            pl.BlockSpec((d_model, d_ff), const2),
            pl.BlockSpec((d_ff, d_model), const2),
            pl.BlockSpec((1, d_model), const2),
        ],
        out_specs=pl.BlockSpec((1, t, d_model), tile3),
        out_shape=jax.ShapeDtypeStruct((b, n_tok, d_model), F32),
        compiler_params=pltpu.CompilerParams(
            dimension_semantics=("arbitrary", "arbitrary"), vmem_limit_bytes=_vmem_limit(est)),
    )(x, attn, four, g_attn_out, g_fourier_out, w_out, g_mlp, w_up, w_down, g_final)


def kernel(x, meta_tokens, g_mix, w_in, g_q, g_k, w_fourier, g_attn_out, g_fourier_out, w_out,
           g_mlp, w_up, w_down, g_final):
    assert g_mix.shape[0] == 1, "meta-token rows are only carried for a single layer"
    b, n_tok, d_model = x.shape
    l = n_tok + N_META
    t = min(IN_PROJ_TILE, n_tok)
    assert n_tok % t == 0 and n_tok % TOKEN_TILE == 0 and n_tok % GRID_W == 0

    cos_tab, sin_tab = _rope_tables(n_tok, n_tok + t)
    meta_pad = jnp.pad(meta_tokens.astype(F32), ((0, t - N_META), (0, 0)))
    ab = _fourier_weights(w_fourier[0], l)
    q, k, v, p, qq = _in_proj(
        x, meta_pad, cos_tab, sin_tab, g_mix, w_in[0].astype(BF16), g_q, g_k, ab, l)
    attn = _attention(q, k, v, g_q, g_k, n_tok)
    four = _sequence_dft(p, qq, l)
    return _out_mlp(
        x, attn, four, g_attn_out, g_fourier_out, w_out[0].astype(BF16), g_mlp,
        w_up[0].astype(BF16), w_down[0].astype(BF16), g_final[None, :])
```

```python
import functools
import math

import jax
import jax.numpy as jnp
import numpy as np
from jax import lax
from jax.experimental import pallas as pl
from jax.experimental.pallas import tpu as pltpu

N_META = 16
GRID_W = 64
HEAD_DIM = 128
N_Q_HEADS = 4
N_KV_HEADS = 2
Q_GROUP = N_Q_HEADS // N_KV_HEADS
ATTN_W = N_Q_HEADS * HEAD_DIM
KV_W = N_KV_HEADS * HEAD_DIM
N_FOURIER_GROUPS = 4
FOURIER_GROUP_W = 128
FOURIER_W = N_FOURIER_GROUPS * FOURIER_GROUP_W
ROPE_THETA = 10000.0
ROPE_AXIS_DIM = HEAD_DIM // 2
RMS_EPS = 1e-6

V7X_VMEM_BYTES = 64 * 1024 * 1024
V7X_BF16_SUBLANES = 16
VMEM_RESERVE_BYTES = 8 * 1024 * 1024
VMEM_MIN_LIMIT_BYTES = 32 * 1024 * 1024

TOKEN_TILE = 512
IN_PROJ_TILE = 1024
IN_PROJ_ROWS = 256
ATTN_Q_TILE = 1024
ATTN_KV_CHUNK = 1024
ATTN_CHUNKS_PER_TRIP = 8
DFT_N2_TILE = 52
DFT_K1_TILE = 16
MAX_UNSHIFTED_SCORE = 64.0
BOUND_SLACK = 1.02

F32 = jnp.float32
BF16 = jnp.bfloat16


def _vmem_limit(estimate_bytes):
    return int(min(max(2 * estimate_bytes, VMEM_MIN_LIMIT_BYTES), V7X_VMEM_BYTES - VMEM_RESERVE_BYTES))


def _rms(x, g):
    return x * lax.rsqrt(jnp.mean(x * x, axis=-1, keepdims=True) + RMS_EPS) * g


def _dft_factors(l):
    best = None
    for n1 in range(V7X_BF16_SUBLANES, l, V7X_BF16_SUBLANES):
        if l % n1 == 0:
            n2 = l // n1
            if best is None or abs(n1 - n2) < abs(best[0] - best[1]):
                best = (n1, n2)
    assert best is not None
    return best


def _rope_tables(n_tok, n_rows):
    rows_count = n_tok // GRID_W
    real_row = np.repeat(np.arange(rows_count, dtype=np.float64), GRID_W)
    real_col = np.tile(np.arange(GRID_W, dtype=np.float64), rows_count)
    meta_row = np.full((N_META,), -1.0)
    meta_col = np.arange(N_META, dtype=np.float64)
    pad = np.zeros((n_rows - n_tok - N_META,))
    row = np.concatenate([real_row, meta_row, pad])
    col = np.concatenate([real_col, meta_col, pad])
    inv_freq = ROPE_THETA ** (-np.arange(0, ROPE_AXIS_DIM, 2, dtype=np.float64) / ROPE_AXIS_DIM)
    ang_r = row[:, None] * inv_freq[None, :]
    ang_c = col[:, None] * inv_freq[None, :]
    cos = np.concatenate([np.cos(ang_r)] * 2 + [np.cos(ang_c)] * 2, axis=-1)
    sin = np.concatenate([-np.sin(ang_r), np.sin(ang_r), -np.sin(ang_c), np.sin(ang_c)], axis=-1)
    return jnp.asarray(cos.astype(np.float32)), jnp.asarray(sin.astype(np.float32))


def _channel_dft_tables(l):
    n = np.arange(FOURIER_GROUP_W)
    ang = 2.0 * np.pi * ((n[:, None] * n[None, :]) % FOURIER_GROUP_W) / FOURIER_GROUP_W
    scale = 1.0 / math.sqrt(l * FOURIER_GROUP_W)
    return (np.cos(ang) * scale).astype(np.float32), (np.sin(ang) * scale).astype(np.float32)


def _dft_stage_tables(l, n1, n2, n2_pad):
    s = N_META
    k1 = np.arange(n1)[:, None]
    ang1 = 2.0 * np.pi * (((k1 + s) * n2 * np.arange(n1)[None, :]) % l) / l
    fr, fi = np.cos(ang1), np.sin(ang1)
    stage1 = np.concatenate(
        [np.concatenate([fr, -fi], axis=1), np.concatenate([fi, fr], axis=1)], axis=0).astype(np.float32)
    ang_t = 2.0 * np.pi * (((k1 + s) * (np.arange(n2)[None, :] + s)) % l) / l
    twiddle = np.zeros((2, n1, n2_pad), np.float32)
    twiddle[0, :, :n2] = np.cos(ang_t)
    twiddle[1, :, :n2] = np.sin(ang_t)
    k2 = np.arange(n2)[:, None]
    jj = np.arange(n2)[None, :]
    ang2 = 2.0 * np.pi * ((k2 * (jj + s)) % n2) / n2
    stage2 = np.zeros((n2_pad, n2_pad, 2), np.float32)
    stage2[:n2, :n2, 0] = np.cos(ang2)
    stage2[:n2, :n2, 1] = -np.sin(ang2)
    return stage1, twiddle, stage2.reshape(n2_pad, 2 * n2_pad)


def _fourier_weight_kernel(c_ref, s_ref, w_ref, ab_ref):
    w = w_ref[0]
    a = jnp.dot(c_ref[...], w, preferred_element_type=F32, precision=lax.Precision.HIGHEST)
    b = jnp.dot(s_ref[...], w, preferred_element_type=F32, precision=lax.Precision.HIGHEST)
    ab_ref[0, :, :FOURIER_GROUP_W] = a.astype(BF16)
    ab_ref[0, :, FOURIER_GROUP_W:] = b.astype(BF16)


def _fourier_weights(w_f, l):
    c, s = _channel_dft_tables(l)
    gw = FOURIER_GROUP_W
    return pl.pallas_call(
        _fourier_weight_kernel,
        grid=(N_FOURIER_GROUPS,),
        in_specs=[
            pl.BlockSpec((gw, gw), lambda g: (0, 0)),
            pl.BlockSpec((gw, gw), lambda g: (0, 0)),
            pl.BlockSpec((1, gw, gw), lambda g: (g, 0, 0)),
        ],
        out_specs=pl.BlockSpec((1, gw, 2 * gw), lambda g: (g, 0, 0)),
        out_shape=jax.ShapeDtypeStruct((N_FOURIER_GROUPS, gw, 2 * gw), BF16),
    )(jnp.asarray(c), jnp.asarray(s), w_f)


def _rope(x, cos, sin, lower_half):
    partner = jnp.where(lower_half, pltpu.roll(x, HEAD_DIM - 32, axis=1), pltpu.roll(x, 32, axis=1))
    return x * cos + partner * sin


def _in_proj_kernel(
    x_ref, meta_ref, cos_ref, sin_ref, gmix_ref, win_ref, gq_ref, gk_ref, ab_ref,
    q_ref, k_ref, v_ref, p_ref, qq_ref, *, n_real_tiles, q_scale,
):
    i = pl.program_id(1)
    is_meta = i == n_real_tiles
    d = HEAD_DIM
    gw = FOURIER_GROUP_W
    lane = lax.broadcasted_iota(jnp.int32, (IN_PROJ_ROWS, d), 1)
    lower_half = (lane % ROPE_AXIS_DIM) < (ROPE_AXIS_DIM // 2)
    for r in range(x_ref.shape[1] // IN_PROJ_ROWS):
        rows = pl.ds(r * IN_PROJ_ROWS, IN_PROJ_ROWS)
        xt = jnp.where(is_meta, meta_ref[rows, :], x_ref[0, rows, :])
        hn = _rms(xt, gmix_ref[...]).astype(BF16)
        proj = jnp.dot(hn, win_ref[...], preferred_element_type=F32)
        cos = cos_ref[rows, :]
        sin = sin_ref[rows, :]
        for h in range(N_Q_HEADS):
            qh = _rope(_rms(proj[:, h * d:(h + 1) * d], gq_ref[...]), cos, sin, lower_half)
            q_ref[0, rows, h * d:(h + 1) * d] = (qh * q_scale).astype(BF16)
        for h in range(N_KV_HEADS):
            off = ATTN_W + h * d
            kh = _rope(_rms(proj[:, off:off + d], gk_ref[...]), cos, sin, lower_half)
            k_ref[0, rows, h * d:(h + 1) * d] = kh.astype(BF16)
            off = ATTN_W + KV_W + h * d
            v_ref[0, rows, 2 * h * d:(2 * h + 1) * d] = proj[:, off:off + d].astype(BF16)
            v_ref[0, rows, (2 * h + 1) * d:(2 * h + 2) * d] = jnp.ones((IN_PROJ_ROWS, d), BF16)
        for g in range(N_FOURIER_GROUPS):
            off = ATTN_W + 2 * KV_W + g * gw
            pq = jnp.dot(proj[:, off:off + gw].astype(BF16), ab_ref[g], preferred_element_type=F32)
            p_ref[0, rows, g * gw:(g + 1) * gw] = pq[:, :gw].astype(BF16)
            qq_ref[0, rows, g * gw:(g + 1) * gw] = pq[:, gw:].astype(BF16)


def _in_proj(x, meta_pad, cos_tab, sin_tab, g_mix, w_in, g_q, g_k, ab, l):
    b, n_tok, d_model = x.shape
    t = meta_pad.shape[0]
    n_real_tiles = n_tok // t
    n_rows = (n_real_tiles + 1) * t
    in_w = w_in.shape[1]
    q_scale = math.log2(math.e) / math.sqrt(HEAD_DIM)
    const2 = lambda bi, i: (0, 0)
    est = (2 * t * d_model * 4 * 2 + d_model * in_w * 2 * 2 + t * in_w * 4 * 3
           + 2 * t * (2 * ATTN_W + KV_W + 2 * FOURIER_W) * 2)
    return pl.pallas_call(
        functools.partial(_in_proj_kernel, n_real_tiles=n_real_tiles, q_scale=q_scale),
        grid=(b, n_real_tiles + 1),
        in_specs=[
            pl.BlockSpec((1, t, d_model), lambda bi, i: (bi, jnp.minimum(i, n_real_tiles - 1), 0)),
            pl.BlockSpec((t, d_model), const2),
            pl.BlockSpec((t, HEAD_DIM), lambda bi, i: (i, 0)),
            pl.BlockSpec((t, HEAD_DIM), lambda bi, i: (i, 0)),
            pl.BlockSpec((1, d_model), const2),
            pl.BlockSpec((d_model, in_w), const2),
            pl.BlockSpec((1, HEAD_DIM), const2),
            pl.BlockSpec((1, HEAD_DIM), const2),
            pl.BlockSpec((N_FOURIER_GROUPS, FOURIER_GROUP_W, 2 * FOURIER_GROUP_W), lambda bi, i: (0, 0, 0)),
        ],
        out_specs=[
            pl.BlockSpec((1, t, ATTN_W), lambda bi, i: (bi, i, 0)),
            pl.BlockSpec((1, t, KV_W), lambda bi, i: (bi, i, 0)),
            pl.BlockSpec((1, t, 2 * KV_W), lambda bi, i: (bi, i, 0)),
            pl.BlockSpec((1, t, FOURIER_W), lambda bi, i: (bi, i, 0)),
            pl.BlockSpec((1, t, FOURIER_W), lambda bi, i: (bi, i, 0)),
        ],
        out_shape=[
            jax.ShapeDtypeStruct((b, n_rows, ATTN_W), BF16),
            jax.ShapeDtypeStruct((b, n_rows, KV_W), BF16),
            jax.ShapeDtypeStruct((b, n_rows, 2 * KV_W), BF16),
            jax.ShapeDtypeStruct((b, l, FOURIER_W), BF16),
            jax.ShapeDtypeStruct((b, l, FOURIER_W), BF16),
        ],
        compiler_params=pltpu.CompilerParams(
            dimension_semantics=("arbitrary", "arbitrary"), vmem_limit_bytes=_vmem_limit(est)),
    )(x, meta_pad, cos_tab, sin_tab, g_mix, w_in, g_q, g_k, ab)


_NT = (((1,), (1,)), ((), ()))


def _attention_bounded_kernel(q_ref, qn_ref, k_ref, v_ref, km_ref, vm_ref, o_ref, q2_ref, p0_ref, p1_ref, acc_ref,
                              *, n_chunks, chunk):
    d = HEAD_DIM
    tq = q_ref.shape[1]
    first_tile = pl.program_id(2) == 0
    q2_ref[0, :tq] = q_ref[0, :, :d]
    q2_ref[0, tq:] = q_ref[0, :, d:]
    q2_ref[1, :tq] = qn_ref[0, :, :d]
    q2_ref[1, tq:] = qn_ref[0, :, d:]

    def probs(tile, c, p_ref):
        start = pl.multiple_of(c * chunk, chunk)
        s = lax.dot_general(q2_ref[tile], k_ref[0, pl.ds(start, chunk), :], _NT, preferred_element_type=F32)
        p_ref[...] = jnp.exp2(s).astype(BF16)

    def accumulate(c, p_ref):
        start = pl.multiple_of(c * chunk, chunk)
        acc_ref[...] += jnp.dot(p_ref[...], v_ref[0, pl.ds(start, chunk), :], preferred_element_type=F32)

    s_meta = lax.dot_general(q2_ref[0], km_ref[0], _NT, preferred_element_type=F32)
    acc_ref[...] = jnp.dot(jnp.exp2(s_meta).astype(BF16), vm_ref[0], preferred_element_type=F32)

    @pl.when(first_tile)
    def _():
        probs(0, 0, p0_ref)

    bufs = (p0_ref, p1_ref)
    per_trip = ATTN_CHUNKS_PER_TRIP if n_chunks % ATTN_CHUNKS_PER_TRIP == 0 else 2

    def body(j, carry):
        c = per_trip * j
        for u in range(per_trip):
            nxt = c + u + 1
            if u + 1 < per_trip:
                probs(0, nxt, bufs[(u + 1) % 2])
            else:
                wraps = nxt >= n_chunks
                probs(wraps.astype(jnp.int32), jnp.where(wraps, 0, nxt), bufs[(u + 1) % 2])
            accumulate(c + u, bufs[u % 2])
        return carry

    lax.fori_loop(0, n_chunks // per_trip, body, 0)

    acc = acc_ref[...]
    out = acc[:, :d] / acc[:, d:]
    o_ref[0, :, :d] = out[:tq]
    o_ref[0, :, d:] = out[tq:]


def _attention_kernel(q_ref, k_ref, v_ref, km_ref, vm_ref, o_ref, q2_ref, s0_ref, s1_ref, m_ref, acc_ref,
                      *, n_chunks, chunk):
    d = HEAD_DIM
    tq = q_ref.shape[1]
    q2_ref[:tq] = q_ref[0, :, :d]
    q2_ref[tq:] = q_ref[0, :, d:]

    def scores(c, s_ref):
        start = pl.multiple_of(c * chunk, chunk)
        s_ref[...] = lax.dot_general(
            q2_ref[...], k_ref[0, pl.ds(start, chunk), :], _NT, preferred_element_type=F32)

    def update(c, s_ref):
        start = pl.multiple_of(c * chunk, chunk)
        s = s_ref[...]
        m_old = m_ref[...]
        m_new = jnp.maximum(m_old, jnp.max(s, axis=-1, keepdims=True))
        p = jnp.exp2(s - m_new).astype(BF16)
        pv = jnp.dot(p, v_ref[0, pl.ds(start, chunk), :], preferred_element_type=F32)
        acc_ref[...] = acc_ref[...] * jnp.exp2(m_old - m_new) + pv
        m_ref[...] = m_new

    s_meta = lax.dot_general(q2_ref[...], km_ref[0], _NT, preferred_element_type=F32)
    m_meta = jnp.max(s_meta, axis=-1, keepdims=True)
    m_ref[...] = m_meta
    acc_ref[...] = jnp.dot(jnp.exp2(s_meta - m_meta).astype(BF16), vm_ref[0], preferred_element_type=F32)

    scores(0, s0_ref)

    def body(j, carry):
        c = 2 * j
        scores(c + 1, s1_ref)
        update(c, s0_ref)
        scores(jnp.minimum(c + 2, n_chunks - 1), s0_ref)
        update(c + 1, s1_ref)
        return carry

    lax.fori_loop(0, n_chunks // 2, body, 0)

    acc = acc_ref[...]
    out = acc[:, :d] / acc[:, d:]
    o_ref[0, :, :d] = out[:tq]
    o_ref[0, :, d:] = out[tq:]


def _attention_call(q, k, v, n_tok, bounded):
    b = q.shape[0]
    tq = min(ATTN_Q_TILE, n_tok)
    chunk = min(ATTN_KV_CHUNK, n_tok)
    d = HEAD_DIM
    gq = Q_GROUP * d
    m = Q_GROUP * tq
    meta_blk = n_tok // N_META
    assert (n_tok // chunk) % 2 == 0
    est = 2 * (tq * gq * 2 + n_tok * d * 2 + n_tok * 2 * d * 2 + tq * gq * 4) + m * (
        chunk * 4 * 3 + chunk * 2 + 2 * d * 4 * 3)
    n_tiles = n_tok // tq
    q_spec = pl.BlockSpec((1, tq, gq), lambda bi, g, i: (bi, i, g))
    if bounded:
        body = _attention_bounded_kernel
        scratch = [pltpu.VMEM((2, m, d), BF16), pltpu.VMEM((m, chunk), BF16), pltpu.VMEM((m, chunk), BF16),
                   pltpu.VMEM((m, 2 * d), F32)]
        q_specs = [q_spec, pl.BlockSpec((1, tq, gq), lambda bi, g, i: (bi, jnp.minimum(i + 1, n_tiles - 1), g))]
        q_args = (q, q)
    else:
        body = _attention_kernel
        scratch = [pltpu.VMEM((m, d), BF16), pltpu.VMEM((m, chunk), F32), pltpu.VMEM((m, chunk), F32),
                   pltpu.VMEM((m, 1), F32), pltpu.VMEM((m, 2 * d), F32)]
        q_specs = [q_spec]
        q_args = (q,)
    return pl.pallas_call(
        functools.partial(body, n_chunks=n_tok // chunk, chunk=chunk),
        grid=(b, N_KV_HEADS, n_tiles),
        in_specs=q_specs + [
            pl.BlockSpec((1, n_tok, d), lambda bi, g, i: (bi, 0, g)),
            pl.BlockSpec((1, n_tok, 2 * d), lambda bi, g, i: (bi, 0, g)),
            pl.BlockSpec((1, N_META, d), lambda bi, g, i: (bi, meta_blk, g)),
            pl.BlockSpec((1, N_META, 2 * d), lambda bi, g, i: (bi, meta_blk, g)),
        ],
        out_specs=pl.BlockSpec((1, tq, gq), lambda bi, g, i: (bi, i, g)),
        out_shape=jax.ShapeDtypeStruct((b, n_tok, ATTN_W), F32),
        scratch_shapes=scratch,
        compiler_params=pltpu.CompilerParams(
            dimension_semantics=("arbitrary", "arbitrary", "arbitrary"), vmem_limit_bytes=_vmem_limit(est)),
    )(*q_args, k, v, k, v)


def _attention(q, k, v, g_q, g_k, n_tok):
    score_bound = (math.log2(math.e) / math.sqrt(HEAD_DIM)) * HEAD_DIM * BOUND_SLACK * (
        jnp.max(jnp.abs(g_q)) * jnp.max(jnp.abs(g_k)))
    return lax.cond(
        score_bound <= MAX_UNSHIFTED_SCORE,
        lambda: _attention_call(q, k, v, n_tok, True),
        lambda: _attention_call(q, k, v, n_tok, False))


def _dft1_kernel(f_ref, t_ref, p_ref, q_ref, g_ref, *, n2, cols):
    step = pl.program_id(0)
    n1 = t_ref.shape[2]
    n_sub = t_ref.shape[3]
    f_top = f_ref[:n1, :]
    f_bot = f_ref[n1:, :]
    for j in range(n_sub):
        z = jnp.concatenate(
            [p_ref[0, :, j * cols:(j + 1) * cols], q_ref[0, :, j * cols:(j + 1) * cols]], axis=0)
        valid = step * n_sub + j < n2
        z = jnp.where(valid, z, jnp.zeros_like(z))
        tr = t_ref[0, 0, :, j:j + 1]
        ti = t_ref[0, 1, :, j:j + 1]
        m = jnp.concatenate([tr * f_top - ti * f_bot, ti * f_top + tr * f_bot], axis=0).astype(BF16)
        g_ref[0, j] = jnp.dot(m, z, preferred_element_type=F32).astype(BF16)


def _dft2_kernel(f_ref, g_ref, y_ref, *, n2):
    f = f_ref[...].astype(BF16)
    for kk in range(g_ref.shape[1]):
        res = jnp.dot(f, g_ref[0, kk], preferred_element_type=F32)
        y_ref[0, kk] = res[:n2].astype(BF16)


def _sequence_dft(p, q, l):
    b = p.shape[0]
    cols = p.shape[2]
    n1, n2 = _dft_factors(l)
    tn = DFT_N2_TILE
    n2_pad = -(-n2 // tn) * tn
    stage1, twiddle, stage2 = _dft_stage_tables(l, n1, n2, n2_pad)
    twiddle = twiddle.reshape(2, n1, n2_pad // tn, tn).transpose(2, 0, 1, 3)
    pv = p.reshape(b, n1, n2 * cols)
    qv = q.reshape(b, n1, n2 * cols)
    est1 = 2 * (4 * n1 * n1 * 4 + 2 * n1 * tn * cols * 2 + tn * 2 * n1 * cols * 4) + 8 * n1 * cols * 4
    g = pl.pallas_call(
        functools.partial(_dft1_kernel, n2=n2, cols=cols),
        grid=(n2_pad // tn, b),
        in_specs=[
            pl.BlockSpec((2 * n1, 2 * n1), lambda i, bi: (0, 0)),
            pl.BlockSpec((1, 2, n1, tn), lambda i, bi: (i, 0, 0, 0)),
            pl.BlockSpec((1, n1, tn * cols), lambda i, bi: (bi, 0, i)),
            pl.BlockSpec((1, n1, tn * cols), lambda i, bi: (bi, 0, i)),
        ],
        out_specs=pl.BlockSpec((1, tn, 2 * n1, cols), lambda i, bi: (bi, i, 0, 0)),
        out_shape=jax.ShapeDtypeStruct((b, n2_pad, 2 * n1, cols), BF16),
        compiler_params=pltpu.CompilerParams(
            dimension_semantics=("arbitrary", "arbitrary"), vmem_limit_bytes=_vmem_limit(est1)),
    )(jnp.asarray(stage1), jnp.asarray(twiddle), pv, qv)

    gv = g.reshape(b, 2 * n2_pad, n1, cols).transpose(0, 2, 1, 3)
    tk = DFT_K1_TILE
    assert n1 % tk == 0
    est2 = 2 * (2 * n2_pad * n2_pad * 4 + tk * 2 * n2_pad * cols * 2 + tk * n2 * cols * 4) + 3 * n2_pad * cols * 4
    y = pl.pallas_call(
        functools.partial(_dft2_kernel, n2=n2),
        grid=(b, n1 // tk),
        in_specs=[
            pl.BlockSpec((n2_pad, 2 * n2_pad), lambda bi, i: (0, 0)),
            pl.BlockSpec((1, tk, 2 * n2_pad, cols), lambda bi, i: (bi, i, 0, 0)),
        ],
        out_specs=pl.BlockSpec((1, tk, n2, cols), lambda bi, i: (bi, i, 0, 0)),
        out_shape=jax.ShapeDtypeStruct((b, n1, n2, cols), BF16),
        compiler_params=pltpu.CompilerParams(
            dimension_semantics=("arbitrary", "arbitrary"), vmem_limit_bytes=_vmem_limit(est2)),
    )(jnp.asarray(stage2), gv)
    return y.transpose(0, 2, 1, 3).reshape(b, l, cols)


def _out_mlp_kernel(
    x_ref, a_ref, f_ref, ga_ref, gf_ref, wout_ref, gmlp_ref, wup_ref, wdown_ref, gfin_ref, o_ref,
):
    an = _rms(a_ref[0], ga_ref[...]).astype(BF16)
    fn = _rms(f_ref[0].astype(F32), gf_ref[...]).astype(BF16)
    h = x_ref[0]
    h = h + jnp.dot(an, wout_ref[:ATTN_W, :], preferred_element_type=F32)
    h = h + jnp.dot(fn, wout_ref[ATTN_W:, :], preferred_element_type=F32)
    m = _rms(h, gmlp_ref[...]).astype(BF16)
    act = jnp.maximum(jnp.dot(m, wup_ref[...], preferred_element_type=F32), 0.0)
    out = h + jnp.dot((act * act).astype(BF16), wdown_ref[...], preferred_element_type=F32)
    o_ref[0] = _rms(out, gfin_ref[...])


def _out_mlp(x, attn, four, g_attn_out, g_fourier_out, w_out, g_mlp, w_up, w_down, g_final):
    b, n_tok, d_model = x.shape
    d_ff = w_up.shape[1]
    t = TOKEN_TILE
    const2 = lambda bi, i: (0, 0)
    tile3 = lambda bi, i: (bi, i, 0)
    est = (2 * (2 * t * d_model * 4 + t * ATTN_W * 4 + t * FOURIER_W * 4)
           + 2 * (d_model * d_model + 2 * d_model * d_ff) * 2 + t * (4 * d_model * 4 + d_ff * 6))
    return pl.pallas_call(
        _out_mlp_kernel,
        grid=(b, n_tok // t),
        in_specs=[
            pl.BlockSpec((1, t, d_model), tile3),
            pl.BlockSpec((1, t, ATTN_W), tile3),
            pl.BlockSpec((1, t, FOURIER_W), tile3),
            pl.BlockSpec((1, ATTN_W), const2),
            pl.BlockSpec((1, FOURIER_W), const2),
            pl.BlockSpec((d_model, d_model), const2),
            pl.BlockSpec((1, d_model), const2),
            pl.BlockSpec((d_model, d_ff), const2),
            pl.BlockSpec((d_ff, d_model), const2),
            pl.BlockSpec((1, d_model), const2),
        ],
        out_specs=pl.BlockSpec((1, t, d_model), tile3),
        out_shape=jax.ShapeDtypeStruct((b, n_tok, d_model), F32),
        compiler_params=pltpu.CompilerParams(
            dimension_semantics=("arbitrary", "arbitrary"), vmem_limit_bytes=_vmem_limit(est)),
    )(x, attn, four, g_attn_out, g_fourier_out, w_out, g_mlp, w_up, w_down, g_final)


def kernel(x, meta_tokens, g_mix, w_in, g_q, g_k, w_fourier, g_attn_out, g_fourier_out, w_out,
           g_mlp, w_up, w_down, g_final):
    assert g_mix.shape[0] == 1, "meta-token rows are only carried for a single layer"
    b, n_tok, d_model = x.shape
    l = n_tok + N_META
    t = min(IN_PROJ_TILE, n_tok)
    assert n_tok % t == 0 and n_tok % TOKEN_TILE == 0 and n_tok % GRID_W == 0

    cos_tab, sin_tab = _rope_tables(n_tok, n_tok + t)
    meta_pad = jnp.pad(meta_tokens.astype(F32), ((0, t - N_META), (0, 0)))
    ab = _fourier_weights(w_fourier[0], l)
    q, k, v, p, qq = _in_proj(
        x, meta_pad, cos_tab, sin_tab, g_mix, w_in[0].astype(BF16), g_q, g_k, ab, l)
    attn = _attention(q, k, v, g_q, g_k, n_tok)
    four = _sequence_dft(p, qq, l)
    return _out_mlp(
        x, attn, four, g_attn_out, g_fourier_out, w_out[0].astype(BF16), g_mlp,
        w_up[0].astype(BF16), w_down[0].astype(BF16), g_final[None, :])
```

```python
import functools
import math

import jax
import jax.numpy as jnp
import numpy as np
from jax import lax
from jax.experimental import pallas as pl
from jax.experimental.pallas import tpu as pltpu

N_META = 16
GRID_W = 64
HEAD_DIM = 128
N_Q_HEADS = 4
N_KV_HEADS = 2
Q_GROUP = N_Q_HEADS // N_KV_HEADS
ATTN_W = N_Q_HEADS * HEAD_DIM
KV_W = N_KV_HEADS * HEAD_DIM
N_FOURIER_GROUPS = 4
FOURIER_GROUP_W = 128
FOURIER_W = N_FOURIER_GROUPS * FOURIER_GROUP_W
ROPE_THETA = 10000.0
ROPE_AXIS_DIM = HEAD_DIM // 2
RMS_EPS = 1e-6

V7X_VMEM_BYTES = 64 * 1024 * 1024
V7X_BF16_SUBLANES = 16
VMEM_RESERVE_BYTES = 8 * 1024 * 1024
VMEM_MIN_LIMIT_BYTES = 32 * 1024 * 1024

TOKEN_TILE = 512
IN_PROJ_TILE = 1024
IN_PROJ_ROWS = 256
ATTN_Q_TILE = 1024
ATTN_KV_CHUNK = 1024
ATTN_CHUNKS_PER_TRIP = 8
DFT_N2_TILE = 26
DFT_LANE_TILE = 128
MAX_UNSHIFTED_SCORE = 64.0
BOUND_SLACK = 1.02

F32 = jnp.float32
BF16 = jnp.bfloat16


def _vmem_limit(estimate_bytes):
    return int(min(max(2 * estimate_bytes, VMEM_MIN_LIMIT_BYTES), V7X_VMEM_BYTES - VMEM_RESERVE_BYTES))


def _rms(x, g):
    return x * lax.rsqrt(jnp.mean(x * x, axis=-1, keepdims=True) + RMS_EPS) * g


def _dft_factors(l):
    best = None
    for n1 in range(V7X_BF16_SUBLANES, l, V7X_BF16_SUBLANES):
        if l % n1 == 0:
            n2 = l // n1
            if best is None or abs(n1 - n2) < abs(best[0] - best[1]):
                best = (n1, n2)
    assert best is not None
    return best


def _rope_tables(n_tok, n_rows):
    rows_count = n_tok // GRID_W
    real_row = np.repeat(np.arange(rows_count, dtype=np.float64), GRID_W)
    real_col = np.tile(np.arange(GRID_W, dtype=np.float64), rows_count)
    meta_row = np.full((N_META,), -1.0)
    meta_col = np.arange(N_META, dtype=np.float64)
    pad = np.zeros((n_rows - n_tok - N_META,))
    row = np.concatenate([real_row, meta_row, pad])
    col = np.concatenate([real_col, meta_col, pad])
    inv_freq = ROPE_THETA ** (-np.arange(0, ROPE_AXIS_DIM, 2, dtype=np.float64) / ROPE_AXIS_DIM)
    ang_r = row[:, None] * inv_freq[None, :]
    ang_c = col[:, None] * inv_freq[None, :]
    cos = np.concatenate([np.cos(ang_r)] * 2 + [np.cos(ang_c)] * 2, axis=-1)
    sin = np.concatenate([-np.sin(ang_r), np.sin(ang_r), -np.sin(ang_c), np.sin(ang_c)], axis=-1)
    return jnp.asarray(cos.astype(np.float32)), jnp.asarray(sin.astype(np.float32))


def _channel_dft_tables(l):
    n = np.arange(FOURIER_GROUP_W)
    ang = 2.0 * np.pi * ((n[:, None] * n[None, :]) % FOURIER_GROUP_W) / FOURIER_GROUP_W
    scale = 1.0 / math.sqrt(l * FOURIER_GROUP_W)
    return (np.cos(ang) * scale).astype(np.float32), (np.sin(ang) * scale).astype(np.float32)


def _dft_stage_tables(l, n1, n2, n2_pad):
    s = N_META
    k1 = np.arange(n1)[:, None]
    ang1 = 2.0 * np.pi * (((k1 + s) * n2 * np.arange(n1)[None, :]) % l) / l
    fr, fi = np.cos(ang1), np.sin(ang1)
    stage1 = np.concatenate(
        [np.concatenate([fr, -fi], axis=1), np.concatenate([fi, fr], axis=1)], axis=0).astype(np.float32)
    ang_t = 2.0 * np.pi * (((k1 + s) * (np.arange(n2)[None, :] + s)) % l) / l
    twiddle = np.zeros((2, n1, n2_pad), np.float32)
    twiddle[0, :, :n2] = np.cos(ang_t)
    twiddle[1, :, :n2] = np.sin(ang_t)
    k2 = np.arange(n2)[:, None]
    jj = np.arange(n2)[None, :]
    ang2 = 2.0 * np.pi * ((k2 * (jj + s)) % n2) / n2
    stage2 = np.zeros((n2_pad, n2_pad, 2), np.float32)
    stage2[:n2, :n2, 0] = np.cos(ang2)
    stage2[:n2, :n2, 1] = -np.sin(ang2)
    return stage1, twiddle, stage2.reshape(n2_pad, 2 * n2_pad)


def _fourier_weight_kernel(c_ref, s_ref, w_ref, ab_ref):
    w = w_ref[0]
    a = jnp.dot(c_ref[...], w, preferred_element_type=F32, precision=lax.Precision.HIGHEST)
    b = jnp.dot(s_ref[...], w, preferred_element_type=F32, precision=lax.Precision.HIGHEST)
    ab_ref[0, :, :FOURIER_GROUP_W] = a.astype(BF16)
    ab_ref[0, :, FOURIER_GROUP_W:] = b.astype(BF16)


def _fourier_weights(w_f, l):
    c, s = _channel_dft_tables(l)
    gw = FOURIER_GROUP_W
    return pl.pallas_call(
        _fourier_weight_kernel,
        grid=(N_FOURIER_GROUPS,),
        in_specs=[
            pl.BlockSpec((gw, gw), lambda g: (0, 0)),
            pl.BlockSpec((gw, gw), lambda g: (0, 0)),
            pl.BlockSpec((1, gw, gw), lambda g: (g, 0, 0)),
        ],
        out_specs=pl.BlockSpec((1, gw, 2 * gw), lambda g: (g, 0, 0)),
        out_shape=jax.ShapeDtypeStruct((N_FOURIER_GROUPS, gw, 2 * gw), BF16),
    )(jnp.asarray(c), jnp.asarray(s), w_f)


def _rope(x, cos, sin, lower_half):
    partner = jnp.where(lower_half, pltpu.roll(x, HEAD_DIM - 32, axis=1), pltpu.roll(x, 32, axis=1))
    return x * cos + partner * sin


def _in_proj_kernel(
    x_ref, meta_ref, cos_ref, sin_ref, gmix_ref, win_ref, gq_ref, gk_ref, ab_ref,
    q_ref, k_ref, v_ref, p_ref, qq_ref, *, n_real_tiles, q_scale,
):
    i = pl.program_id(1)
    is_meta = i == n_real_tiles
    d = HEAD_DIM
    gw = FOURIER_GROUP_W
    lane = lax.broadcasted_iota(jnp.int32, (IN_PROJ_ROWS, d), 1)
    lower_half = (lane % ROPE_AXIS_DIM) < (ROPE_AXIS_DIM // 2)
    for r in range(x_ref.shape[1] // IN_PROJ_ROWS):
        rows = pl.ds(r * IN_PROJ_ROWS, IN_PROJ_ROWS)
        xt = jnp.where(is_meta, meta_ref[rows, :], x_ref[0, rows, :])
        hn = _rms(xt, gmix_ref[...]).astype(BF16)
        proj = jnp.dot(hn, win_ref[...], preferred_element_type=F32)
        cos = cos_ref[rows, :]
        sin = sin_ref[rows, :]
        for h in range(N_Q_HEADS):
            qh = _rope(_rms(proj[:, h * d:(h + 1) * d], gq_ref[...]), cos, sin, lower_half)
            q_ref[0, rows, h * d:(h + 1) * d] = (qh * q_scale).astype(BF16)
        for h in range(N_KV_HEADS):
            off = ATTN_W + h * d
            kh = _rope(_rms(proj[:, off:off + d], gk_ref[...]), cos, sin, lower_half)
            k_ref[0, rows, h * d:(h + 1) * d] = kh.astype(BF16)
            off = ATTN_W + KV_W + h * d
            v_ref[0, rows, 2 * h * d:(2 * h + 1) * d] = proj[:, off:off + d].astype(BF16)
            v_ref[0, rows, (2 * h + 1) * d:(2 * h + 2) * d] = jnp.ones((IN_PROJ_ROWS, d), BF16)
        for g in range(N_FOURIER_GROUPS):
            off = ATTN_W + 2 * KV_W + g * gw
            pq = jnp.dot(proj[:, off:off + gw].astype(BF16), ab_ref[g], preferred_element_type=F32)
            p_ref[0, rows, g * gw:(g + 1) * gw] = pq[:, :gw].astype(BF16)
            qq_ref[0, rows, g * gw:(g + 1) * gw] = pq[:, gw:].astype(BF16)


def _in_proj(x, meta_pad, cos_tab, sin_tab, g_mix, w_in, g_q, g_k, ab, l):
    b, n_tok, d_model = x.shape
    t = meta_pad.shape[0]
    n_real_tiles = n_tok // t
    n_rows = (n_real_tiles + 1) * t
    in_w = w_in.shape[1]
    q_scale = math.log2(math.e) / math.sqrt(HEAD_DIM)
    const2 = lambda bi, i: (0, 0)
    est = (2 * t * d_model * 4 * 2 + d_model * in_w * 2 * 2 + t * in_w * 4 * 3
           + 2 * t * (2 * ATTN_W + KV_W + 2 * FOURIER_W) * 2)
    return pl.pallas_call(
        functools.partial(_in_proj_kernel, n_real_tiles=n_real_tiles, q_scale=q_scale),
        grid=(b, n_real_tiles + 1),
        in_specs=[
            pl.BlockSpec((1, t, d_model), lambda bi, i: (bi, jnp.minimum(i, n_real_tiles - 1), 0)),
            pl.BlockSpec((t, d_model), const2),
            pl.BlockSpec((t, HEAD_DIM), lambda bi, i: (i, 0)),
            pl.BlockSpec((t, HEAD_DIM), lambda bi, i: (i, 0)),
            pl.BlockSpec((1, d_model), const2),
            pl.BlockSpec((d_model, in_w), const2),
            pl.BlockSpec((1, HEAD_DIM), const2),
            pl.BlockSpec((1, HEAD_DIM), const2),
            pl.BlockSpec((N_FOURIER_GROUPS, FOURIER_GROUP_W, 2 * FOURIER_GROUP_W), lambda bi, i: (0, 0, 0)),
        ],
        out_specs=[
            pl.BlockSpec((1, t, ATTN_W), lambda bi, i: (bi, i, 0)),
            pl.BlockSpec((1, t, KV_W), lambda bi, i: (bi, i, 0)),
            pl.BlockSpec((1, t, 2 * KV_W), lambda bi, i: (bi, i, 0)),
            pl.BlockSpec((1, t, FOURIER_W), lambda bi, i: (bi, i, 0)),
            pl.BlockSpec((1, t, FOURIER_W), lambda bi, i: (bi, i, 0)),
        ],
        out_shape=[
            jax.ShapeDtypeStruct((b, n_rows, ATTN_W), BF16),
            jax.ShapeDtypeStruct((b, n_rows, KV_W), BF16),
            jax.ShapeDtypeStruct((b, n_rows, 2 * KV_W), BF16),
            jax.ShapeDtypeStruct((b, l, FOURIER_W), BF16),
            jax.ShapeDtypeStruct((b, l, FOURIER_W), BF16),
        ],
        compiler_params=pltpu.CompilerParams(
            dimension_semantics=("arbitrary", "arbitrary"), vmem_limit_bytes=_vmem_limit(est)),
    )(x, meta_pad, cos_tab, sin_tab, g_mix, w_in, g_q, g_k, ab)


_NT = (((1,), (1,)), ((), ()))


def _attention_bounded_kernel(q_ref, qn_ref, k_ref, v_ref, km_ref, vm_ref, o_ref, q2_ref, p0_ref, p1_ref, acc_ref,
                              *, n_chunks, chunk):
    d = HEAD_DIM
    tq = q_ref.shape[1]
    first_tile = pl.program_id(2) == 0
    q2_ref[0, :tq] = q_ref[0, :, :d]
    q2_ref[0, tq:] = q_ref[0, :, d:]
    q2_ref[1, :tq] = qn_ref[0, :, :d]
    q2_ref[1, tq:] = qn_ref[0, :, d:]

    def probs(tile, c, p_ref):
        start = pl.multiple_of(c * chunk, chunk)
        s = lax.dot_general(q2_ref[tile], k_ref[0, pl.ds(start, chunk), :], _NT, preferred_element_type=F32)
        p_ref[...] = jnp.exp2(s).astype(BF16)

    def accumulate(c, p_ref):
        start = pl.multiple_of(c * chunk, chunk)
        acc_ref[...] += jnp.dot(p_ref[...], v_ref[0, pl.ds(start, chunk), :], preferred_element_type=F32)

    s_meta = lax.dot_general(q2_ref[0], km_ref[0], _NT, preferred_element_type=F32)
    acc_ref[...] = jnp.dot(jnp.exp2(s_meta).astype(BF16), vm_ref[0], preferred_element_type=F32)

    @pl.when(first_tile)
    def _():
        probs(0, 0, p0_ref)

    bufs = (p0_ref, p1_ref)
    per_trip = ATTN_CHUNKS_PER_TRIP if n_chunks % ATTN_CHUNKS_PER_TRIP == 0 else 2

    def body(j, carry):
        c = per_trip * j
        for u in range(per_trip):
            nxt = c + u + 1
            if u + 1 < per_trip:
                probs(0, nxt, bufs[(u + 1) % 2])
            else:
                wraps = nxt >= n_chunks
                probs(wraps.astype(jnp.int32), jnp.where(wraps, 0, nxt), bufs[(u + 1) % 2])
            accumulate(c + u, bufs[u % 2])
        return carry

    lax.fori_loop(0, n_chunks // per_trip, body, 0)

    acc = acc_ref[...]
    out = acc[:, :d] / acc[:, d:]
    o_ref[0, :, :d] = out[:tq]
    o_ref[0, :, d:] = out[tq:]


def _attention_kernel(q_ref, k_ref, v_ref, km_ref, vm_ref, o_ref, q2_ref, s0_ref, s1_ref, m_ref, acc_ref,
                      *, n_chunks, chunk):
    d = HEAD_DIM
    tq = q_ref.shape[1]
    q2_ref[:tq] = q_ref[0, :, :d]
    q2_ref[tq:] = q_ref[0, :, d:]

    def scores(c, s_ref):
        start = pl.multiple_of(c * chunk, chunk)
        s_ref[...] = lax.dot_general(
            q2_ref[...], k_ref[0, pl.ds(start, chunk), :], _NT, preferred_element_type=F32)

    def update(c, s_ref):
        start = pl.multiple_of(c * chunk, chunk)
        s = s_ref[...]
        m_old = m_ref[...]
        m_new = jnp.maximum(m_old, jnp.max(s, axis=-1, keepdims=True))
        p = jnp.exp2(s - m_new).astype(BF16)
        pv = jnp.dot(p, v_ref[0, pl.ds(start, chunk), :], preferred_element_type=F32)
        acc_ref[...] = acc_ref[...] * jnp.exp2(m_old - m_new) + pv
        m_ref[...] = m_new

    s_meta = lax.dot_general(q2_ref[...], km_ref[0], _NT, preferred_element_type=F32)
    m_meta = jnp.max(s_meta, axis=-1, keepdims=True)
    m_ref[...] = m_meta
    acc_ref[...] = jnp.dot(jnp.exp2(s_meta - m_meta).astype(BF16), vm_ref[0], preferred_element_type=F32)

    scores(0, s0_ref)

    def body(j, carry):
        c = 2 * j
        scores(c + 1, s1_ref)
        update(c, s0_ref)
        scores(jnp.minimum(c + 2, n_chunks - 1), s0_ref)
        update(c + 1, s1_ref)
        return carry

    lax.fori_loop(0, n_chunks // 2, body, 0)

    acc = acc_ref[...]
    out = acc[:, :d] / acc[:, d:]
    o_ref[0, :, :d] = out[:tq]
    o_ref[0, :, d:] = out[tq:]


def _attention_call(q, k, v, n_tok, bounded):
    b = q.shape[0]
    tq = min(ATTN_Q_TILE, n_tok)
    chunk = min(ATTN_KV_CHUNK, n_tok)
    d = HEAD_DIM
    gq = Q_GROUP * d
    m = Q_GROUP * tq
    meta_blk = n_tok // N_META
    assert (n_tok // chunk) % 2 == 0
    est = 2 * (tq * gq * 2 + n_tok * d * 2 + n_tok * 2 * d * 2 + tq * gq * 4) + m * (
        chunk * 4 * 3 + chunk * 2 + 2 * d * 4 * 3)
    n_tiles = n_tok // tq
    q_spec = pl.BlockSpec((1, tq, gq), lambda bi, g, i: (bi, i, g))
    if bounded:
        body = _attention_bounded_kernel
        scratch = [pltpu.VMEM((2, m, d), BF16), pltpu.VMEM((m, chunk), BF16), pltpu.VMEM((m, chunk), BF16),
                   pltpu.VMEM((m, 2 * d), F32)]
        q_specs = [q_spec, pl.BlockSpec((1, tq, gq), lambda bi, g, i: (bi, jnp.minimum(i + 1, n_tiles - 1), g))]
        q_args = (q, q)
    else:
        body = _attention_kernel
        scratch = [pltpu.VMEM((m, d), BF16), pltpu.VMEM((m, chunk), F32), pltpu.VMEM((m, chunk), F32),
                   pltpu.VMEM((m, 1), F32), pltpu.VMEM((m, 2 * d), F32)]
        q_specs = [q_spec]
        q_args = (q,)
    return pl.pallas_call(
        functools.partial(body, n_chunks=n_tok // chunk, chunk=chunk),
        grid=(b, N_KV_HEADS, n_tiles),
        in_specs=q_specs + [
            pl.BlockSpec((1, n_tok, d), lambda bi, g, i: (bi, 0, g)),
            pl.BlockSpec((1, n_tok, 2 * d), lambda bi, g, i: (bi, 0, g)),
            pl.BlockSpec((1, N_META, d), lambda bi, g, i: (bi, meta_blk, g)),
            pl.BlockSpec((1, N_META, 2 * d), lambda bi, g, i: (bi, meta_blk, g)),
        ],
        out_specs=pl.BlockSpec((1, tq, gq), lambda bi, g, i: (bi, i, g)),
        out_shape=jax.ShapeDtypeStruct((b, n_tok, ATTN_W), F32),
        scratch_shapes=scratch,
        compiler_params=pltpu.CompilerParams(
            dimension_semantics=("arbitrary", "arbitrary", "arbitrary"), vmem_limit_bytes=_vmem_limit(est)),
    )(*q_args, k, v, k, v)


def _attention(q, k, v, g_q, g_k, n_tok):
    score_bound = (math.log2(math.e) / math.sqrt(HEAD_DIM)) * HEAD_DIM * BOUND_SLACK * (
        jnp.max(jnp.abs(g_q)) * jnp.max(jnp.abs(g_k)))
    return lax.cond(
        score_bound <= MAX_UNSHIFTED_SCORE,
        lambda: _attention_call(q, k, v, n_tok, True),
        lambda: _attention_call(q, k, v, n_tok, False))


def _dft1_kernel(f_ref, t_ref, p_ref, q_ref, g_ref, *, n2, cols):
    step = pl.program_id(0)
    n1 = t_ref.shape[2]
    n_sub = t_ref.shape[3]
    f_top = f_ref[:n1, :]
    f_bot = f_ref[n1:, :]
    for j in range(n_sub):
        z = jnp.concatenate(
            [p_ref[0, :, j * cols:(j + 1) * cols], q_ref[0, :, j * cols:(j + 1) * cols]], axis=0)
        valid = step * n_sub + j < n2
        z = jnp.where(valid, z, jnp.zeros_like(z))
        tr = t_ref[0, 0, :, j:j + 1]
        ti = t_ref[0, 1, :, j:j + 1]
        m = jnp.concatenate([tr * f_top - ti * f_bot, ti * f_top + tr * f_bot], axis=0).astype(BF16)
        res = jnp.dot(m, z, preferred_element_type=F32)
        g_ref[0, j, 0, :n1] = res[:n1]
        g_ref[0, j, 1, :n1] = res[n1:]


def _dft2_kernel(f_ref, g_ref, y_ref, *, n2, pitch):
    n1 = y_ref.shape[1]
    n_rows = f_ref.shape[1]
    f = f_ref[...].astype(BF16)

    def body(k1, carry):
        rhs = g_ref[0, pl.ds(k1, n_rows, stride=pitch), :].astype(BF16)
        res = jnp.dot(f, rhs, preferred_element_type=F32)
        y_ref[0, k1] = res[:n2].astype(BF16)
        return carry

    lax.fori_loop(0, n1, body, 0, unroll=8)


def _slab_pitch(n1):
    pitch = -(-n1 // 8) * 8
    return pitch if (pitch // 8) % 2 else pitch + 8


def _sequence_dft(p, q, l):
    b = p.shape[0]
    cols = p.shape[2]
    n1, n2 = _dft_factors(l)
    tn = DFT_N2_TILE
    n2_pad = -(-n2 // tn) * tn
    stage1, twiddle, stage2 = _dft_stage_tables(l, n1, n2, n2_pad)
    twiddle = twiddle.reshape(2, n1, n2_pad // tn, tn).transpose(2, 0, 1, 3)
    pv = p.reshape(b, n1, n2 * cols)
    qv = q.reshape(b, n1, n2 * cols)
    pitch = _slab_pitch(n1)
    est1 = 2 * (4 * n1 * n1 * 4 + 2 * n1 * tn * cols * 2 + tn * 2 * pitch * cols * 4) + 8 * n1 * cols * 4
    g = pl.pallas_call(
        functools.partial(_dft1_kernel, n2=n2, cols=cols),
        grid=(n2_pad // tn, b),
        in_specs=[
            pl.BlockSpec((2 * n1, 2 * n1), lambda i, bi: (0, 0)),
            pl.BlockSpec((1, 2, n1, tn), lambda i, bi: (i, 0, 0, 0)),
            pl.BlockSpec((1, n1, tn * cols), lambda i, bi: (bi, 0, i)),
            pl.BlockSpec((1, n1, tn * cols), lambda i, bi: (bi, 0, i)),
        ],
        out_specs=pl.BlockSpec((1, tn, 2, pitch, cols), lambda i, bi: (bi, i, 0, 0, 0)),
        out_shape=jax.ShapeDtypeStruct((b, n2_pad, 2, pitch, cols), F32),
        compiler_params=pltpu.CompilerParams(
            dimension_semantics=("arbitrary", "arbitrary"), vmem_limit_bytes=_vmem_limit(est1)),
    )(jnp.asarray(stage1), jnp.asarray(twiddle), pv, qv)

    gv = g.reshape(b, 2 * n2_pad * pitch, cols)
    lanes = DFT_LANE_TILE
    est2 = 2 * (2 * n2_pad * n2_pad * 4 + 2 * n2_pad * pitch * lanes * 4 + n1 * n2_pad * lanes * 2) + 8 * n2_pad * lanes * 4
    y = pl.pallas_call(
        functools.partial(_dft2_kernel, n2=n2, pitch=pitch),
        grid=(b, cols // lanes),
        in_specs=[
            pl.BlockSpec((n2_pad, 2 * n2_pad), lambda bi, i: (0, 0)),
            pl.BlockSpec((1, 2 * n2_pad * pitch, lanes), lambda bi, i: (bi, 0, i)),
        ],
        out_specs=pl.BlockSpec((1, n1, n2, lanes), lambda bi, i: (bi, 0, 0, i)),
        out_shape=jax.ShapeDtypeStruct((b, n1, n2, cols), BF16),
        compiler_params=pltpu.CompilerParams(
            dimension_semantics=("arbitrary", "arbitrary"), vmem_limit_bytes=_vmem_limit(est2)),
    )(jnp.asarray(stage2), gv)
    return y.transpose(0, 2, 1, 3).reshape(b, l, cols)


def _out_mlp_kernel(
    x_ref, a_ref, f_ref, ga_ref, gf_ref, wout_ref, gmlp_ref, wup_ref, wdown_ref, gfin_ref, o_ref,
):
    an = _rms(a_ref[0], ga_ref[...]).astype(BF16)
    fn = _rms(f_ref[0].astype(F32), gf_ref[...]).astype(BF16)
    h = x_ref[0]
    h = h + jnp.dot(an, wout_ref[:ATTN_W, :], preferred_element_type=F32)
    h = h + jnp.dot(fn, wout_ref[ATTN_W:, :], preferred_element_type=F32)
    m = _rms(h, gmlp_ref[...]).astype(BF16)
    act = jnp.maximum(jnp.dot(m, wup_ref[...], preferred_element_type=F32), 0.0)
    out = h + jnp.dot((act * act).astype(BF16), wdown_ref[...], preferred_element_type=F32)
    o_ref[0] = _rms(out, gfin_ref[...])


def _out_mlp(x, attn, four, g_attn_out, g_fourier_out, w_out, g_mlp, w_up, w_down, g_final):
    b, n_tok, d_model = x.shape
    d_ff = w_up.shape[1]
    t = TOKEN_TILE
    const2 = lambda bi, i: (0, 0)
    tile3 = lambda bi, i: (bi, i, 0)
    est = (2 * (2 * t * d_model * 4 + t * ATTN_W * 4 + t * FOURIER_W * 4)
           + 2 * (d_model * d_model + 2 * d_model * d_ff) * 2 + t * (4 * d_model * 4 + d_ff * 6))
    return pl.pallas_call(
        _out_mlp_kernel,
        grid=(b, n_tok // t),
        in_specs=[
            pl.BlockSpec((1, t, d_model), tile3),
            pl.BlockSpec((1, t, ATTN_W), tile3),
            pl.BlockSpec((1, t, FOURIER_W), tile3),
            pl.BlockSpec((1, ATTN_W), const2),
            pl.BlockSpec((1, FOURIER_W), const2),
            pl.BlockSpec((d_model, d_model), const2),
            pl.BlockSpec((1, d_model), const2),
            pl.BlockSpec((d_model, d_ff), const2),
            pl.BlockSpec((d_ff, d_model), const2),
            pl.BlockSpec((1, d_model), const2),
        ],
        out_specs=pl.BlockSpec((1, t, d_model), tile3),
        out_shape=jax.ShapeDtypeStruct((b, n_tok, d_model), F32),
        compiler_params=pltpu.CompilerParams(
            dimension_semantics=("arbitrary", "arbitrary"), vmem_limit_bytes=_vmem_limit(est)),
    )(x, attn, four, g_attn_out, g_fourier_out, w_out, g_mlp, w_up, w_down, g_final)


def kernel(x, meta_tokens, g_mix, w_in, g_q, g_k, w_fourier, g_attn_out, g_fourier_out, w_out,
           g_mlp, w_up, w_down, g_final):
    assert g_mix.shape[0] == 1, "meta-token rows are only carried for a single layer"
    b, n_tok, d_model = x.shape
    l = n_tok + N_META
    t = min(IN_PROJ_TILE, n_tok)
    assert n_tok % t == 0 and n_tok % TOKEN_TILE == 0 and n_tok % GRID_W == 0

    cos_tab, sin_tab = _rope_tables(n_tok, n_tok + t)
    meta_pad = jnp.pad(meta_tokens.astype(F32), ((0, t - N_META), (0, 0)))
    ab = _fourier_weights(w_fourier[0], l)
    q, k, v, p, qq = _in_proj(
        x, meta_pad, cos_tab, sin_tab, g_mix, w_in[0].astype(BF16), g_q, g_k, ab, l)
    attn = _attention(q, k, v, g_q, g_k, n_tok)
    four = _sequence_dft(p, qq, l)
    return _out_mlp(
        x, attn, four, g_attn_out, g_fourier_out, w_out[0].astype(BF16), g_mlp,
        w_up[0].astype(BF16), w_down[0].astype(BF16), g_final[None, :])
```

```python
import functools
import math

import jax
import jax.numpy as jnp
import numpy as np
from jax import lax
from jax.experimental import pallas as pl
from jax.experimental.pallas import tpu as pltpu

N_META = 16
GRID_W = 64
HEAD_DIM = 128
N_Q_HEADS = 4
N_KV_HEADS = 2
Q_GROUP = N_Q_HEADS // N_KV_HEADS
ATTN_W = N_Q_HEADS * HEAD_DIM
KV_W = N_KV_HEADS * HEAD_DIM
N_FOURIER_GROUPS = 4
FOURIER_GROUP_W = 128
FOURIER_W = N_FOURIER_GROUPS * FOURIER_GROUP_W
ROPE_THETA = 10000.0
ROPE_AXIS_DIM = HEAD_DIM // 2
RMS_EPS = 1e-6

V7X_VMEM_BYTES = 64 * 1024 * 1024
V7X_BF16_SUBLANES = 16
VMEM_RESERVE_BYTES = 8 * 1024 * 1024
VMEM_MIN_LIMIT_BYTES = 32 * 1024 * 1024

TOKEN_TILE = 512
IN_PROJ_TILE = 1024
IN_PROJ_ROWS = 256
ATTN_Q_TILE = 1024
ATTN_KV_CHUNK = 1024
ATTN_CHUNKS_PER_TRIP = 8
DFT_N2_TILE = 26
DFT_LANE_TILE = 128
MAX_UNSHIFTED_SCORE = 64.0
BOUND_SLACK = 1.02

F32 = jnp.float32
BF16 = jnp.bfloat16


def _vmem_limit(estimate_bytes):
    return int(min(max(2 * estimate_bytes, VMEM_MIN_LIMIT_BYTES), V7X_VMEM_BYTES - VMEM_RESERVE_BYTES))


def _rms(x, g):
    return x * lax.rsqrt(jnp.mean(x * x, axis=-1, keepdims=True) + RMS_EPS) * g


def _dft_factors(l):
    best = None
    for n1 in range(V7X_BF16_SUBLANES, l, V7X_BF16_SUBLANES):
        if l % n1 == 0:
            n2 = l // n1
            if best is None or abs(n1 - n2) < abs(best[0] - best[1]):
                best = (n1, n2)
    assert best is not None
    return best


def _rope_tables(n_tok, n_rows):
    rows_count = n_tok // GRID_W
    real_row = np.repeat(np.arange(rows_count, dtype=np.float64), GRID_W)
    real_col = np.tile(np.arange(GRID_W, dtype=np.float64), rows_count)
    meta_row = np.full((N_META,), -1.0)
    meta_col = np.arange(N_META, dtype=np.float64)
    pad = np.zeros((n_rows - n_tok - N_META,))
    row = np.concatenate([real_row, meta_row, pad])
    col = np.concatenate([real_col, meta_col, pad])
    inv_freq = ROPE_THETA ** (-np.arange(0, ROPE_AXIS_DIM, 2, dtype=np.float64) / ROPE_AXIS_DIM)
    ang_r = row[:, None] * inv_freq[None, :]
    ang_c = col[:, None] * inv_freq[None, :]
    cos = np.concatenate([np.cos(ang_r)] * 2 + [np.cos(ang_c)] * 2, axis=-1)
    sin = np.concatenate([-np.sin(ang_r), np.sin(ang_r), -np.sin(ang_c), np.sin(ang_c)], axis=-1)
    return jnp.asarray(cos.astype(np.float32)), jnp.asarray(sin.astype(np.float32))


def _channel_dft_tables(l):
    n = np.arange(FOURIER_GROUP_W)
    ang = 2.0 * np.pi * ((n[:, None] * n[None, :]) % FOURIER_GROUP_W) / FOURIER_GROUP_W
    scale = 1.0 / math.sqrt(l * FOURIER_GROUP_W)
    return (np.cos(ang) * scale).astype(np.float32), (np.sin(ang) * scale).astype(np.float32)


def _dft_stage_tables(l, n1, n2, n2_pad):
    s = N_META
    k1 = np.arange(n1)[:, None]
    ang1 = 2.0 * np.pi * (((k1 + s) * n2 * np.arange(n1)[None, :]) % l) / l
    fr, fi = np.cos(ang1), np.sin(ang1)
    stage1 = np.concatenate(
        [np.concatenate([fr, -fi], axis=1), np.concatenate([fi, fr], axis=1)], axis=0).astype(np.float32)
    ang_t = 2.0 * np.pi * (((k1 + s) * (np.arange(n2)[None, :] + s)) % l) / l
    twiddle = np.zeros((2, n1, n2_pad), np.float32)
    twiddle[0, :, :n2] = np.cos(ang_t)
    twiddle[1, :, :n2] = np.sin(ang_t)
    k2 = np.arange(n2)[:, None]
    jj = np.arange(n2)[None, :]
    ang2 = 2.0 * np.pi * ((k2 * (jj + s)) % n2) / n2
    stage2 = np.zeros((n2_pad, n2_pad, 2), np.float32)
    stage2[:n2, :n2, 0] = np.cos(ang2)
    stage2[:n2, :n2, 1] = -np.sin(ang2)
    return stage1, twiddle, stage2.reshape(n2_pad, 2 * n2_pad)


def _fourier_weight_kernel(c_ref, s_ref, w_ref, ab_ref):
    w = w_ref[0]
    a = jnp.dot(c_ref[...], w, preferred_element_type=F32, precision=lax.Precision.HIGHEST)
    b = jnp.dot(s_ref[...], w, preferred_element_type=F32, precision=lax.Precision.HIGHEST)
    ab_ref[0, :, :FOURIER_GROUP_W] = a.astype(BF16)
    ab_ref[0, :, FOURIER_GROUP_W:] = b.astype(BF16)


def _fourier_weights(w_f, l):
    c, s = _channel_dft_tables(l)
    gw = FOURIER_GROUP_W
    return pl.pallas_call(
        _fourier_weight_kernel,
        grid=(N_FOURIER_GROUPS,),
        in_specs=[
            pl.BlockSpec((gw, gw), lambda g: (0, 0)),
            pl.BlockSpec((gw, gw), lambda g: (0, 0)),
            pl.BlockSpec((1, gw, gw), lambda g: (g, 0, 0)),
        ],
        out_specs=pl.BlockSpec((1, gw, 2 * gw), lambda g: (g, 0, 0)),
        out_shape=jax.ShapeDtypeStruct((N_FOURIER_GROUPS, gw, 2 * gw), BF16),
    )(jnp.asarray(c), jnp.asarray(s), w_f)


def _rope(x, cos, sin, lower_half):
    partner = jnp.where(lower_half, pltpu.roll(x, HEAD_DIM - 32, axis=1), pltpu.roll(x, 32, axis=1))
    return x * cos + partner * sin


def _in_proj_kernel(
    x_ref, meta_ref, cos_ref, sin_ref, gmix_ref, win_ref, gq_ref, gk_ref, ab_ref,
    q_ref, k_ref, v_ref, p_ref, qq_ref, *, n_real_tiles, q_scale,
):
    i = pl.program_id(1)
    is_meta = i == n_real_tiles
    d = HEAD_DIM
    gw = FOURIER_GROUP_W
    lane = lax.broadcasted_iota(jnp.int32, (IN_PROJ_ROWS, d), 1)
    lower_half = (lane % ROPE_AXIS_DIM) < (ROPE_AXIS_DIM // 2)
    for r in range(x_ref.shape[1] // IN_PROJ_ROWS):
        rows = pl.ds(r * IN_PROJ_ROWS, IN_PROJ_ROWS)
        xt = jnp.where(is_meta, meta_ref[rows, :], x_ref[0, rows, :])
        hn = _rms(xt, gmix_ref[...]).astype(BF16)
        proj = jnp.dot(hn, win_ref[...], preferred_element_type=F32)
        cos = cos_ref[rows, :]
        sin = sin_ref[rows, :]
        for h in range(N_Q_HEADS):
            qh = _rope(_rms(proj[:, h * d:(h + 1) * d], gq_ref[...]), cos, sin, lower_half)
            q_ref[0, rows, h * d:(h + 1) * d] = (qh * q_scale).astype(BF16)
        for h in range(N_KV_HEADS):
            off = ATTN_W + h * d
            kh = _rope(_rms(proj[:, off:off + d], gk_ref[...]), cos, sin, lower_half)
            k_ref[0, rows, h * d:(h + 1) * d] = kh.astype(BF16)
            off = ATTN_W + KV_W + h * d
            v_ref[0, rows, 2 * h * d:(2 * h + 1) * d] = proj[:, off:off + d].astype(BF16)
            v_ref[0, rows, (2 * h + 1) * d:(2 * h + 2) * d] = jnp.ones((IN_PROJ_ROWS, d), BF16)
        for g in range(N_FOURIER_GROUPS):
            off = ATTN_W + 2 * KV_W + g * gw
            pq = jnp.dot(proj[:, off:off + gw].astype(BF16), ab_ref[g], preferred_element_type=F32)
            p_ref[0, rows, g * gw:(g + 1) * gw] = pq[:, :gw].astype(BF16)
            qq_ref[0, rows, g * gw:(g + 1) * gw] = pq[:, gw:].astype(BF16)


def _in_proj(x, meta_pad, cos_tab, sin_tab, g_mix, w_in, g_q, g_k, ab, l):
    b, n_tok, d_model = x.shape
    t = meta_pad.shape[0]
    n_real_tiles = n_tok // t
    n_rows = (n_real_tiles + 1) * t
    in_w = w_in.shape[1]
    q_scale = math.log2(math.e) / math.sqrt(HEAD_DIM)
    const2 = lambda bi, i: (0, 0)
    est = (2 * t * d_model * 4 * 2 + d_model * in_w * 2 * 2 + t * in_w * 4 * 3
           + 2 * t * (2 * ATTN_W + KV_W + 2 * FOURIER_W) * 2)
    return pl.pallas_call(
        functools.partial(_in_proj_kernel, n_real_tiles=n_real_tiles, q_scale=q_scale),
        grid=(b, n_real_tiles + 1),
        in_specs=[
            pl.BlockSpec((1, t, d_model), lambda bi, i: (bi, jnp.minimum(i, n_real_tiles - 1), 0)),
            pl.BlockSpec((t, d_model), const2),
            pl.BlockSpec((t, HEAD_DIM), lambda bi, i: (i, 0)),
            pl.BlockSpec((t, HEAD_DIM), lambda bi, i: (i, 0)),
            pl.BlockSpec((1, d_model), const2),
            pl.BlockSpec((d_model, in_w), const2),
            pl.BlockSpec((1, HEAD_DIM), const2),
            pl.BlockSpec((1, HEAD_DIM), const2),
            pl.BlockSpec((N_FOURIER_GROUPS, FOURIER_GROUP_W, 2 * FOURIER_GROUP_W), lambda bi, i: (0, 0, 0)),
        ],
        out_specs=[
            pl.BlockSpec((1, t, ATTN_W), lambda bi, i: (bi, i, 0)),
            pl.BlockSpec((1, t, KV_W), lambda bi, i: (bi, i, 0)),
            pl.BlockSpec((1, t, 2 * KV_W), lambda bi, i: (bi, i, 0)),
            pl.BlockSpec((1, t, FOURIER_W), lambda bi, i: (bi, i, 0)),
            pl.BlockSpec((1, t, FOURIER_W), lambda bi, i: (bi, i, 0)),
        ],
        out_shape=[
            jax.ShapeDtypeStruct((b, n_rows, ATTN_W), BF16),
            jax.ShapeDtypeStruct((b, n_rows, KV_W), BF16),
            jax.ShapeDtypeStruct((b, n_rows, 2 * KV_W), BF16),
            jax.ShapeDtypeStruct((b, l, FOURIER_W), BF16),
            jax.ShapeDtypeStruct((b, l, FOURIER_W), BF16),
        ],
        compiler_params=pltpu.CompilerParams(
            dimension_semantics=("arbitrary", "arbitrary"), vmem_limit_bytes=_vmem_limit(est)),
    )(x, meta_pad, cos_tab, sin_tab, g_mix, w_in, g_q, g_k, ab)


_NT = (((1,), (1,)), ((), ()))


def _attention_bounded_kernel(q_ref, qn_ref, k_ref, v_ref, km_ref, vm_ref, o_ref, q2_ref, p0_ref, p1_ref, acc_ref,
                              *, n_chunks, chunk):
    d = HEAD_DIM
    tq = q_ref.shape[1]
    first_tile = pl.program_id(2) == 0
    q2_ref[0, :tq] = q_ref[0, :, :d]
    q2_ref[0, tq:] = q_ref[0, :, d:]
    q2_ref[1, :tq] = qn_ref[0, :, :d]
    q2_ref[1, tq:] = qn_ref[0, :, d:]

    def probs(tile, c, p_ref):
        start = pl.multiple_of(c * chunk, chunk)
        s = lax.dot_general(q2_ref[tile], k_ref[0, pl.ds(start, chunk), :], _NT, preferred_element_type=F32)
        p_ref[...] = jnp.exp2(s).astype(BF16)

    def accumulate(c, p_ref):
        start = pl.multiple_of(c * chunk, chunk)
        acc_ref[...] += jnp.dot(p_ref[...], v_ref[0, pl.ds(start, chunk), :], preferred_element_type=F32)

    s_meta = lax.dot_general(q2_ref[0], km_ref[0], _NT, preferred_element_type=F32)
    acc_ref[...] = jnp.dot(jnp.exp2(s_meta).astype(BF16), vm_ref[0], preferred_element_type=F32)

    @pl.when(first_tile)
    def _():
        probs(0, 0, p0_ref)

    bufs = (p0_ref, p1_ref)
    per_trip = ATTN_CHUNKS_PER_TRIP if n_chunks % ATTN_CHUNKS_PER_TRIP == 0 else 2

    def body(j, carry):
        c = per_trip * j
        for u in range(per_trip):
            nxt = c + u + 1
            if u + 1 < per_trip:
                probs(0, nxt, bufs[(u + 1) % 2])
            else:
                wraps = nxt >= n_chunks
                probs(wraps.astype(jnp.int32), jnp.where(wraps, 0, nxt), bufs[(u + 1) % 2])
            accumulate(c + u, bufs[u % 2])
        return carry

    lax.fori_loop(0, n_chunks // per_trip, body, 0)

    acc = acc_ref[...]
    out = acc[:, :d] / acc[:, d:]
    o_ref[0, :, :d] = out[:tq]
    o_ref[0, :, d:] = out[tq:]


def _attention_kernel(q_ref, k_ref, v_ref, km_ref, vm_ref, o_ref, q2_ref, s0_ref, s1_ref, m_ref, acc_ref,
                      *, n_chunks, chunk):
    d = HEAD_DIM
    tq = q_ref.shape[1]
    q2_ref[:tq] = q_ref[0, :, :d]
    q2_ref[tq:] = q_ref[0, :, d:]

    def scores(c, s_ref):
        start = pl.multiple_of(c * chunk, chunk)
        s_ref[...] = lax.dot_general(
            q2_ref[...], k_ref[0, pl.ds(start, chunk), :], _NT, preferred_element_type=F32)

    def update(c, s_ref):
        start = pl.multiple_of(c * chunk, chunk)
        s = s_ref[...]
        m_old = m_ref[...]
        m_new = jnp.maximum(m_old, jnp.max(s, axis=-1, keepdims=True))
        p = jnp.exp2(s - m_new).astype(BF16)
        pv = jnp.dot(p, v_ref[0, pl.ds(start, chunk), :], preferred_element_type=F32)
        acc_ref[...] = acc_ref[...] * jnp.exp2(m_old - m_new) + pv
        m_ref[...] = m_new

    s_meta = lax.dot_general(q2_ref[...], km_ref[0], _NT, preferred_element_type=F32)
    m_meta = jnp.max(s_meta, axis=-1, keepdims=True)
    m_ref[...] = m_meta
    acc_ref[...] = jnp.dot(jnp.exp2(s_meta - m_meta).astype(BF16), vm_ref[0], preferred_element_type=F32)

    scores(0, s0_ref)

    def body(j, carry):
        c = 2 * j
        scores(c + 1, s1_ref)
        update(c, s0_ref)
        scores(jnp.minimum(c + 2, n_chunks - 1), s0_ref)
        update(c + 1, s1_ref)
        return carry

    lax.fori_loop(0, n_chunks // 2, body, 0)

    acc = acc_ref[...]
    out = acc[:, :d] / acc[:, d:]
    o_ref[0, :, :d] = out[:tq]
    o_ref[0, :, d:] = out[tq:]


def _attention_call(q, k, v, n_tok, bounded):
    b = q.shape[0]
    tq = min(ATTN_Q_TILE, n_tok)
    chunk = min(ATTN_KV_CHUNK, n_tok)
    d = HEAD_DIM
    gq = Q_GROUP * d
    m = Q_GROUP * tq
    meta_blk = n_tok // N_META
    assert (n_tok // chunk) % 2 == 0
    est = 2 * (tq * gq * 2 + n_tok * d * 2 + n_tok * 2 * d * 2 + tq * gq * 4) + m * (
        chunk * 4 * 3 + chunk * 2 + 2 * d * 4 * 3)
    n_tiles = n_tok // tq
    q_spec = pl.BlockSpec((1, tq, gq), lambda bi, g, i: (bi, i, g))
    if bounded:
        body = _attention_bounded_kernel
        scratch = [pltpu.VMEM((2, m, d), BF16), pltpu.VMEM((m, chunk), BF16), pltpu.VMEM((m, chunk), BF16),
                   pltpu.VMEM((m, 2 * d), F32)]
        q_specs = [q_spec, pl.BlockSpec((1, tq, gq), lambda bi, g, i: (bi, jnp.minimum(i + 1, n_tiles - 1), g))]
        q_args = (q, q)
    else:
        body = _attention_kernel
        scratch = [pltpu.VMEM((m, d), BF16), pltpu.VMEM((m, chunk), F32), pltpu.VMEM((m, chunk), F32),
                   pltpu.VMEM((m, 1), F32), pltpu.VMEM((m, 2 * d), F32)]
        q_specs = [q_spec]
        q_args = (q,)
    return pl.pallas_call(
        functools.partial(body, n_chunks=n_tok // chunk, chunk=chunk),
        grid=(b, N_KV_HEADS, n_tiles),
        in_specs=q_specs + [
            pl.BlockSpec((1, n_tok, d), lambda bi, g, i: (bi, 0, g)),
            pl.BlockSpec((1, n_tok, 2 * d), lambda bi, g, i: (bi, 0, g)),
            pl.BlockSpec((1, N_META, d), lambda bi, g, i: (bi, meta_blk, g)),
            pl.BlockSpec((1, N_META, 2 * d), lambda bi, g, i: (bi, meta_blk, g)),
        ],
        out_specs=pl.BlockSpec((1, tq, gq), lambda bi, g, i: (bi, i, g)),
        out_shape=jax.ShapeDtypeStruct((b, n_tok, ATTN_W), F32),
        scratch_shapes=scratch,
        compiler_params=pltpu.CompilerParams(
            dimension_semantics=("arbitrary", "arbitrary", "arbitrary"), vmem_limit_bytes=_vmem_limit(est)),
    )(*q_args, k, v, k, v)


def _attention(q, k, v, g_q, g_k, n_tok):
    score_bound = (math.log2(math.e) / math.sqrt(HEAD_DIM)) * HEAD_DIM * BOUND_SLACK * (
        jnp.max(jnp.abs(g_q)) * jnp.max(jnp.abs(g_k)))
    return lax.cond(
        score_bound <= MAX_UNSHIFTED_SCORE,
        lambda: _attention_call(q, k, v, n_tok, True),
        lambda: _attention_call(q, k, v, n_tok, False))


def _dft1_kernel(f_ref, t_ref, p_ref, q_ref, g_ref, *, n2, cols):
    step = pl.program_id(0)
    n1 = t_ref.shape[2]
    n_sub = t_ref.shape[3]
    f_top = f_ref[:n1, :]
    f_bot = f_ref[n1:, :]
    for j in range(n_sub):
        z = jnp.concatenate(
            [p_ref[0, :, j * cols:(j + 1) * cols], q_ref[0, :, j * cols:(j + 1) * cols]], axis=0)
        valid = step * n_sub + j < n2
        z = jnp.where(valid, z, jnp.zeros_like(z))
        tr = t_ref[0, 0, :, j:j + 1]
        ti = t_ref[0, 1, :, j:j + 1]
        m = jnp.concatenate([tr * f_top - ti * f_bot, ti * f_top + tr * f_bot], axis=0).astype(BF16)
        res = jnp.dot(m, z, preferred_element_type=F32)
        g_ref[0, j, 0, :n1] = res[:n1]
        g_ref[0, j, 1, :n1] = res[n1:]


def _dft2_kernel(f_ref, g_ref, y_ref, *, n2, pitch):
    n1 = y_ref.shape[1]
    n_rows = f_ref.shape[1]
    f = f_ref[...].astype(BF16)

    def body(k1, carry):
        rhs = g_ref[0, pl.ds(k1, n_rows, stride=pitch), :].astype(BF16)
        res = jnp.dot(f, rhs, preferred_element_type=F32)
        y_ref[0, k1] = res[:n2].astype(BF16)
        return carry

    lax.fori_loop(0, n1, body, 0, unroll=16)


def _slab_pitch(n1):
    pitch = -(-n1 // 8) * 8
    return pitch if (pitch // 8) % 2 else pitch + 8


def _sequence_dft(p, q, l):
    b = p.shape[0]
    cols = p.shape[2]
    n1, n2 = _dft_factors(l)
    tn = DFT_N2_TILE
    n2_pad = -(-n2 // tn) * tn
    stage1, twiddle, stage2 = _dft_stage_tables(l, n1, n2, n2_pad)
    twiddle = twiddle.reshape(2, n1, n2_pad // tn, tn).transpose(2, 0, 1, 3)
    pv = p.reshape(b, n1, n2 * cols)
    qv = q.reshape(b, n1, n2 * cols)
    pitch = _slab_pitch(n1)
    est1 = 2 * (4 * n1 * n1 * 4 + 2 * n1 * tn * cols * 2 + tn * 2 * pitch * cols * 4) + 8 * n1 * cols * 4
    g = pl.pallas_call(
        functools.partial(_dft1_kernel, n2=n2, cols=cols),
        grid=(n2_pad // tn, b),
        in_specs=[
            pl.BlockSpec((2 * n1, 2 * n1), lambda i, bi: (0, 0)),
            pl.BlockSpec((1, 2, n1, tn), lambda i, bi: (i, 0, 0, 0)),
            pl.BlockSpec((1, n1, tn * cols), lambda i, bi: (bi, 0, i)),
            pl.BlockSpec((1, n1, tn * cols), lambda i, bi: (bi, 0, i)),
        ],
        out_specs=pl.BlockSpec((1, tn, 2, pitch, cols), lambda i, bi: (bi, i, 0, 0, 0)),
        out_shape=jax.ShapeDtypeStruct((b, n2_pad, 2, pitch, cols), F32),
        compiler_params=pltpu.CompilerParams(
            dimension_semantics=("arbitrary", "arbitrary"), vmem_limit_bytes=_vmem_limit(est1)),
    )(jnp.asarray(stage1), jnp.asarray(twiddle), pv, qv)

    gv = g.reshape(b, 2 * n2_pad * pitch, cols)
    lanes = DFT_LANE_TILE
    est2 = 2 * (2 * n2_pad * n2_pad * 4 + 2 * n2_pad * pitch * lanes * 4 + n1 * n2_pad * lanes * 2) + 8 * n2_pad * lanes * 4
    y = pl.pallas_call(
        functools.partial(_dft2_kernel, n2=n2, pitch=pitch),
        grid=(b, cols // lanes),
        in_specs=[
            pl.BlockSpec((n2_pad, 2 * n2_pad), lambda bi, i: (0, 0)),
            pl.BlockSpec((1, 2 * n2_pad * pitch, lanes), lambda bi, i: (bi, 0, i)),
        ],
        out_specs=pl.BlockSpec((1, n1, n2, lanes), lambda bi, i: (bi, 0, 0, i)),
        out_shape=jax.ShapeDtypeStruct((b, n1, n2, cols), BF16),
        compiler_params=pltpu.CompilerParams(
            dimension_semantics=("arbitrary", "arbitrary"), vmem_limit_bytes=_vmem_limit(est2)),
    )(jnp.asarray(stage2), gv)
    return y.transpose(0, 2, 1, 3).reshape(b, l, cols)


def _out_mlp_kernel(
    x_ref, a_ref, f_ref, ga_ref, gf_ref, wout_ref, gmlp_ref, wup_ref, wdown_ref, gfin_ref, o_ref,
):
    an = _rms(a_ref[0], ga_ref[...]).astype(BF16)
    fn = _rms(f_ref[0].astype(F32), gf_ref[...]).astype(BF16)
    h = x_ref[0]
    h = h + jnp.dot(an, wout_ref[:ATTN_W, :], preferred_element_type=F32)
    h = h + jnp.dot(fn, wout_ref[ATTN_W:, :], preferred_element_type=F32)
    m = _rms(h, gmlp_ref[...]).astype(BF16)
    act = jnp.maximum(jnp.dot(m, wup_ref[...], preferred_element_type=F32), 0.0)
    out = h + jnp.dot((act * act).astype(BF16), wdown_ref[...], preferred_element_type=F32)
    o_ref[0] = _rms(out, gfin_ref[...])


def _out_mlp(x, attn, four, g_attn_out, g_fourier_out, w_out, g_mlp, w_up, w_down, g_final):
    b, n_tok, d_model = x.shape
    d_ff = w_up.shape[1]
    t = TOKEN_TILE
    const2 = lambda bi, i: (0, 0)
    tile3 = lambda bi, i: (bi, i, 0)
    est = (2 * (2 * t * d_model * 4 + t * ATTN_W * 4 + t * FOURIER_W * 4)
           + 2 * (d_model * d_model + 2 * d_model * d_ff) * 2 + t * (4 * d_model * 4 + d_ff * 6))
    return pl.pallas_call(
        _out_mlp_kernel,
        grid=(b, n_tok // t),
        in_specs=[
            pl.BlockSpec((1, t, d_model), tile3),
            pl.BlockSpec((1, t, ATTN_W), tile3),
            pl.BlockSpec((1, t, FOURIER_W), tile3),
            pl.BlockSpec((1, ATTN_W), const2),
            pl.BlockSpec((1, FOURIER_W), const2),
            pl.BlockSpec((d_model, d_model), const2),
            pl.BlockSpec((1, d_model), const2),
            pl.BlockSpec((d_model, d_ff), const2),
            pl.BlockSpec((d_ff, d_model), const2),
            pl.BlockSpec((1, d_model), const2),
        ],
        out_specs=pl.BlockSpec((1, t, d_model), tile3),
        out_shape=jax.ShapeDtypeStruct((b, n_tok, d_model), F32),
        compiler_params=pltpu.CompilerParams(
            dimension_semantics=("arbitrary", "arbitrary"), vmem_limit_bytes=_vmem_limit(est)),
    )(x, attn, four, g_attn_out, g_fourier_out, w_out, g_mlp, w_up, w_down, g_final)


def kernel(x, meta_tokens, g_mix, w_in, g_q, g_k, w_fourier, g_attn_out, g_fourier_out, w_out,
           g_mlp, w_up, w_down, g_final):
    assert g_mix.shape[0] == 1, "meta-token rows are only carried for a single layer"
    b, n_tok, d_model = x.shape
    l = n_tok + N_META
    t = min(IN_PROJ_TILE, n_tok)
    assert n_tok % t == 0 and n_tok % TOKEN_TILE == 0 and n_tok % GRID_W == 0

    cos_tab, sin_tab = _rope_tables(n_tok, n_tok + t)
    meta_pad = jnp.pad(meta_tokens.astype(F32), ((0, t - N_META), (0, 0)))
    ab = _fourier_weights(w_fourier[0], l)
    q, k, v, p, qq = _in_proj(
        x, meta_pad, cos_tab, sin_tab, g_mix, w_in[0].astype(BF16), g_q, g_k, ab, l)
    attn = _attention(q, k, v, g_q, g_k, n_tok)
    four = _sequence_dft(p, qq, l)
    return _out_mlp(
        x, attn, four, g_attn_out, g_fourier_out, w_out[0].astype(BF16), g_mlp,
        w_up[0].astype(BF16), w_down[0].astype(BF16), g_final[None, :])
```
